```python
import jax, jax.numpy as jnp
from jax import lax
import numpy as np

D_MODEL = 2048
BATCH = 2
SEQ = 4096
DEPTH = 2

N_EVEN = (DEPTH + 1) // 2
N_ODD = DEPTH // 2
EPS = 1e-6
CHUNK = 64

A_HEADS = 4
A_QK_DIM = 128
A_V_DIM = 256
A_GATE_CAP = 15.0
A_QK = A_HEADS * A_QK_DIM
A_V = A_HEADS * A_V_DIM
B_HEADS = 8
B_HEAD_DIM = 128
B_CONV = 4
B_W = B_HEADS * B_HEAD_DIM
MIX_WIDTH = A_V + B_W
AB_SPLITS = (A_QK, A_QK, A_V, A_V, A_HEADS, A_HEADS, B_W, B_W, B_W, B_W, B_HEADS, B_HEADS)
N_IN_AB = 2 * A_QK + 2 * A_V + 2 * A_HEADS + 4 * B_W + 2 * B_HEADS

SSM_WIDTH = D_MODEL
SSM_GROUP = 16
SSM_GROUPS = SSM_WIDTH // SSM_GROUP
SSM_STATE = 64

MOE_GROUPS = 8
MOE_PER_GROUP = 8
MOE_EXPERTS = MOE_GROUPS * MOE_PER_GROUP
MOE_TOPK = 2
MOE_FF = 768
MOE_BLOCK = 128

kernel_name = "hybrid_mlstm_gdn_s5_hmoe"


def rmsnorm(x, g):
    xf = x.astype(jnp.float32)
    y = xf * lax.rsqrt(jnp.mean(xf * xf, axis=-1, keepdims=True) + EPS)
    return (y * g.astype(jnp.float32)).astype(x.dtype)


def head_rmsnorm(t, g):
    return t * lax.rsqrt(jnp.mean(t * t, axis=-1, keepdims=True) + EPS) * g.astype(jnp.float32)


def l2norm(t):
    return t * lax.rsqrt(jnp.sum(t * t, axis=-1, keepdims=True) + EPS)


def split_heads(t, n):
    b, s, _ = t.shape
    return t.reshape(b, s, n, -1).transpose(0, 2, 1, 3)


def merge_heads(t):
    b, n, s, d = t.shape
    return t.transpose(0, 2, 1, 3).reshape(b, s, n * d)


def softcap(t, cap):
    return cap * jnp.tanh(t / cap)


def causal_depthwise_conv(x, w):
    k, c = w.shape
    return lax.conv_general_dilated(x, w[:, None, :], window_strides=(1,), padding=[(k - 1, 0)],
                                    dimension_numbers=('NWC', 'WIO', 'NWC'), feature_group_count=c)


def to_chunk_major(t):
    return jnp.moveaxis(t, 2, 0)


def from_chunk_major(t):
    return jnp.moveaxis(t, 0, 2)


def mlstm_chunked(q, k, v, i_pre, f_pre):
    bsz, nh, s, dk = q.shape
    dv = v.shape[-1]
    nc = s // CHUNK
    q = q.reshape(bsz, nh, nc, CHUNK, dk)
    k = k.reshape(bsz, nh, nc, CHUNK, dk)
    v = v.reshape(bsz, nh, nc, CHUNK, dv)
    i_pre = i_pre.reshape(bsz, nh, nc, CHUNK)
    logf = jax.nn.log_sigmoid(f_pre).reshape(bsz, nh, nc, CHUNK)
    b = jnp.cumsum(logf, axis=-1)
    b_last = b[..., -1]
    causal = jnp.tril(jnp.ones((CHUNK, CHUNK), dtype=bool))
    log_d = jnp.where(causal, b[..., :, None] - b[..., None, :] + i_pre[..., None, :], -jnp.inf)
    log_end = b_last[..., None] - b + i_pre

    def step(carry, inp):
        c_mat, n_vec, m = carry
        k_c, v_c, le_c, bl_c = inp
        m_new = jnp.maximum(bl_c + m, jnp.max(le_c, axis=-1))
        carry_scale = jnp.exp(bl_c + m - m_new)
        wt = jnp.exp(le_c - m_new[..., None])
        c_new = c_mat * carry_scale[..., None, None] + jnp.einsum('bhl,bhld,bhle->bhde', wt, k_c, v_c)
        n_new = n_vec * carry_scale[..., None] + jnp.einsum('bhl,bhld->bhd', wt, k_c)
        return (c_new, n_new, m_new), (c_mat, n_vec, m)

    init = (jnp.zeros((bsz, nh, dk, dv), jnp.float32), jnp.zeros((bsz, nh, dk), jnp.float32),
            jnp.zeros((bsz, nh), jnp.float32))
    _, (c_s, n_s, m_s) = lax.scan(step, init, (to_chunk_major(k), to_chunk_major(v),
                                               to_chunk_major(log_end), to_chunk_major(b_last)))
    c_s, n_s, m_s = from_chunk_major(c_s), from_chunk_major(n_s), from_chunk_major(m_s)
    log_inter = b + m_s[..., None]
    m_t = jnp.maximum(log_inter, jnp.max(log_d, axis=-1))
    scores = jnp.einsum('bhntd,bhnsd->bhnts', q, k) * jnp.exp(log_d - m_t[..., None])
    inter = jnp.exp(log_inter - m_t)
    num = inter[..., None] * jnp.einsum('bhntd,bhnde->bhnte', q, c_s) + jnp.einsum('bhnts,bhnse->bhnte', scores, v)
    den = inter * jnp.einsum('bhntd,bhnd->bhnt', q, n_s) + jnp.sum(scores, axis=-1)
    h = num / jnp.maximum(jnp.abs(den), jnp.exp(-m_t))[..., None]
    return h.reshape(bsz, nh, s, dv)


def gated_delta_chunked(q, k, v, g, beta):
    bsz, nh, s, dk = q.shape
    dv = v.shape[-1]
    nc = s // CHUNK
    q = q.reshape(bsz, nh, nc, CHUNK, dk)
    k = k.reshape(bsz, nh, nc, CHUNK, dk)
    v = v.reshape(bsz, nh, nc, CHUNK, dv)
    g = g.reshape(bsz, nh, nc, CHUNK)
    beta = beta.reshape(bsz, nh, nc, CHUNK)
    decay = jnp.cumsum(g, axis=-1)
    causal = jnp.tril(jnp.ones((CHUNK, CHUNK), dtype=bool))
    strict = jnp.tril(jnp.ones((CHUNK, CHUNK), dtype=bool), k=-1)
    seg = jnp.exp(jnp.where(causal, decay[..., :, None] - decay[..., None, :], -jnp.inf))
    k_beta = k * beta[..., None]
    a_low = jnp.where(strict, jnp.einsum('bhntd,bhnsd->bhnts', k_beta, k) * seg, 0.0)
    eye = jnp.eye(CHUNK, dtype=jnp.float32)
    t_mat = lax.linalg.triangular_solve(a_low + eye, jnp.broadcast_to(eye, a_low.shape),
                                        left_side=True, lower=True, unit_diagonal=True)
    u = jnp.matmul(t_mat, v * beta[..., None])
    w = jnp.matmul(t_mat, k_beta * jnp.exp(decay)[..., None])
    attn = jnp.einsum('bhntd,bhnsd->bhnts', q, k) * seg
    q_dec = q * jnp.exp(decay)[..., None]
    d_last = decay[..., -1]
    k_end = k * jnp.exp(d_last[..., None] - decay)[..., None]

    def step(state, inp):
        q_c, w_c, u_c, a_c, k_c, dl_c = inp
        v_new = u_c - jnp.matmul(w_c, state)
        o = jnp.matmul(q_c, state) + jnp.matmul(a_c, v_new)
        state = state * jnp.exp(dl_c)[..., None, None] + jnp.einsum('bhld,bhle->bhde', k_c, v_new)
        return state, o

    _, o = lax.scan(step, jnp.zeros((bsz, nh, dk, dv), jnp.float32),
                    (to_chunk_major(q_dec), to_chunk_major(w), to_chunk_major(u), to_chunk_major(attn),
                     to_chunk_major(k_end), to_chunk_major(d_last)))
    return from_chunk_major(o).reshape(bsz, nh, s, dv)


def mixer_ab(h, w_in, a_i_bias, a_f_bias, a_norm, b_conv, b_a_log, b_dt_bias, b_norm, w_out):
    f32 = jnp.float32
    proj = h @ w_in
    idx = np.cumsum(AB_SPLITS)[:-1].tolist()
    aq, ak, av, ao, ai, af, bq, bk, bv, bz, bb, ba = jnp.split(proj, idx, axis=-1)
    q = split_heads(aq, A_HEADS).astype(f32)
    k = split_heads(ak, A_HEADS).astype(f32) * (A_QK_DIM ** -0.5)
    v = split_heads(av, A_HEADS).astype(f32)
    i_pre = softcap(ai.astype(f32) + a_i_bias.astype(f32), A_GATE_CAP).transpose(0, 2, 1)
    f_pre = softcap(af.astype(f32) + a_f_bias.astype(f32), A_GATE_CAP).transpose(0, 2, 1)
    ha = mlstm_chunked(q, k, v, i_pre, f_pre)
    ha = head_rmsnorm(ha, a_norm.reshape(A_HEADS, 1, A_V_DIM))
    ya = merge_heads(ha) * jax.nn.sigmoid(ao.astype(f32))
    qkv = jnp.concatenate([bq, bk, bv], axis=-1).astype(f32)
    qkv = jax.nn.silu(causal_depthwise_conv(qkv, b_conv.astype(f32)))
    cq, ck, cv = jnp.split(qkv, 3, axis=-1)
    q = l2norm(split_heads(cq, B_HEADS)) * (B_HEAD_DIM ** -0.5)
    k = l2norm(split_heads(ck, B_HEADS))
    v = split_heads(cv, B_HEADS)
    beta = jax.nn.sigmoid(bb.astype(f32)).transpose(0, 2, 1)
    g = (-jnp.exp(b_a_log.astype(f32)) * jax.nn.softplus(ba.astype(f32) + b_dt_bias.astype(f32))).transpose(0, 2, 1)
    hb = gated_delta_chunked(q, k, v, g, beta)
    hb = head_rmsnorm(hb, b_norm)
    yb = merge_heads(hb) * jax.nn.silu(bz.astype(f32))
    y = jnp.concatenate([ya, yb], axis=-1).astype(h.dtype)
    return y @ w_out


def ssm_combine(e1, e2):
    a1r, a1i, b1r, b1i = e1
    a2r, a2i, b2r, b2i = e2
    return (a1r * a2r - a1i * a2i, a1r * a2i + a1i * a2r,
            a2r * b1r - a2i * b1i + b2r, a2r * b1i + a2i * b1r + b2i)


def mixer_s5(h, lam_re, lam_im, log_step, b_re, b_im, c_re, c_im, d_skip, w_glu, b_glu):
    f32 = jnp.float32
    bsz, s, width = h.shape
    u = h.astype(f32)
    ug = u.reshape(bsz, s, SSM_GROUPS, SSM_GROUP)
    lr = jnp.minimum(lam_re.astype(f32), -1e-4)
    li = lam_im.astype(f32)
    step = jnp.exp(log_step.astype(f32))[:, None]
    mag = jnp.exp(lr * step)
    ang = li * step
    ab_re = mag * jnp.cos(ang)
    ab_im = mag * jnp.sin(ang)
    den = lr * lr + li * li
    zr = ab_re - 1.0
    f_re = (zr * lr + ab_im * li) / den
    f_im = (ab_im * lr - zr * li) / den
    br, bi = b_re.astype(f32), b_im.astype(f32)
    bb_re = f_re[..., None] * br - f_im[..., None] * bi
    bb_im = f_re[..., None] * bi + f_im[..., None] * br
    bu_re = jnp.einsum('bsgw,gpw->bsgp', ug, bb_re)
    bu_im = jnp.einsum('bsgw,gpw->bsgp', ug, bb_im)
    a_re = jnp.broadcast_to(ab_re, (1, s) + ab_re.shape)
    a_im = jnp.broadcast_to(ab_im, (1, s) + ab_im.shape)
    _, _, x_re, x_im = lax.associative_scan(ssm_combine, (a_re, a_im, bu_re, bu_im), axis=1)
    y = (jnp.einsum('bsgp,gwp->bsgw', x_re, c_re.astype(f32))
         - jnp.einsum('bsgp,gwp->bsgw', x_im, c_im.astype(f32)))
    y = y.reshape(bsz, s, width) + d_skip.astype(f32) * u
    z = jax.nn.gelu(y).astype(h.dtype) @ w_glu + b_glu
    val, gate = jnp.split(z, 2, axis=-1)
    return val * jax.nn.sigmoid(gate)


def grouped_experts(xt, expert, gate, w_gate, w_up, w_down):
    t, d = xt.shape
    n_exp = w_gate.shape[0]
    kk = expert.shape[1]
    n_assign = t * kk
    n_blocks = (n_assign + n_exp * (MOE_BLOCK - 1)) // MOE_BLOCK
    rows = n_blocks * MOE_BLOCK
    flat_e = expert.reshape(n_assign)
    flat_tok = jnp.repeat(jnp.arange(t, dtype=jnp.int32), kk)
    flat_w = gate.reshape(n_assign)
    order = jnp.argsort(flat_e)
    sorted_e = flat_e[order]
    counts = jnp.bincount(flat_e, length=n_exp)
    starts = jnp.cumsum(counts) - counts
    padded = (counts + MOE_BLOCK - 1) // MOE_BLOCK * MOE_BLOCK
    padded_ends = jnp.cumsum(padded)
    padded_starts = padded_ends - padded
    dest = padded_starts[sorted_e] + jnp.arange(n_assign) - starts[sorted_e]
    row_tok = jnp.zeros((rows,), jnp.int32).at[dest].set(flat_tok[order])
    row_w = jnp.zeros((rows,), jnp.float32).at[dest].set(flat_w[order])
    block_start = jnp.arange(n_blocks) * MOE_BLOCK
    block_expert = jnp.minimum(jnp.searchsorted(padded_ends, block_start, side='right'), n_exp - 1)

    def run_block(args):
        tok, e = args
        xb = xt[tok]
        hb = jax.nn.silu(xb @ w_gate[e]) * (xb @ w_up[e])
        return hb @ w_down[e]

    yb = lax.map(run_block, (row_tok.reshape(n_blocks, MOE_BLOCK), block_expert))
    yb = yb.reshape(rows, d) * row_w[:, None].astype(yb.dtype)
    return jnp.zeros((t, d), yb.dtype).at[row_tok].add(yb)


def hier_moe(h, w_rg, b_rg, w_re, b_re, w_gate, w_up, w_down):
    bsz, s, d = h.shape
    t = bsz * s
    xt = h.reshape(t, d)
    g_logits = (xt @ w_rg).astype(jnp.float32) + b_rg.astype(jnp.float32)
    g_prob = jax.nn.softmax(g_logits, axis=-1)
    _, g_idx = lax.top_k(g_logits, 1)
    g_w = jnp.take_along_axis(g_prob, g_idx, axis=-1)
    e_logits = ((xt @ w_re).astype(jnp.float32) + b_re.astype(jnp.float32)).reshape(t, MOE_GROUPS, MOE_PER_GROUP)
    e_in_group = e_logits[jnp.arange(t), g_idx[:, 0]]
    e_top, e_idx = lax.top_k(e_in_group, MOE_TOPK)
    gate = g_w * jax.nn.softmax(e_top, axis=-1)
    expert = g_idx * MOE_PER_GROUP + e_idx
    return grouped_experts(xt, expert, gate, w_gate, w_up, w_down).reshape(bsz, s, d)


def setup_inputs(seed: int = 0) -> dict:
    key = jax.random.key(seed)
    ks = jax.random.split(key, 32)
    nrm = jax.random.normal
    uni = jax.random.uniform
    d = D_MODEL
    f_bias_base = jnp.linspace(3.0, 6.0, A_HEADS, dtype=jnp.float32)
    dt = jnp.exp(uni(ks[8], (N_EVEN, B_HEADS), minval=np.log(1e-3), maxval=np.log(1e-1)))
    lam_im_base = jnp.pi * jnp.arange(SSM_STATE, dtype=jnp.float32)
    return {
        "x": nrm(ks[0], (BATCH, SEQ, d), jnp.float32),
        "mix_norm": 1.0 + 0.01 * nrm(ks[1], (DEPTH, d)),
        "ab_w_in": nrm(ks[2], (N_EVEN, d, N_IN_AB)) * d ** -0.5,
        "mlstm_i_bias": 0.1 * nrm(ks[3], (N_EVEN, A_HEADS)),
        "mlstm_f_bias": f_bias_base + 0.1 * nrm(ks[4], (N_EVEN, A_HEADS)),
        "mlstm_norm": 1.0 + 0.01 * nrm(ks[5], (N_EVEN, A_V)),
        "gdn_conv": nrm(ks[6], (N_EVEN, B_CONV, 3 * B_W)) * B_CONV ** -0.5,
        "gdn_a_log": jnp.log(uni(ks[7], (N_EVEN, B_HEADS), minval=1.0, maxval=16.0)),
        "gdn_dt_bias": dt + jnp.log(-jnp.expm1(-dt)),
        "gdn_norm": 1.0 + 0.01 * nrm(ks[9], (N_EVEN, B_HEAD_DIM)),
        "ab_w_out": nrm(ks[10], (N_EVEN, MIX_WIDTH, d)) * MIX_WIDTH ** -0.5,
        "ssm_lambda_re": -0.5 + 0.01 * nrm(ks[11], (N_ODD, SSM_GROUPS, SSM_STATE)),
        "ssm_lambda_im": lam_im_base + 0.01 * nrm(ks[12], (N_ODD, SSM_GROUPS, SSM_STATE)),
        "ssm_log_step": uni(ks[13], (N_ODD, SSM_GROUPS), minval=np.log(1e-3), maxval=np.log(1e-1)),
        "ssm_b_re": nrm(ks[14], (N_ODD, SSM_GROUPS, SSM_STATE, SSM_GROUP)) * (2 * SSM_GROUP) ** -0.5,
        "ssm_b_im": nrm(ks[15], (N_ODD, SSM_GROUPS, SSM_STATE, SSM_GROUP)) * (2 * SSM_GROUP) ** -0.5,
        "ssm_c_re": nrm(ks[16], (N_ODD, SSM_GROUPS, SSM_GROUP, SSM_STATE)) * SSM_STATE ** -0.5,
        "ssm_c_im": nrm(ks[17], (N_ODD, SSM_GROUPS, SSM_GROUP, SSM_STATE)) * SSM_STATE ** -0.5,
        "ssm_d": nrm(ks[18], (N_ODD, SSM_WIDTH)),
        "glu_w": nrm(ks[19], (N_ODD, SSM_WIDTH, 2 * d)) * SSM_WIDTH ** -0.5,
        "glu_b": 0.01 * nrm(ks[20], (N_ODD, 2 * d)),
        "ffn_norm": 1.0 + 0.01 * nrm(ks[21], (DEPTH, d)),
        "router_group_w": nrm(ks[22], (DEPTH, d, MOE_GROUPS)) * d ** -0.5,
        "router_group_b": 0.01 * nrm(ks[23], (DEPTH, MOE_GROUPS)),
        "router_expert_w": nrm(ks[24], (DEPTH, d, MOE_EXPERTS)) * d ** -0.5,
        "router_expert_b": 0.01 * nrm(ks[25], (DEPTH, MOE_EXPERTS)),
        "expert_w_gate": nrm(ks[26], (DEPTH, MOE_EXPERTS, d, MOE_FF)) * d ** -0.5,
        "expert_w_up": nrm(ks[27], (DEPTH, MOE_EXPERTS, d, MOE_FF)) * d ** -0.5,
        "expert_w_down": nrm(ks[28], (DEPTH, MOE_EXPERTS, MOE_FF, d)) * MOE_FF ** -0.5,
        "final_norm": 1.0 + 0.01 * nrm(ks[29], (d,)),
    }


def reference(x, mix_norm, ab_w_in, mlstm_i_bias, mlstm_f_bias, mlstm_norm, gdn_conv, gdn_a_log,
              gdn_dt_bias, gdn_norm, ab_w_out, ssm_lambda_re, ssm_lambda_im, ssm_log_step, ssm_b_re,
              ssm_b_im, ssm_c_re, ssm_c_im, ssm_d, glu_w, glu_b, ffn_norm, router_group_w,
              router_group_b, router_expert_w, router_expert_b, expert_w_gate, expert_w_up,
              expert_w_down, final_norm):
    for layer in range(DEPTH):
        j = layer // 2
        h = rmsnorm(x, mix_norm[layer])
        if layer % 2 == 0:
            x = x + mixer_ab(h, ab_w_in[j], mlstm_i_bias[j], mlstm_f_bias[j], mlstm_norm[j], gdn_conv[j],
                             gdn_a_log[j], gdn_dt_bias[j], gdn_norm[j], ab_w_out[j]).astype(x.dtype)
        else:
            x = x + mixer_s5(h, ssm_lambda_re[j], ssm_lambda_im[j], ssm_log_step[j], ssm_b_re[j], ssm_b_im[j],
                             ssm_c_re[j], ssm_c_im[j], ssm_d[j], glu_w[j], glu_b[j]).astype(x.dtype)
        x = x + hier_moe(rmsnorm(x, ffn_norm[layer]), router_group_w[layer], router_group_b[layer],
                         router_expert_w[layer], router_expert_b[layer], expert_w_gate[layer],
                         expert_w_up[layer], expert_w_down[layer]).astype(x.dtype)
    return rmsnorm(x, final_norm)
```

```python
import functools

import jax
import jax.numpy as jnp
from jax import lax
from jax.experimental import pallas as pl
from jax.experimental.pallas import tpu as pltpu

F32 = jnp.float32
BF16 = jnp.bfloat16
HIGHEST = lax.Precision.HIGHEST

EPS = 1e-6
D_MODEL = 2048
A_HEADS = 4
A_QK_DIM = 128
A_V_DIM = 256
A_GATE_CAP = 15.0
B_HEADS = 8
B_HEAD_DIM = 128
B_CONV = 4
SSM_GROUP = 16
SSM_STATE = 64
MOE_GROUPS = 8
MOE_PER_GROUP = 8
MOE_EXPERTS = 64
MOE_FF = 768

LANES = 128
SUBLANES = 8
VMEM_LIMIT = 56 * 1024 * 1024

COL_AQ, COL_AK, COL_AV, COL_AO = 0, 512, 1024, 2048
COL_BQ, COL_BK, COL_BV, COL_BZ = 3072, 4096, 5120, 6144
COL_GATES = 7168
N_PROJ = 7296
G_AI, G_AF, G_BB, G_BA = 0, 4, 8, 16
GATE_ROWS = 32

MLSTM_CHUNK = 256
GDN_BLOCK = 256
GDN_CHUNK = 64
S5_BLOCK = 512
S5_SEG = S5_BLOCK // SUBLANES
S5_GROUPS_PER_BLOCK = 8
S5_NSTATE = S5_GROUPS_PER_BLOCK * SSM_STATE
MOE_ROWS = 512
MOE_SUB = 128
MOE_FT = 256


def _cparams(sem):
    return pltpu.CompilerParams(dimension_semantics=sem, vmem_limit_bytes=VMEM_LIMIT)


def _softcap(t, cap):
    return cap * jnp.tanh(t / cap)


def _log_sigmoid(t):
    return jnp.minimum(t, 0.0) - jnp.log(1.0 + jnp.exp(-jnp.abs(t)))


def _softplus(t):
    return jnp.maximum(t, 0.0) + jnp.log(1.0 + jnp.exp(-jnp.abs(t)))


def _sigmoid(t):
    return 1.0 / (1.0 + jnp.exp(-t))


def _silu(t):
    return t * _sigmoid(t)


def _pick_col(x, idx):
    lane = lax.broadcasted_iota(jnp.int32, x.shape, 1)
    return jnp.sum(jnp.where(lane == idx, x, 0.0), axis=-1, keepdims=True)


def _dot(a, b):
    return jnp.dot(a, b, preferred_element_type=F32)


def _dot_nt(a, b):
    return lax.dot_general(a, b, (((1,), (1,)), ((), ())), preferred_element_type=F32)


def _inproj_kernel(x_ref, g_ref, w_ref, o_ref, h_scr):
    @pl.when(pl.program_id(1) == 0)
    def _():
        x = x_ref[...]
        ms = jnp.mean(x * x, axis=-1, keepdims=True)
        h_scr[...] = (x * lax.rsqrt(ms + EPS) * g_ref[...]).astype(BF16)

    o_ref[...] = _dot(h_scr[...], w_ref[...])


def _inproj(x2, g, w_bf):
    t, d = x2.shape
    n = w_bf.shape[1]
    tm = min(1024, t)
    tn = 384
    return pl.pallas_call(
        _inproj_kernel,
        grid=(t // tm, n // tn),
        in_specs=[pl.BlockSpec((tm, d), lambda i, j: (i, 0)),
                  pl.BlockSpec((1, d), lambda i, j: (0, 0)),
                  pl.BlockSpec((d, tn), lambda i, j: (0, j))],
        out_specs=pl.BlockSpec((tm, tn), lambda i, j: (i, j)),
        out_shape=jax.ShapeDtypeStruct((t, n), F32),
        scratch_shapes=[pltpu.VMEM((tm, d), BF16)],
        compiler_params=_cparams(("parallel", "arbitrary")),
        name="inproj",
    )(x2, g, w_bf)


def _mlstm_kernel(q_ref, k_ref, v_ref, o_ref, gc_ref, gr_ref, bc_ref, br_ref, nw_ref, out_ref,
                  c_scr, n_scr, m_scr):
    h = pl.program_id(1)
    c = pl.program_id(2)
    L = q_ref.shape[0]

    @pl.when(c == 0)
    def _():
        c_scr[...] = jnp.zeros_like(c_scr)
        n_scr[...] = jnp.zeros_like(n_scr)
        m_scr[...] = jnp.zeros_like(m_scr)

    row = lax.broadcasted_iota(jnp.int32, (L, L), 0)
    col = lax.broadcasted_iota(jnp.int32, (L, L), 1)
    causal = row >= col
    tril = causal.astype(F32)
    triu = (row <= col).astype(F32)

    gcol = _softcap(gc_ref[...] + bc_ref[...], A_GATE_CAP)
    i_col = _pick_col(gcol, G_AI + h)
    b_all = jnp.dot(tril, _log_sigmoid(gcol), precision=HIGHEST, preferred_element_type=F32)
    b_col = _pick_col(b_all, G_AF + h)
    i_row = _softcap(gr_ref[pl.ds(G_AI + h, 1), :] + br_ref[pl.ds(G_AI + h, 1), :], A_GATE_CAP)
    f_row = _softcap(gr_ref[pl.ds(G_AF + h, 1), :] + br_ref[pl.ds(G_AF + h, 1), :], A_GATE_CAP)
    logf8 = jnp.broadcast_to(_log_sigmoid(f_row), (SUBLANES, L))
    b_row = jnp.dot(logf8, triu, precision=HIGHEST, preferred_element_type=F32)[0:1, :]
    b_last = b_col[L - 1:L, :]

    m_prev = m_scr[...]
    log_d = jnp.where(causal, b_col - b_row + i_row, -jnp.inf)
    log_inter = b_col + m_prev
    m_t = jnp.maximum(log_inter, jnp.max(log_d, axis=-1, keepdims=True))
    dmat = jnp.exp(log_d - m_t)
    inter = jnp.exp(log_inter - m_t)

    q = q_ref[...]
    k = k_ref[...] * (A_QK_DIM ** -0.5)
    v = v_ref[...]
    qb = q.astype(BF16)
    kb = k.astype(BF16)
    vb = v.astype(BF16)
    scores = _dot_nt(qb, kb) * dmat
    c_mat = c_scr[...]
    n_vec = n_scr[...]
    num = inter * _dot(qb, c_mat.astype(BF16)) + _dot(scores.astype(BF16), vb)
    den = inter * jnp.sum(q * n_vec, axis=-1, keepdims=True) + jnp.sum(scores, axis=-1, keepdims=True)
    hh = num / jnp.maximum(jnp.abs(den), jnp.exp(-m_t))
    hh = hh * lax.rsqrt(jnp.mean(hh * hh, axis=-1, keepdims=True) + EPS) * nw_ref[...]
    out_ref[...] = hh * _sigmoid(o_ref[...])

    le_col = b_last - b_col + i_col
    m_new = jnp.maximum(b_last + m_prev, jnp.max(le_col, axis=0, keepdims=True))
    carry_scale = jnp.exp(b_last + m_prev - m_new)
    kw = k * jnp.exp(le_col - m_new)
    c_scr[...] = c_mat * carry_scale + _dot(kw.T.astype(BF16), vb)
    n_scr[...] = n_vec * carry_scale + jnp.sum(kw, axis=0, keepdims=True)
    m_scr[...] = m_new


def _mlstm(proj, gates_t, bias_col, bias_row, norm_w):
    b, s, _ = proj.shape
    L = min(MLSTM_CHUNK, s)
    qb, kb = COL_AQ // A_QK_DIM, COL_AK // A_QK_DIM
    vb, ob = COL_AV // A_V_DIM, COL_AO // A_V_DIM
    gb = COL_GATES // LANES
    return pl.pallas_call(
        _mlstm_kernel,
        grid=(b, A_HEADS, s // L),
        in_specs=[pl.BlockSpec((None, L, A_QK_DIM), lambda i, h, c: (i, c, qb + h)),
                  pl.BlockSpec((None, L, A_QK_DIM), lambda i, h, c: (i, c, kb + h)),
                  pl.BlockSpec((None, L, A_V_DIM), lambda i, h, c: (i, c, vb + h)),
                  pl.BlockSpec((None, L, A_V_DIM), lambda i, h, c: (i, c, ob + h)),
                  pl.BlockSpec((None, L, LANES), lambda i, h, c: (i, c, gb)),
                  pl.BlockSpec((None, GATE_ROWS, L), lambda i, h, c: (i, 0, c)),
                  pl.BlockSpec((1, LANES), lambda i, h, c: (0, 0)),
                  pl.BlockSpec((GATE_ROWS, 1), lambda i, h, c: (0, 0)),
                  pl.BlockSpec((1, A_V_DIM), lambda i, h, c: (0, h))],
        out_specs=pl.BlockSpec((None, L, A_V_DIM), lambda i, h, c: (i, c, h)),
        out_shape=jax.ShapeDtypeStruct((b, s, A_HEADS * A_V_DIM), F32),
        scratch_shapes=[pltpu.VMEM((A_QK_DIM, A_V_DIM), F32),
                        pltpu.VMEM((1, A_QK_DIM), F32),
                        pltpu.VMEM((1, 1), F32)],
        compiler_params=_cparams(("parallel", "parallel", "arbitrary")),
        name="mlstm",
    )(proj, proj, proj, proj, proj, gates_t, bias_col, bias_row, norm_w)


def _causal_conv_silu(x, tail, w):
    L = x.shape[0]
    row8 = lax.broadcasted_iota(jnp.int32, (SUBLANES, x.shape[1]), 0)
    acc = x * w[B_CONV - 1:B_CONV, :]
    for d in range(1, B_CONV):
        rolled = pltpu.roll(x, d, 0)
        head = jnp.where(row8 < d, pltpu.roll(tail, d, 0), rolled[0:SUBLANES, :])
        shifted = jnp.concatenate([head, rolled[SUBLANES:, :]], axis=0)
        acc = acc + shifted * w[B_CONV - 1 - d:B_CONV - d, :]
    return _silu(acc)


def _l2norm(t):
    return t * lax.rsqrt(jnp.sum(t * t, axis=-1, keepdims=True) + EPS)


def _gdn_kernel(q_ref, k_ref, v_ref, z_ref, gc_ref, gr_ref, bc_ref, br_ref, alc_ref, alr_ref,
                wq_ref, wk_ref, wv_ref, nw_ref, out_ref, s_scr, tq_scr, tk_scr, tv_scr):
    h = pl.program_id(1)
    c = pl.program_id(2)
    Lb = q_ref.shape[0]
    C = GDN_CHUNK
    nsub = Lb // C

    @pl.when(c == 0)
    def _():
        s_scr[...] = jnp.zeros_like(s_scr)
        tq_scr[...] = jnp.zeros_like(tq_scr)
        tk_scr[...] = jnp.zeros_like(tk_scr)
        tv_scr[...] = jnp.zeros_like(tv_scr)

    xq, xk, xv = q_ref[...], k_ref[...], v_ref[...]
    cq = _causal_conv_silu(xq, tq_scr[...], wq_ref[...])
    ck = _causal_conv_silu(xk, tk_scr[...], wk_ref[...])
    cv = _causal_conv_silu(xv, tv_scr[...], wv_ref[...])
    tq_scr[...] = xq[Lb - SUBLANES:, :]
    tk_scr[...] = xk[Lb - SUBLANES:, :]
    tv_scr[...] = xv[Lb - SUBLANES:, :]
    q = _l2norm(cq) * (B_HEAD_DIM ** -0.5)
    k = _l2norm(ck)
    v = cv

    row = lax.broadcasted_iota(jnp.int32, (Lb, Lb), 0)
    col = lax.broadcasted_iota(jnp.int32, (Lb, Lb), 1)
    same = (row // C) == (col // C)
    blk_tril = (same & (row >= col)).astype(F32)
    blk_triu = (same & (row <= col)).astype(F32)

    gcol = gc_ref[...]
    beta = _pick_col(_sigmoid(gcol), G_BB + h)
    g_all = -jnp.exp(alc_ref[...]) * _softplus(gcol + bc_ref[...])
    dec_all = jnp.dot(blk_tril, g_all, precision=HIGHEST, preferred_element_type=F32)
    dec_col = _pick_col(dec_all, G_BA + h)
    g_row = (-jnp.exp(alr_ref[pl.ds(G_BA + h, 1), :])
             * _softplus(gr_ref[pl.ds(G_BA + h, 1), :] + br_ref[pl.ds(G_BA + h, 1), :]))
    dec_row = jnp.dot(jnp.broadcast_to(g_row, (SUBLANES, Lb)), blk_triu,
                      precision=HIGHEST, preferred_element_type=F32)[0:1, :]

    r64 = lax.broadcasted_iota(jnp.int32, (C, C), 0)
    c64 = lax.broadcasted_iota(jnp.int32, (C, C), 1)
    causal = r64 >= c64
    strict = r64 > c64
    eye = (r64 == c64).astype(F32)

    kbeta = k * beta
    edec = jnp.exp(dec_col)
    q_dec = (q * edec).astype(BF16)
    vbeta = (v * beta).astype(BF16)
    kbdec = (kbeta * edec).astype(BF16)
    kb16 = k.astype(BF16)
    qb16 = q.astype(BF16)
    kbeta16 = kbeta.astype(BF16)

    segs, nmats = [], []
    for j in range(nsub):
        sl = slice(j * C, (j + 1) * C)
        seg = jnp.exp(jnp.where(causal, dec_col[sl, :] - dec_row[:, sl], -jnp.inf))
        a_low = jnp.where(strict, _dot_nt(kbeta16[sl, :], kb16[sl, :]) * seg, 0.0)
        segs.append(seg)
        nmats.append(-a_low)
    nmat = jnp.stack(nmats, axis=0)
    bmm = functools.partial(jnp.einsum, "bij,bjk->bik", precision=HIGHEST, preferred_element_type=F32)
    tmat = eye[None] + nmat
    npow = nmat
    for _ in range(5):
        npow = bmm(npow, npow)
        tmat = tmat + bmm(tmat, npow)

    state = s_scr[...]
    outs = []
    for j in range(nsub):
        sl = slice(j * C, (j + 1) * C)
        t16 = tmat[j].astype(BF16)
        u = _dot(t16, vbeta[sl, :])
        w = _dot(t16, kbdec[sl, :])
        attn = _dot_nt(qb16[sl, :], kb16[sl, :]) * segs[j]
        s16 = state.astype(BF16)
        v_new = u - _dot(w.astype(BF16), s16)
        o = _dot(q_dec[sl, :], s16) + _dot(attn.astype(BF16), v_new.astype(BF16))
        d_last = dec_col[(j + 1) * C - 1:(j + 1) * C, :]
        k_end = k[sl, :] * jnp.exp(d_last - dec_col[sl, :])
        state = state * jnp.exp(d_last) + _dot(k_end.T.astype(BF16), v_new.astype(BF16))
        outs.append(o)
    s_scr[...] = state
    hb = jnp.concatenate(outs, axis=0)
    hb = hb * lax.rsqrt(jnp.mean(hb * hb, axis=-1, keepdims=True) + EPS) * nw_ref[...]
    out_ref[...] = hb * _silu(z_ref[...])


def _gdn(proj, gates_t, bias_col, bias_row, alog_col, alog_row, conv_w, norm_w):
    b, s, _ = proj.shape
    Lb = min(GDN_BLOCK, s)
    hd = B_HEAD_DIM
    qb, kb, vb, zb = COL_BQ // hd, COL_BK // hd, COL_BV // hd, COL_BZ // hd
    gb = COL_GATES // LANES
    blk = lambda off: pl.BlockSpec((None, Lb, hd), lambda i, h, c: (i, c, off + h))
    return pl.pallas_call(
        _gdn_kernel,
        grid=(b, B_HEADS, s // Lb),
        in_specs=[blk(qb), blk(kb), blk(vb), blk(zb),
                  pl.BlockSpec((None, Lb, LANES), lambda i, h, c: (i, c, gb)),
                  pl.BlockSpec((None, GATE_ROWS, Lb), lambda i, h, c: (i, 0, c)),
                  pl.BlockSpec((1, LANES), lambda i, h, c: (0, 0)),
                  pl.BlockSpec((GATE_ROWS, 1), lambda i, h, c: (0, 0)),
                  pl.BlockSpec((1, LANES), lambda i, h, c: (0, 0)),
                  pl.BlockSpec((GATE_ROWS, 1), lambda i, h, c: (0, 0)),
                  pl.BlockSpec((B_CONV, hd), lambda i, h, c: (0, h)),
                  pl.BlockSpec((B_CONV, hd), lambda i, h, c: (0, B_HEADS + h)),
                  pl.BlockSpec((B_CONV, hd), lambda i, h, c: (0, 2 * B_HEADS + h)),
                  pl.BlockSpec((1, hd), lambda i, h, c: (0, 0))],
        out_specs=pl.BlockSpec((None, Lb, hd), lambda i, h, c: (i, c, h)),
        out_shape=jax.ShapeDtypeStruct((b, s, B_HEADS * hd), F32),
        scratch_shapes=[pltpu.VMEM((hd, hd), F32),
                        pltpu.VMEM((SUBLANES, hd), F32),
                        pltpu.VMEM((SUBLANES, hd), F32),
                        pltpu.VMEM((SUBLANES, hd), F32)],
        compiler_params=_cparams(("parallel", "parallel", "arbitrary")),
        name="gdn",
    )(proj, proj, proj, proj, proj, gates_t, bias_col, bias_row, alog_col, alog_row,
      conv_w, conv_w, conv_w, norm_w)


def _outproj_kernel(ya_ref, yb_ref, x_ref, wa_ref, wb_ref, o_ref):
    acc = _dot(ya_ref[...].astype(BF16), wa_ref[...]) + _dot(yb_ref[...].astype(BF16), wb_ref[...])
    o_ref[...] = x_ref[...] + acc


def _outproj(ya, yb, x2, wa, wb):
    t, d = x2.shape
    ka, kb = ya.shape[1], yb.shape[1]
    tm = min(512, t)
    tn = 512
    return pl.pallas_call(
        _outproj_kernel,
        grid=(t // tm, d // tn),
        in_specs=[pl.BlockSpec((tm, ka), lambda i, j: (i, 0)),
                  pl.BlockSpec((tm, kb), lambda i, j: (i, 0)),
                  pl.BlockSpec((tm, tn), lambda i, j: (i, j)),
                  pl.BlockSpec((ka, tn), lambda i, j: (0, j)),
                  pl.BlockSpec((kb, tn), lambda i, j: (0, j))],
        out_specs=pl.BlockSpec((tm, tn), lambda i, j: (i, j)),
        out_shape=jax.ShapeDtypeStruct((t, d), F32),
        compiler_params=_cparams(("parallel", "arbitrary")),
        name="outproj",
    )(ya, yb, x2, wa, wb)


def _router_kernel(x_ref, g_ref, w_ref, b_ref, h_ref, ids_ref, gate_ref):
    x = x_ref[...]
    ms = jnp.mean(x * x, axis=-1, keepdims=True)
    h = x * lax.rsqrt(ms + EPS) * g_ref[...]
    h_ref[...] = h
    logits = jnp.dot(h, w_ref[...], precision=HIGHEST, preferred_element_type=F32) + b_ref[...]
    lane = lax.broadcasted_iota(jnp.int32, logits.shape, 1)
    neg = -jnp.inf
    big = jnp.int32(1 << 20)
    is_g = (lane >= MOE_EXPERTS) & (lane < MOE_EXPERTS + MOE_GROUPS)
    gl = jnp.where(is_g, logits, neg)
    gmax = jnp.max(gl, axis=-1, keepdims=True)
    g_lane = jnp.min(jnp.where(gl == gmax, lane, big), axis=-1, keepdims=True)
    g_idx = g_lane - MOE_EXPERTS
    g_w = 1.0 / jnp.sum(jnp.exp(gl - gmax), axis=-1, keepdims=True)
    in_grp = (lane >= g_idx * MOE_PER_GROUP) & (lane < (g_idx + 1) * MOE_PER_GROUP)
    el = jnp.where(in_grp, logits, neg)
    e0 = jnp.max(el, axis=-1, keepdims=True)
    l0 = jnp.min(jnp.where(el == e0, lane, big), axis=-1, keepdims=True)
    el1 = jnp.where(lane == l0, neg, el)
    e1 = jnp.max(el1, axis=-1, keepdims=True)
    l1 = jnp.min(jnp.where(el1 == e1, lane, big), axis=-1, keepdims=True)
    r = jnp.exp(e1 - e0)
    p0 = 1.0 / (1.0 + r)
    p1 = r / (1.0 + r)
    ids_ref[...] = jnp.where(lane == 0, l0, jnp.where(lane == 1, l1, 0))
    gate_ref[...] = jnp.where(lane == 0, g_w * p0, jnp.where(lane == 1, g_w * p1, 0.0))


def _router(x2, g, w, bias):
    t, d = x2.shape
    tm = min(256, t)
    return pl.pallas_call(
        _router_kernel,
        grid=(t // tm,),
        in_specs=[pl.BlockSpec((tm, d), lambda i: (i, 0)),
                  pl.BlockSpec((1, d), lambda i: (0, 0)),
                  pl.BlockSpec((d, LANES), lambda i: (0, 0)),
                  pl.BlockSpec((1, LANES), lambda i: (0, 0))],
        out_specs=[pl.BlockSpec((tm, d), lambda i: (i, 0)),
                   pl.BlockSpec((tm, LANES), lambda i: (i, 0)),
                   pl.BlockSpec((tm, LANES), lambda i: (i, 0))],
        out_shape=[jax.ShapeDtypeStruct((t, d), F32),
                   jax.ShapeDtypeStruct((t, LANES), jnp.int32),
                   jax.ShapeDtypeStruct((t, LANES), F32)],
        compiler_params=_cparams(("parallel",)),
        name="router",
    )(x2, g, w, bias)


def _moe_plan(expert, n_blocks):
    t = expert.shape[0]
    n_assign = 2 * t
    flat_e = expert.reshape(n_assign)
    order = jnp.argsort(flat_e, stable=True).astype(jnp.int32)
    sorted_e = flat_e[order]
    counts = jnp.zeros((MOE_EXPERTS,), jnp.int32).at[flat_e].add(1)
    starts = jnp.cumsum(counts) - counts
    nblk = (counts + MOE_ROWS - 1) // MOE_ROWS
    blk_end = jnp.cumsum(nblk)
    blk_start = blk_end - nblk
    dest = blk_start[sorted_e] * MOE_ROWS + jnp.arange(n_assign, dtype=jnp.int32) - starts[sorted_e]
    row_tok = jnp.zeros((n_blocks * MOE_ROWS,), jnp.int32).at[dest].set(order // 2)
    pos = jnp.zeros((n_assign,), jnp.int32).at[order].set(dest)
    bid = jnp.arange(n_blocks, dtype=jnp.int32)
    total = blk_end[-1]
    be = jnp.minimum(jnp.searchsorted(blk_end, bid, side="right").astype(jnp.int32), MOE_EXPERTS - 1)
    used = bid < total
    blk_n = jnp.where(used, jnp.clip(counts[be] - (bid - blk_start[be]) * MOE_ROWS, 0, MOE_ROWS), 0)
    blk_e = jnp.where(used, be, be[jnp.maximum(total - 1, 0)])
    return row_tok, pos, blk_e.astype(jnp.int32), blk_n.astype(jnp.int32)


def _moe_kernel(blk_e_ref, blk_n_ref, tok_ref, h_hbm, wg_ref, wu_ref, wd_ref, y_ref, xf_scr, xb_scr, sem):
    b = pl.program_id(0)
    f = pl.program_id(1)
    n = blk_n_ref[b]
    R = xf_scr.shape[0]

    def row_copy(r):
        tok = tok_ref[b * R + r]
        return pltpu.make_async_copy(h_hbm.at[pl.ds(tok, 1), :], xf_scr.at[pl.ds(r, 1), :], sem)

    @pl.when((f == 0) & (n > 0))
    def _():
        def start(r, carry):
            row_copy(r).start()
            return carry
        lax.fori_loop(0, R, start, 0)

        def wait(r, carry):
            row_copy(r).wait()
            return carry
        lax.fori_loop(0, R, wait, 0)
        xb_scr[...] = xf_scr[...].astype(BF16)

    @pl.when((f == 0) & (n == 0))
    def _():
        y_ref[...] = jnp.zeros_like(y_ref)

    @pl.when(n > 0)
    def _():
        wg = wg_ref[...].astype(BF16)
        wu = wu_ref[...].astype(BF16)
        wd = wd_ref[...].astype(BF16)
        for sb in range(R // MOE_SUB):
            rows = pl.ds(sb * MOE_SUB, MOE_SUB)

            @pl.when(sb * MOE_SUB < n)
            def _():
                x = xb_scr[rows, :]
                hmid = _silu(_dot(x, wg)) * _dot(x, wu)
                contrib = _dot(hmid.astype(BF16), wd)

                @pl.when(f == 0)
                def _():
                    y_ref[rows, :] = contrib

                @pl.when(f != 0)
                def _():
                    y_ref[rows, :] += contrib

            @pl.when((sb * MOE_SUB >= n) & (f == 0))
            def _():
                y_ref[rows, :] = jnp.zeros((MOE_SUB, y_ref.shape[1]), F32)


def _moe_experts(h2, row_tok, blk_e, blk_n, w_gate, w_up, w_down, n_blocks):
    t, d = h2.shape
    nf = MOE_FF // MOE_FT
    R = MOE_ROWS

    def w_in_map(b, f, be, bn, tok):
        return (be[b], 0, jnp.where(bn[b] > 0, f, nf - 1))

    def w_out_map(b, f, be, bn, tok):
        return (be[b], jnp.where(bn[b] > 0, f, nf - 1), 0)

    grid_spec = pltpu.PrefetchScalarGridSpec(
        num_scalar_prefetch=3,
        grid=(n_blocks, nf),
        in_specs=[pl.BlockSpec(memory_space=pl.ANY),
                  pl.BlockSpec((None, d, MOE_FT), w_in_map),
                  pl.BlockSpec((None, d, MOE_FT), w_in_map),
                  pl.BlockSpec((None, MOE_FT, d), w_out_map)],
        out_specs=pl.BlockSpec((R, d), lambda b, f, be, bn, tok: (b, 0)),
        scratch_shapes=[pltpu.VMEM((R, d), F32), pltpu.VMEM((R, d), BF16), pltpu.SemaphoreType.DMA(())],
    )
    return pl.pallas_call(
        _moe_kernel,
        grid_spec=grid_spec,
        out_shape=jax.ShapeDtypeStruct((n_blocks * R, d), F32),
        compiler_params=_cparams(("arbitrary", "arbitrary")),
        name="moe_experts",
    )(blk_e, blk_n, row_tok, h2, w_gate, w_up, w_down)


def _combine_kernel(pos_ref, y_hbm, x_ref, gate_ref, g_ref, o_ref, hn_ref, buf0, buf1, sem):
    i = pl.program_id(0)
    tm = x_ref.shape[0]

    def copies(r):
        p0 = pos_ref[2 * (i * tm + r)]
        p1 = pos_ref[2 * (i * tm + r) + 1]
        return (pltpu.make_async_copy(y_hbm.at[pl.ds(p0, 1), :], buf0.at[pl.ds(r, 1), :], sem),
                pltpu.make_async_copy(y_hbm.at[pl.ds(p1, 1), :], buf1.at[pl.ds(r, 1), :], sem))

    def start(r, carry):
        c0, c1 = copies(r)
        c0.start()
        c1.start()
        return carry
    lax.fori_loop(0, tm, start, 0)

    def wait(r, carry):
        c0, c1 = copies(r)
        c0.wait()
        c1.wait()
        return carry
    lax.fori_loop(0, tm, wait, 0)

    gate = gate_ref[...]
    g0 = gate[:, 0:1]
    g1 = gate[:, 1:2]
    x = x_ref[...] + g0 * buf0[...] + g1 * buf1[...]
    o_ref[...] = x
    ms = jnp.mean(x * x, axis=-1, keepdims=True)
    hn_ref[...] = x * lax.rsqrt(ms + EPS) * g_ref[...]


def _moe_combine(y_sorted, pos, x2, gate, next_norm):
    t, d = x2.shape
    tm = min(256, t)
    grid_spec = pltpu.PrefetchScalarGridSpec(
        num_scalar_prefetch=1,
        grid=(t // tm,),
        in_specs=[pl.BlockSpec(memory_space=pl.ANY),
                  pl.BlockSpec((tm, d), lambda i, p: (i, 0)),
                  pl.BlockSpec((tm, LANES), lambda i, p: (i, 0)),
                  pl.BlockSpec((1, d), lambda i, p: (0, 0))],
        out_specs=[pl.BlockSpec((tm, d), lambda i, p: (i, 0)),
                   pl.BlockSpec((tm, d), lambda i, p: (i, 0))],
        scratch_shapes=[pltpu.VMEM((tm, d), F32), pltpu.VMEM((tm, d), F32), pltpu.SemaphoreType.DMA(())],
    )
    return pl.pallas_call(
        _combine_kernel,
        grid_spec=grid_spec,
        out_shape=[jax.ShapeDtypeStruct((t, d), F32), jax.ShapeDtypeStruct((t, d), F32)],
        compiler_params=_cparams(("arbitrary",)),
        name="moe_combine",
    )(pos, y_sorted, x2, gate, next_norm)


def _hier_moe(x2, ffn_norm, w_rg, b_rg, w_re, b_re, w_gate, w_up, w_down, next_norm):
    t, d = x2.shape
    pad = LANES - MOE_EXPERTS - MOE_GROUPS
    w_r = jnp.concatenate([w_re, w_rg, jnp.zeros((d, pad), F32)], axis=1)
    b_r = jnp.concatenate([b_re, b_rg, jnp.zeros((pad,), F32)]).reshape(1, LANES)
    h2, ids, gate = _router(x2, ffn_norm.reshape(1, d), w_r, b_r)
    n_blocks = (2 * t) // MOE_ROWS + MOE_EXPERTS
    row_tok, pos, blk_e, blk_n = _moe_plan(ids[:, :2], n_blocks)
    y_sorted = _moe_experts(h2, row_tok, blk_e, blk_n, w_gate, w_up, w_down, n_blocks)
    return _moe_combine(y_sorted, pos, x2, gate, next_norm.reshape(1, d))


def _s5_disc_kernel(lre_ref, lim_ref, ls_ref, bre_ref, bim_ref, are_ref, aim_ref, bbre_ref, bbim_ref):
    lr = jnp.minimum(lre_ref[...], -1e-4)
    li = lim_ref[...]
    step = jnp.exp(ls_ref[...])
    mag = jnp.exp(lr * step)
    ang = li * step
    ab_re = mag * jnp.cos(ang)
    ab_im = mag * jnp.sin(ang)
    den = lr * lr + li * li
    zr = ab_re - 1.0
    f_re = (zr * lr + ab_im * li) / den
    f_im = (ab_im * lr - zr * li) / den
    br = bre_ref[...]
    bi = bim_ref[...]
    are_ref[...] = ab_re
    aim_ref[...] = ab_im
    bbre_ref[...] = f_re * br - f_im * bi
    bbim_ref[...] = f_re * bi + f_im * br


def _s5_discretise(lam_re, lam_im, log_step, b_re, b_im):
    g, p = lam_re.shape
    n = g * p
    col = lambda a: a.reshape(n, 1)
    ls = jnp.broadcast_to(log_step[:, None], (g, p))
    rows = 1024
    full = lambda w: pl.BlockSpec((rows, w), lambda i: (i, 0))
    return pl.pallas_call(
        _s5_disc_kernel,
        grid=(n // rows,),
        in_specs=[full(1), full(1), full(1), full(SSM_GROUP), full(SSM_GROUP)],
        out_specs=[full(1), full(1), full(SSM_GROUP), full(SSM_GROUP)],
        out_shape=[jax.ShapeDtypeStruct((n, 1), F32), jax.ShapeDtypeStruct((n, 1), F32),
                   jax.ShapeDtypeStruct((n, SSM_GROUP), F32), jax.ShapeDtypeStruct((n, SSM_GROUP), F32)],
        compiler_params=_cparams(("parallel",)),
        name="s5_discretise",
    )(col(lam_re), col(lam_im), col(ls), b_re.reshape(n, SSM_GROUP), b_im.reshape(n, SSM_GROUP))


def _cmul(ar, ai, br, bi):
    return ar * br - ai * bi, ar * bi + ai * br


def _s5_kernel(u_ref, bw_ref, cw_ref, a_ref, d_ref, y_ref, bu_scr, x_scr, pow_scr, carry_scr):
    tau = pl.program_id(2)
    Lb = u_ref.shape[0]
    seg = Lb // SUBLANES
    ns = S5_NSTATE
    ar = a_ref[0:1, :]
    ai = a_ref[1:2, :]

    @pl.when(tau == 0)
    def _():
        carry_scr[...] = jnp.zeros_like(carry_scr)
        pr, pi = ar, ai
        for i in range(seg):
            pow_scr[i:i + 1, 0:ns] = pr
            pow_scr[i:i + 1, ns:2 * ns] = pi
            pr, pi = _cmul(pr, pi, ar, ai)

    u_perm = jnp.concatenate([u_ref[pl.ds(i, SUBLANES, stride=seg), :] for i in range(seg)], axis=0)
    bu_scr[...] = _dot(u_perm.astype(BF16), bw_ref[...])

    ar8 = jnp.broadcast_to(ar, (SUBLANES, ns))
    ai8 = jnp.broadcast_to(ai, (SUBLANES, ns))

    def scan_body(i, carry):
        xr, xi = carry
        r0 = pl.multiple_of(i * SUBLANES, SUBLANES)
        nr = ar8 * xr - ai8 * xi + bu_scr[pl.ds(r0, SUBLANES), 0:ns]
        ni = ar8 * xi + ai8 * xr + bu_scr[pl.ds(r0, SUBLANES), ns:2 * ns]
        x_scr[pl.ds(r0, SUBLANES), 0:ns] = nr
        x_scr[pl.ds(r0, SUBLANES), ns:2 * ns] = ni
        return nr, ni

    zeros = jnp.zeros((SUBLANES, ns), F32)
    er, ei = lax.fori_loop(0, seg, scan_body, (zeros, zeros), unroll=8)

    alr = pow_scr[seg - 1:seg, 0:ns]
    ali = pow_scr[seg - 1:seg, ns:2 * ns]
    pr = carry_scr[0:1, 0:ns]
    pi = carry_scr[0:1, ns:2 * ns]
    prs, pis = [], []
    for j in range(SUBLANES):
        prs.append(pr)
        pis.append(pi)
        mr, mi = _cmul(alr, ali, pr, pi)
        pr = er[j:j + 1, :] + mr
        pi = ei[j:j + 1, :] + mi
    carry_scr[0:1, 0:ns] = pr
    carry_scr[0:1, ns:2 * ns] = pi
    p_re = jnp.concatenate(prs, axis=0)
    p_im = jnp.concatenate(pis, axis=0)

    def fix_body(i, carry):
        r0 = pl.multiple_of(i * SUBLANES, SUBLANES)
        wr = jnp.broadcast_to(pow_scr[pl.ds(i, 1), 0:ns], (SUBLANES, ns))
        wi = jnp.broadcast_to(pow_scr[pl.ds(i, 1), ns:2 * ns], (SUBLANES, ns))
        mr, mi = _cmul(wr, wi, p_re, p_im)
        x_scr[pl.ds(r0, SUBLANES), 0:ns] += mr
        x_scr[pl.ds(r0, SUBLANES), ns:2 * ns] += mi
        return carry

    lax.fori_loop(0, seg, fix_body, 0, unroll=8)

    y = _dot(x_scr[...].astype(BF16), cw_ref[...]) + d_ref[...] * u_perm
    for i in range(seg):
        y_ref[pl.ds(i, SUBLANES, stride=seg), :] = y[i * SUBLANES:(i + 1) * SUBLANES, :]


def _s5_scan(h3, bw, cw, a_rows, d_skip):
    b, s, w = h3.shape
    Lb = min(S5_BLOCK, s)
    nb = w // LANES
    ns2 = 2 * S5_NSTATE
    return pl.pallas_call(
        _s5_kernel,
        grid=(b, nb, s // Lb),
        in_specs=[pl.BlockSpec((None, Lb, LANES), lambda i, k, c: (i, c, k)),
                  pl.BlockSpec((None, LANES, ns2), lambda i, k, c: (k, 0, 0)),
                  pl.BlockSpec((None, ns2, LANES), lambda i, k, c: (k, 0, 0)),
                  pl.BlockSpec((None, 2, S5_NSTATE), lambda i, k, c: (k, 0, 0)),
                  pl.BlockSpec((1, LANES), lambda i, k, c: (0, k))],
        out_specs=pl.BlockSpec((None, Lb, LANES), lambda i, k, c: (i, c, k)),
        out_shape=jax.ShapeDtypeStruct((b, s, w), F32),
        scratch_shapes=[pltpu.VMEM((Lb, ns2), F32), pltpu.VMEM((Lb, ns2), F32),
                        pltpu.VMEM((Lb // SUBLANES, ns2), F32), pltpu.VMEM((SUBLANES, ns2), F32)],
        compiler_params=_cparams(("parallel", "parallel", "arbitrary")),
        name="s5_scan",
    )(h3, bw, cw, a_rows, d_skip)


def _glu_kernel(y_ref, x_ref, wv_ref, wg_ref, bv_ref, bg_ref, o_ref, a_scr):
    @pl.when(pl.program_id(1) == 0)
    def _():
        a_scr[...] = jax.nn.gelu(y_ref[...]).astype(BF16)

    a = a_scr[...]
    val = _dot(a, wv_ref[...]) + bv_ref[...]
    gate = _dot(a, wg_ref[...]) + bg_ref[...]
    o_ref[...] = x_ref[...] + val * _sigmoid(gate)


def _glu(y2, x2, w_bf, bias):
    t, d = x2.shape
    tm = min(512, t)
    tn = 512
    nj = d // tn
    return pl.pallas_call(
        _glu_kernel,
        grid=(t // tm, nj),
        in_specs=[pl.BlockSpec((tm, d), lambda i, j: (i, 0)),
                  pl.BlockSpec((tm, tn), lambda i, j: (i, j)),
                  pl.BlockSpec((d, tn), lambda i, j: (0, j)),
                  pl.BlockSpec((d, tn), lambda i, j: (0, nj + j)),
                  pl.BlockSpec((1, tn), lambda i, j: (0, j)),
                  pl.BlockSpec((1, tn), lambda i, j: (0, nj + j))],
        out_specs=pl.BlockSpec((tm, tn), lambda i, j: (i, j)),
        out_shape=jax.ShapeDtypeStruct((t, d), F32),
        scratch_shapes=[pltpu.VMEM((tm, d), BF16)],
        compiler_params=_cparams(("parallel", "arbitrary")),
        name="glu",
    )(y2, x2, w_bf, w_bf, bias, bias)


def _block_diag(w):
    nb, gb, r, c = w.shape
    eye = jnp.eye(gb, dtype=w.dtype)
    return (w[:, :, :, None, :] * eye[None, :, None, :, None]).reshape(nb, gb * r, gb * c)


def _mixer_s5(h3, x2, lam_re, lam_im, log_step, b_re, b_im, c_re, c_im, d_skip, w_glu, b_glu):
    b, s, w = h3.shape
    g, p = lam_re.shape
    gb = S5_GROUPS_PER_BLOCK
    nb = g // gb
    a_re, a_im, bb_re, bb_im = _s5_discretise(lam_re, lam_im, log_step, b_re, b_im)
    a_rows = jnp.stack([a_re.reshape(nb, gb * p), a_im.reshape(nb, gb * p)], axis=1)
    bt = lambda m: jnp.swapaxes(m.reshape(nb, gb, p, SSM_GROUP), 2, 3)
    bw = jnp.concatenate([_block_diag(bt(bb_re)), _block_diag(bt(bb_im))], axis=2).astype(BF16)
    ct = lambda m: jnp.swapaxes(m.reshape(nb, gb, SSM_GROUP, p), 2, 3)
    cw = jnp.concatenate([_block_diag(ct(c_re)), -_block_diag(ct(c_im))], axis=1).astype(BF16)
    y = _s5_scan(h3, bw, cw, a_rows, d_skip.reshape(1, w))
    return _glu(y.reshape(b * s, w), x2, w_glu.astype(BF16), b_glu.reshape(1, -1))


def _mixer_ab(x2, bsz, norm_w, w_in, a_i_bias, a_f_bias, a_norm, b_conv, b_a_log, b_dt_bias, b_norm, w_out):
    t, d = x2.shape
    s = t // bsz
    zeros = lambda n: jnp.zeros((n,), F32)
    gate_cols = jnp.concatenate([w_in[:, 3072:3080], w_in[:, 7176:7192],
                                 jnp.zeros((d, LANES - 24), F32)], axis=1)
    w_all = jnp.concatenate([w_in[:, :3072], w_in[:, 3080:7176], gate_cols], axis=1).astype(BF16)
    proj = _inproj(x2, norm_w.reshape(1, d), w_all).reshape(bsz, s, N_PROJ)
    gates_t = jnp.swapaxes(proj[:, :, COL_GATES:COL_GATES + GATE_ROWS], 1, 2)
    bias = jnp.concatenate([a_i_bias, a_f_bias, zeros(B_HEADS), b_dt_bias, zeros(LANES - 24)])
    alog = jnp.concatenate([zeros(G_BA), b_a_log, zeros(LANES - 24)])
    bias_col, bias_row = bias.reshape(1, LANES), bias[:GATE_ROWS].reshape(GATE_ROWS, 1)
    alog_col, alog_row = alog.reshape(1, LANES), alog[:GATE_ROWS].reshape(GATE_ROWS, 1)
    ya = _mlstm(proj, gates_t, bias_col, bias_row, a_norm.reshape(1, -1))
    yb = _gdn(proj, gates_t, bias_col, bias_row, alog_col, alog_row, b_conv, b_norm.reshape(1, -1))
    n_a = A_HEADS * A_V_DIM
    w_out_bf = w_out.astype(BF16)
    return _outproj(ya.reshape(t, n_a), yb.reshape(t, -1), x2, w_out_bf[:n_a], w_out_bf[n_a:])


def kernel(x, mix_norm, ab_w_in, mlstm_i_bias, mlstm_f_bias, mlstm_norm, gdn_conv, gdn_a_log, gdn_dt_bias, gdn_norm, ab_w_out, ssm_lambda_re, ssm_lambda_im, ssm_log_step, ssm_b_re, ssm_b_im, ssm_c_re, ssm_c_im, ssm_d, glu_w, glu_b, ffn_norm, router_group_w, router_group_b, router_expert_w, router_expert_b, expert_w_gate, expert_w_up, expert_w_down, final_norm):
    bsz, s, d = x.shape
    t = bsz * s
    x2 = x.reshape(t, d)
    x2 = _mixer_ab(x2, bsz, mix_norm[0], ab_w_in[0], mlstm_i_bias[0], mlstm_f_bias[0], mlstm_norm[0],
                   gdn_conv[0], gdn_a_log[0], gdn_dt_bias[0], gdn_norm[0], ab_w_out[0])
    x2, h3 = _hier_moe(x2, ffn_norm[0], router_group_w[0], router_group_b[0], router_expert_w[0],
                       router_expert_b[0], expert_w_gate[0], expert_w_up[0], expert_w_down[0], mix_norm[1])
    x2 = _mixer_s5(h3.reshape(bsz, s, d), x2, ssm_lambda_re[0], ssm_lambda_im[0], ssm_log_step[0],
                   ssm_b_re[0], ssm_b_im[0], ssm_c_re[0], ssm_c_im[0], ssm_d[0], glu_w[0], glu_b[0])
    _, out = _hier_moe(x2, ffn_norm[1], router_group_w[1], router_group_b[1], router_expert_w[1],
                       router_expert_b[1], expert_w_gate[1], expert_w_up[1], expert_w_down[1], final_norm)
    return out.reshape(bsz, s, d)
```

```python
import functools

import jax
import jax.numpy as jnp
from jax import lax
from jax.experimental import pallas as pl
from jax.experimental.pallas import tpu as pltpu

F32 = jnp.float32
BF16 = jnp.bfloat16
HIGHEST = lax.Precision.HIGHEST

EPS = 1e-6
D_MODEL = 2048
A_HEADS = 4
A_QK_DIM = 128
A_V_DIM = 256
A_GATE_CAP = 15.0
B_HEADS = 8
B_HEAD_DIM = 128
B_CONV = 4
SSM_GROUP = 16
SSM_STATE = 64
MOE_GROUPS = 8
MOE_PER_GROUP = 8
MOE_EXPERTS = 64
MOE_FF = 768

LANES = 128
SUBLANES = 8
VMEM_LIMIT = 56 * 1024 * 1024

COL_AQ, COL_AK, COL_AV, COL_AO = 0, 512, 1024, 2048
COL_BQ, COL_BK, COL_BV, COL_BZ = 3072, 4096, 5120, 6144
COL_GATES = 7168
N_PROJ = 7296
G_AI, G_AF, G_BB, G_BA, G_END = 0, 4, 8, 16, 24
GATE_ROWS = 32

MLSTM_CHUNK = 256
GDN_BLOCK = MLSTM_CHUNK
GDN_CHUNK = 64
S5_BLOCK = 512
S5_GROUPS_PER_BLOCK = 8
S5_NSTATE = S5_GROUPS_PER_BLOCK * SSM_STATE
MOE_ROWS = 512
MOE_KT = 4
MOE_KW = D_MODEL // MOE_KT


def _cparams(sem):
    return pltpu.CompilerParams(dimension_semantics=sem, vmem_limit_bytes=VMEM_LIMIT)


def _softcap(t, cap):
    return cap * jnp.tanh(t / cap)


def _log_sigmoid(t):
    return jnp.minimum(t, 0.0) - jnp.log(1.0 + jnp.exp(-jnp.abs(t)))


def _softplus(t):
    return jnp.maximum(t, 0.0) + jnp.log(1.0 + jnp.exp(-jnp.abs(t)))


def _sigmoid(t):
    return 1.0 / (1.0 + jnp.exp(-t))


def _silu(t):
    return t * _sigmoid(t)


def _pick_col(x, idx):
    lane = lax.broadcasted_iota(jnp.int32, x.shape, 1)
    return jnp.sum(jnp.where(lane == idx, x, 0.0), axis=-1, keepdims=True)


def _dot(a, b):
    return jnp.dot(a, b, preferred_element_type=F32)


def _dot_hi(a, b):
    return jnp.dot(a, b, precision=HIGHEST, preferred_element_type=F32)


def _dot_nt(a, b):
    return lax.dot_general(a, b, (((1,), (1,)), ((), ())), preferred_element_type=F32)


def _inproj_kernel(x_ref, g_ref, w_ref, o_ref, h_scr):
    @pl.when(pl.program_id(1) == 0)
    def _():
        x = x_ref[...]
        ms = jnp.mean(x * x, axis=-1, keepdims=True)
        h_scr[...] = (x * lax.rsqrt(ms + EPS) * g_ref[...]).astype(BF16)

    o_ref[...] = _dot(h_scr[...], w_ref[...])


def _inproj(x2, g, w_bf):
    t, d = x2.shape
    n = w_bf.shape[1]
    tm = min(1024, t)
    tn = 384
    return pl.pallas_call(
        _inproj_kernel,
        grid=(t // tm, n // tn),
        in_specs=[pl.BlockSpec((tm, d), lambda i, j: (i, 0)),
                  pl.BlockSpec((1, d), lambda i, j: (0, 0)),
                  pl.BlockSpec((d, tn), lambda i, j: (0, j))],
        out_specs=pl.BlockSpec((tm, tn), lambda i, j: (i, j)),
        out_shape=jax.ShapeDtypeStruct((t, n), F32),
        scratch_shapes=[pltpu.VMEM((tm, d), BF16)],
        compiler_params=_cparams(("parallel", "arbitrary")),
        name="inproj",
    )(x2, g, w_bf)


def _gate_values(g, alog, idx, cum_a, cum_b):
    sc = _softcap(g, A_GATE_CAP)
    cum_logf = cum_a(_log_sigmoid(sc))
    cum_g = cum_b(-jnp.exp(alog) * _softplus(g))
    return jnp.where(idx < G_AF, sc,
                     jnp.where(idx < G_BB, cum_logf,
                               jnp.where(idx < G_BA, _sigmoid(g), jnp.where(idx < G_END, cum_g, 0.0))))


def _gates_kernel(gc_ref, gr_ref, bc_ref, br_ref, alc_ref, alr_ref, pc_ref, pr_ref):
    L = gc_ref.shape[0]
    row = lax.broadcasted_iota(jnp.int32, (L, L), 0)
    col = lax.broadcasted_iota(jnp.int32, (L, L), 1)
    same = (row // GDN_CHUNK) == (col // GDN_CHUNK)
    tril = (row >= col).astype(F32)
    triu = (row <= col).astype(F32)
    blk_tril = (same & (row >= col)).astype(F32)
    blk_triu = (same & (row <= col)).astype(F32)
    lane = lax.broadcasted_iota(jnp.int32, (L, LANES), 1)
    pc_ref[...] = _gate_values(gc_ref[...] + bc_ref[...], alc_ref[...], lane,
                               lambda v: _dot_hi(tril, v), lambda v: _dot_hi(blk_tril, v))
    sub = lax.broadcasted_iota(jnp.int32, (GATE_ROWS, L), 0)
    pr_ref[...] = _gate_values(gr_ref[...] + br_ref[...], alr_ref[...], sub,
                               lambda v: _dot_hi(v, triu), lambda v: _dot_hi(v, blk_triu))


def _gates(proj, gates_t, bias_col, bias_row, alog_col, alog_row):
    b, s, _ = proj.shape
    L = min(MLSTM_CHUNK, s)
    gb = COL_GATES // LANES
    return pl.pallas_call(
        _gates_kernel,
        grid=(b, s // L),
        in_specs=[pl.BlockSpec((None, L, LANES), lambda i, c: (i, c, gb)),
                  pl.BlockSpec((None, GATE_ROWS, L), lambda i, c: (i, 0, c)),
                  pl.BlockSpec((1, LANES), lambda i, c: (0, 0)),
                  pl.BlockSpec((GATE_ROWS, 1), lambda i, c: (0, 0)),
                  pl.BlockSpec((1, LANES), lambda i, c: (0, 0)),
                  pl.BlockSpec((GATE_ROWS, 1), lambda i, c: (0, 0))],
        out_specs=[pl.BlockSpec((None, L, LANES), lambda i, c: (i, c, 0)),
                   pl.BlockSpec((None, GATE_ROWS, L), lambda i, c: (i, 0, c))],
        out_shape=[jax.ShapeDtypeStruct((b, s, LANES), F32),
                   jax.ShapeDtypeStruct((b, GATE_ROWS, s), F32)],
        compiler_params=_cparams(("parallel", "parallel")),
        name="gates",
    )(proj, gates_t, bias_col, bias_row, alog_col, alog_row)


def _mlstm_kernel(q_ref, k_ref, v_ref, o_ref, pc_ref, pr_ref, nw_ref, out_ref, c_scr, n_scr, m_scr):
    h = pl.program_id(1)
    c = pl.program_id(2)
    L = q_ref.shape[0]

    @pl.when(c == 0)
    def _():
        c_scr[...] = jnp.zeros_like(c_scr)
        n_scr[...] = jnp.zeros_like(n_scr)
        m_scr[...] = jnp.zeros_like(m_scr)

    row = lax.broadcasted_iota(jnp.int32, (L, L), 0)
    col = lax.broadcasted_iota(jnp.int32, (L, L), 1)
    causal = row >= col

    pc = pc_ref[...]
    i_col = _pick_col(pc, G_AI + h)
    b_col = _pick_col(pc, G_AF + h)
    i_row = pr_ref[pl.ds(G_AI + h, 1), :]
    b_row = pr_ref[pl.ds(G_AF + h, 1), :]
    b_last = b_col[L - 1:L, :]

    m_prev = m_scr[...]
    log_d = jnp.where(causal, b_col - b_row + i_row, -jnp.inf)
    log_inter = b_col + m_prev
    m_t = jnp.maximum(log_inter, jnp.max(log_d, axis=-1, keepdims=True))
    dmat = jnp.exp(log_d - m_t)
    inter = jnp.exp(log_inter - m_t)

    q = q_ref[...]
    k = k_ref[...] * (A_QK_DIM ** -0.5)
    qb = q.astype(BF16)
    kb = k.astype(BF16)
    vb = v_ref[...].astype(BF16)
    scores = _dot_nt(qb, kb) * dmat
    c_mat = c_scr[...]
    n_vec = n_scr[...]
    num = inter * _dot(qb, c_mat.astype(BF16)) + _dot(scores.astype(BF16), vb)
    den = inter * jnp.sum(q * n_vec, axis=-1, keepdims=True) + jnp.sum(scores, axis=-1, keepdims=True)
    hh = num / jnp.maximum(jnp.abs(den), jnp.exp(-m_t))
    hh = hh * lax.rsqrt(jnp.mean(hh * hh, axis=-1, keepdims=True) + EPS) * nw_ref[...]
    out_ref[...] = hh * _sigmoid(o_ref[...])

    le_col = b_last - b_col + i_col
    m_new = jnp.maximum(b_last + m_prev, jnp.max(le_col, axis=0, keepdims=True))
    carry_scale = jnp.exp(b_last + m_prev - m_new)
    kw = k * jnp.exp(le_col - m_new)
    c_scr[...] = c_mat * carry_scale + _dot(kw.T.astype(BF16), vb)
    n_scr[...] = n_vec * carry_scale + jnp.sum(kw, axis=0, keepdims=True)
    m_scr[...] = m_new


def _mlstm(proj, pc, pr, norm_w):
    b, s, _ = proj.shape
    L = min(MLSTM_CHUNK, s)
    qb, kb = COL_AQ // A_QK_DIM, COL_AK // A_QK_DIM
    vb, ob = COL_AV // A_V_DIM, COL_AO // A_V_DIM
    return pl.pallas_call(
        _mlstm_kernel,
        grid=(b, A_HEADS, s // L),
        in_specs=[pl.BlockSpec((None, L, A_QK_DIM), lambda i, h, c: (i, c, qb + h)),
                  pl.BlockSpec((None, L, A_QK_DIM), lambda i, h, c: (i, c, kb + h)),
                  pl.BlockSpec((None, L, A_V_DIM), lambda i, h, c: (i, c, vb + h)),
                  pl.BlockSpec((None, L, A_V_DIM), lambda i, h, c: (i, c, ob + h)),
                  pl.BlockSpec((None, L, LANES), lambda i, h, c: (i, c, 0)),
                  pl.BlockSpec((None, GATE_ROWS, L), lambda i, h, c: (i, 0, c)),
                  pl.BlockSpec((1, A_V_DIM), lambda i, h, c: (0, h))],
        out_specs=pl.BlockSpec((None, L, A_V_DIM), lambda i, h, c: (i, c, h)),
        out_shape=jax.ShapeDtypeStruct((b, s, A_HEADS * A_V_DIM), F32),
        scratch_shapes=[pltpu.VMEM((A_QK_DIM, A_V_DIM), F32),
                        pltpu.VMEM((1, A_QK_DIM), F32),
                        pltpu.VMEM((1, 1), F32)],
        compiler_params=_cparams(("parallel", "parallel", "arbitrary")),
        name="mlstm",
    )(proj, proj, proj, proj, pc, pr, norm_w)


def _causal_conv_silu(x, tail, w):
    row8 = lax.broadcasted_iota(jnp.int32, (SUBLANES, x.shape[1]), 0)
    acc = x * w[B_CONV - 1:B_CONV, :]
    for d in range(1, B_CONV):
        rolled = pltpu.roll(x, d, 0)
        head = jnp.where(row8 < d, pltpu.roll(tail, d, 0), rolled[0:SUBLANES, :])
        shifted = jnp.concatenate([head, rolled[SUBLANES:, :]], axis=0)
        acc = acc + shifted * w[B_CONV - 1 - d:B_CONV - d, :]
    return _silu(acc)


def _l2norm(t):
    return t * lax.rsqrt(jnp.sum(t * t, axis=-1, keepdims=True) + EPS)


def _gdn_kernel(q_ref, k_ref, v_ref, z_ref, pc_ref, pr_ref, wq_ref, wk_ref, wv_ref, nw_ref, out_ref,
                s_scr, tq_scr, tk_scr, tv_scr):
    h = pl.program_id(1)
    c = pl.program_id(2)
    Lb = q_ref.shape[0]
    C = GDN_CHUNK
    nsub = Lb // C

    @pl.when(c == 0)
    def _():
        s_scr[...] = jnp.zeros_like(s_scr)
        tq_scr[...] = jnp.zeros_like(tq_scr)
        tk_scr[...] = jnp.zeros_like(tk_scr)
        tv_scr[...] = jnp.zeros_like(tv_scr)

    xq, xk, xv = q_ref[...], k_ref[...], v_ref[...]
    cq = _causal_conv_silu(xq, tq_scr[...], wq_ref[...])
    ck = _causal_conv_silu(xk, tk_scr[...], wk_ref[...])
    cv = _causal_conv_silu(xv, tv_scr[...], wv_ref[...])
    tq_scr[...] = xq[Lb - SUBLANES:, :]
    tk_scr[...] = xk[Lb - SUBLANES:, :]
    tv_scr[...] = xv[Lb - SUBLANES:, :]
    q = _l2norm(cq) * (B_HEAD_DIM ** -0.5)
    k = _l2norm(ck)
    v = cv

    pc = pc_ref[...]
    beta = _pick_col(pc, G_BB + h)
    dec_col = _pick_col(pc, G_BA + h)
    dec_row = pr_ref[pl.ds(G_BA + h, 1), :]

    r64 = lax.broadcasted_iota(jnp.int32, (C, C), 0)
    c64 = lax.broadcasted_iota(jnp.int32, (C, C), 1)
    causal = r64 >= c64
    strict = r64 > c64
    eye = (r64 == c64).astype(F32)

    kbeta = k * beta
    edec = jnp.exp(dec_col)
    q_dec = (q * edec).astype(BF16)
    vbeta = (v * beta).astype(BF16)
    kbdec = (kbeta * edec).astype(BF16)
    kb16 = k.astype(BF16)
    qb16 = q.astype(BF16)
    kbeta16 = kbeta.astype(BF16)

    segs, nmats = [], []
    for j in range(nsub):
        sl = slice(j * C, (j + 1) * C)
        seg = jnp.exp(jnp.where(causal, dec_col[sl, :] - dec_row[:, sl], -jnp.inf))
        a_low = jnp.where(strict, _dot_nt(kbeta16[sl, :], kb16[sl, :]) * seg, 0.0)
        segs.append(seg)
        nmats.append(-a_low)
    nmat = jnp.stack(nmats, axis=0)
    bmm = lambda a, b: jnp.einsum("bij,bjk->bik", a.astype(BF16), b.astype(BF16), preferred_element_type=F32)
    tmat = eye[None] + nmat
    npow = bmm(nmat, nmat)
    for _ in range(4):
        tmat, npow = tmat + bmm(tmat, npow), bmm(npow, npow)
    tmat = tmat + bmm(tmat, npow)

    state = s_scr[...]
    outs = []
    for j in range(nsub):
        sl = slice(j * C, (j + 1) * C)
        t16 = tmat[j].astype(BF16)
        u = _dot(t16, vbeta[sl, :])
        w = _dot(t16, kbdec[sl, :])
        attn = _dot_nt(qb16[sl, :], kb16[sl, :]) * segs[j]
        s16 = state.astype(BF16)
        v_new = u - _dot(w.astype(BF16), s16)
        o = _dot(q_dec[sl, :], s16) + _dot(attn.astype(BF16), v_new.astype(BF16))
        d_last = dec_col[(j + 1) * C - 1:(j + 1) * C, :]
        k_end = k[sl, :] * jnp.exp(d_last - dec_col[sl, :])
        state = state * jnp.exp(d_last) + _dot(k_end.T.astype(BF16), v_new.astype(BF16))
        outs.append(o)
    s_scr[...] = state
    hb = jnp.concatenate(outs, axis=0)
    hb = hb * lax.rsqrt(jnp.mean(hb * hb, axis=-1, keepdims=True) + EPS) * nw_ref[...]
    out_ref[...] = hb * _silu(z_ref[...])


def _gdn(proj, pc, pr, conv_w, norm_w):
    b, s, _ = proj.shape
    Lb = min(GDN_BLOCK, s)
    hd = B_HEAD_DIM
    qb, kb, vb, zb = COL_BQ // hd, COL_BK // hd, COL_BV // hd, COL_BZ // hd
    blk = lambda off: pl.BlockSpec((None, Lb, hd), lambda i, h, c: (i, c, off + h))
    return pl.pallas_call(
        _gdn_kernel,
        grid=(b, B_HEADS, s // Lb),
        in_specs=[blk(qb), blk(kb), blk(vb), blk(zb),
                  pl.BlockSpec((None, Lb, LANES), lambda i, h, c: (i, c, 0)),
                  pl.BlockSpec((None, GATE_ROWS, Lb), lambda i, h, c: (i, 0, c)),
                  pl.BlockSpec((B_CONV, hd), lambda i, h, c: (0, h)),
                  pl.BlockSpec((B_CONV, hd), lambda i, h, c: (0, B_HEADS + h)),
                  pl.BlockSpec((B_CONV, hd), lambda i, h, c: (0, 2 * B_HEADS + h)),
                  pl.BlockSpec((1, hd), lambda i, h, c: (0, 0))],
        out_specs=pl.BlockSpec((None, Lb, hd), lambda i, h, c: (i, c, h)),
        out_shape=jax.ShapeDtypeStruct((b, s, B_HEADS * hd), F32),
        scratch_shapes=[pltpu.VMEM((hd, hd), F32),
                        pltpu.VMEM((SUBLANES, hd), F32),
                        pltpu.VMEM((SUBLANES, hd), F32),
                        pltpu.VMEM((SUBLANES, hd), F32)],
        compiler_params=_cparams(("parallel", "parallel", "arbitrary")),
        name="gdn",
    )(proj, proj, proj, proj, pc, pr, conv_w, conv_w, conv_w, norm_w)


def _outproj_kernel(ya_ref, yb_ref, x_ref, wa_ref, wb_ref, o_ref):
    acc = _dot(ya_ref[...].astype(BF16), wa_ref[...]) + _dot(yb_ref[...].astype(BF16), wb_ref[...])
    o_ref[...] = x_ref[...] + acc


def _outproj(ya, yb, x2, wa, wb):
    t, d = x2.shape
    ka, kb = ya.shape[1], yb.shape[1]
    tm = min(512, t)
    tn = 512
    return pl.pallas_call(
        _outproj_kernel,
        grid=(t // tm, d // tn),
        in_specs=[pl.BlockSpec((tm, ka), lambda i, j: (i, 0)),
                  pl.BlockSpec((tm, kb), lambda i, j: (i, 0)),
                  pl.BlockSpec((tm, tn), lambda i, j: (i, j)),
                  pl.BlockSpec((ka, tn), lambda i, j: (0, j)),
                  pl.BlockSpec((kb, tn), lambda i, j: (0, j))],
        out_specs=pl.BlockSpec((tm, tn), lambda i, j: (i, j)),
        out_shape=jax.ShapeDtypeStruct((t, d), F32),
        compiler_params=_cparams(("parallel", "arbitrary")),
        name="outproj",
    )(ya, yb, x2, wa, wb)


def _router_kernel(x_ref, g_ref, w_ref, b_ref, h_ref, ids_ref, gate_ref):
    x = x_ref[...]
    ms = jnp.mean(x * x, axis=-1, keepdims=True)
    h = x * lax.rsqrt(ms + EPS) * g_ref[...]
    h_ref[...] = h
    logits = _dot_hi(h, w_ref[...]) + b_ref[...]
    lane = lax.broadcasted_iota(jnp.int32, logits.shape, 1)
    neg = -jnp.inf
    big = jnp.int32(1 << 20)
    is_g = (lane >= MOE_EXPERTS) & (lane < MOE_EXPERTS + MOE_GROUPS)
    gl = jnp.where(is_g, logits, neg)
    gmax = jnp.max(gl, axis=-1, keepdims=True)
    g_lane = jnp.min(jnp.where(gl == gmax, lane, big), axis=-1, keepdims=True)
    g_idx = g_lane - MOE_EXPERTS
    g_w = 1.0 / jnp.sum(jnp.exp(gl - gmax), axis=-1, keepdims=True)
    in_grp = (lane >= g_idx * MOE_PER_GROUP) & (lane < (g_idx + 1) * MOE_PER_GROUP)
    el = jnp.where(in_grp, logits, neg)
    e0 = jnp.max(el, axis=-1, keepdims=True)
    l0 = jnp.min(jnp.where(el == e0, lane, big), axis=-1, keepdims=True)
    el1 = jnp.where(lane == l0, neg, el)
    e1 = jnp.max(el1, axis=-1, keepdims=True)
    l1 = jnp.min(jnp.where(el1 == e1, lane, big), axis=-1, keepdims=True)
    r = jnp.exp(e1 - e0)
    p0 = 1.0 / (1.0 + r)
    p1 = r / (1.0 + r)
    ids_ref[...] = jnp.where(lane == 0, l0, jnp.where(lane == 1, l1, 0))
    gate_ref[...] = jnp.where(lane == 0, g_w * p0, jnp.where(lane == 1, g_w * p1, 0.0))


def _router(x2, g, w, bias):
    t, d = x2.shape
    tm = min(256, t)
    return pl.pallas_call(
        _router_kernel,
        grid=(t // tm,),
        in_specs=[pl.BlockSpec((tm, d), lambda i: (i, 0)),
                  pl.BlockSpec((1, d), lambda i: (0, 0)),
                  pl.BlockSpec((d, LANES), lambda i: (0, 0)),
                  pl.BlockSpec((1, LANES), lambda i: (0, 0))],
        out_specs=[pl.BlockSpec((tm, d), lambda i: (i, 0)),
                   pl.BlockSpec((tm, LANES), lambda i: (i, 0)),
                   pl.BlockSpec((tm, LANES), lambda i: (i, 0))],
        out_shape=[jax.ShapeDtypeStruct((t, d), F32),
                   jax.ShapeDtypeStruct((t, LANES), jnp.int32),
                   jax.ShapeDtypeStruct((t, LANES), F32)],
        compiler_params=_cparams(("parallel",)),
        name="router",
    )(x2, g, w, bias)


def _moe_plan(expert, n_blocks):
    t = expert.shape[0]
    n_assign = 2 * t
    flat_e = expert.reshape(n_assign)
    onehot = (flat_e[:, None] == jnp.arange(MOE_EXPERTS, dtype=jnp.int32)[None, :]).astype(jnp.int32)
    csum = jnp.cumsum(onehot, axis=0)
    rank = jnp.sum(onehot * csum, axis=1) - 1
    counts = csum[-1]
    nblk = (counts + MOE_ROWS - 1) // MOE_ROWS
    blk_end = jnp.cumsum(nblk)
    blk_start = blk_end - nblk
    dest = jnp.sum(onehot * blk_start[None, :], axis=1) * MOE_ROWS + rank
    row_dst = jnp.zeros((n_blocks * MOE_ROWS,), jnp.int32).at[dest].set(jnp.arange(n_assign, dtype=jnp.int32))
    bid = jnp.arange(n_blocks, dtype=jnp.int32)
    total = blk_end[-1]
    be = jnp.minimum(jnp.sum((bid[:, None] >= blk_end[None, :]).astype(jnp.int32), axis=1), MOE_EXPERTS - 1)
    used = bid < total
    blk_n = jnp.where(used, jnp.clip(counts[be] - (bid - blk_start[be]) * MOE_ROWS, 0, MOE_ROWS), 0)
    blk_e = jnp.where(used, be, be[jnp.maximum(total - 1, 0)])
    return row_dst, blk_e.astype(jnp.int32), blk_n.astype(jnp.int32)


def _moe_kernel(blk_e_ref, blk_n_ref, dst_ref, h_hbm, wg_ref, wu_ref, wd_ref, yk_hbm,
                xf_scr, xb_scr, g_scr, u_scr, y_scr, gsem, ssem):
    b = pl.program_id(0)
    k = pl.program_id(1)
    nb = pl.num_programs(0)
    n = blk_n_ref[b]
    R = xf_scr.shape[0]

    def gather_copy(blk, r):
        tok = lax.shift_right_logical(dst_ref[blk * R + r], 1)
        return pltpu.make_async_copy(h_hbm.at[pl.ds(tok, 1), :], xf_scr.at[pl.ds(r, 1), :], gsem)

    def scatter_copy(blk, r):
        dst = dst_ref[blk * R + r]
        return pltpu.make_async_copy(y_scr.at[pl.ds(r, 1), :], yk_hbm.at[pl.ds(dst, 1), :], ssem)

    def for_rows(blk, fn):
        def body(r, carry):
            fn(blk, r)
            return carry
        lax.fori_loop(0, blk_n_ref[blk], body, 0)

    @pl.when((b == 0) & (k == 0))
    def _():
        xf_scr[...] = jnp.zeros_like(xf_scr)
        for_rows(0, lambda blk, r: gather_copy(blk, r).start())

    @pl.when(k == 0)
    def _():
        for_rows(b, lambda blk, r: gather_copy(blk, r).wait())
        for kk in range(MOE_KT):
            xb_scr[kk] = xf_scr[:, kk * MOE_KW:(kk + 1) * MOE_KW].astype(BF16)

    @pl.when((k == 1) & (b + 1 < nb))
    def _():
        for_rows(b + 1, lambda blk, r: gather_copy(blk, r).start())

    @pl.when(n > 0)
    def _():
        x = xb_scr[k]
        pg = _dot(x, wg_ref[...].astype(BF16))
        pu = _dot(x, wu_ref[...].astype(BF16))

        @pl.when(k == 0)
        def _():
            g_scr[...] = pg
            u_scr[...] = pu

        @pl.when(k != 0)
        def _():
            g_scr[...] += pg
            u_scr[...] += pu

    @pl.when(k == MOE_KT - 1)
    def _():
        @pl.when(b > 0)
        def _():
            for_rows(b - 1, lambda blk, r: scatter_copy(blk, r).wait())

        @pl.when(n > 0)
        def _():
            hmid = _silu(g_scr[...]) * u_scr[...]
            y_scr[...] = _dot(hmid.astype(BF16), wd_ref[...].astype(BF16))
            for_rows(b, lambda blk, r: scatter_copy(blk, r).start())

        @pl.when(b == nb - 1)
        def _():
            for_rows(b, lambda blk, r: scatter_copy(blk, r).wait())


def _moe_experts(h2, row_dst, blk_e, blk_n, w_gate, w_up, w_down, layer, n_blocks):
    t, d = h2.shape
    R = MOE_ROWS
    last = MOE_KT - 1

    def w_in_map(b, k, be, bn, dst):
        return (layer, be[b], jnp.where(bn[b] > 0, k, last), 0)

    def w_out_map(b, k, be, bn, dst):
        return (layer, be[b], 0, 0)

    grid_spec = pltpu.PrefetchScalarGridSpec(
        num_scalar_prefetch=3,
        grid=(n_blocks, MOE_KT),
        in_specs=[pl.BlockSpec(memory_space=pl.ANY),
                  pl.BlockSpec((None, None, MOE_KW, MOE_FF), w_in_map),
                  pl.BlockSpec((None, None, MOE_KW, MOE_FF), w_in_map),
                  pl.BlockSpec((None, None, MOE_FF, d), w_out_map)],
        out_specs=pl.BlockSpec(memory_space=pl.ANY),
        scratch_shapes=[pltpu.VMEM((R, d), F32), pltpu.VMEM((MOE_KT, R, MOE_KW), BF16),
                        pltpu.VMEM((R, MOE_FF), F32), pltpu.VMEM((R, MOE_FF), F32), pltpu.VMEM((R, d), F32),
                        pltpu.SemaphoreType.DMA(()), pltpu.SemaphoreType.DMA(())],
    )
    return pl.pallas_call(
        _moe_kernel,
        grid_spec=grid_spec,
        out_shape=jax.ShapeDtypeStruct((2 * t, d), F32),
        compiler_params=_cparams(("arbitrary", "arbitrary")),
        name="moe_experts",
    )(blk_e, blk_n, row_dst, h2, w_gate, w_up, w_down)


def _combine_kernel(yk_ref, x_ref, gate_ref, g_ref, o_ref, hn_ref):
    d = x_ref.shape[1]
    gate = gate_ref[...]
    x = x_ref[...] + gate[:, 0:1] * yk_ref[:, 0:d] + gate[:, 1:2] * yk_ref[:, d:2 * d]
    o_ref[...] = x
    ms = jnp.mean(x * x, axis=-1, keepdims=True)
    hn_ref[...] = x * lax.rsqrt(ms + EPS) * g_ref[...]


def _moe_combine(yk, x2, gate, next_norm):
    t, d = x2.shape
    tm = min(256, t)
    return pl.pallas_call(
        _combine_kernel,
        grid=(t // tm,),
        in_specs=[pl.BlockSpec((tm, 2 * d), lambda i: (i, 0)),
                  pl.BlockSpec((tm, d), lambda i: (i, 0)),
                  pl.BlockSpec((tm, LANES), lambda i: (i, 0)),
                  pl.BlockSpec((1, d), lambda i: (0, 0))],
        out_specs=[pl.BlockSpec((tm, d), lambda i: (i, 0)),
                   pl.BlockSpec((tm, d), lambda i: (i, 0))],
        out_shape=[jax.ShapeDtypeStruct((t, d), F32), jax.ShapeDtypeStruct((t, d), F32)],
        compiler_params=_cparams(("parallel",)),
        name="moe_combine",
    )(yk, x2, gate, next_norm)


def _hier_moe(x2, ffn_norm, w_rg, b_rg, w_re, b_re, w_gate, w_up, w_down, layer, next_norm):
    t, d = x2.shape
    pad = LANES - MOE_EXPERTS - MOE_GROUPS
    w_r = jnp.concatenate([w_re, w_rg, jnp.zeros((d, pad), F32)], axis=1)
    b_r = jnp.concatenate([b_re, b_rg, jnp.zeros((pad,), F32)]).reshape(1, LANES)
    h2, ids, gate = _router(x2, ffn_norm.reshape(1, d), w_r, b_r)
    n_blocks = (2 * t) // MOE_ROWS + MOE_EXPERTS
    row_dst, blk_e, blk_n = _moe_plan(ids[:, :2], n_blocks)
    yk = _moe_experts(h2, row_dst, blk_e, blk_n, w_gate, w_up, w_down, layer, n_blocks)
    return _moe_combine(yk.reshape(t, 2 * d), x2, gate, next_norm.reshape(1, d))


def _s5_disc_kernel(lre_ref, lim_ref, ls_ref, bre_ref, bim_ref, are_ref, aim_ref, bbre_ref, bbim_ref):
    lr = jnp.minimum(lre_ref[...], -1e-4)
    li = lim_ref[...]
    step = jnp.exp(ls_ref[...])
    mag = jnp.exp(lr * step)
    ang = li * step
    ab_re = mag * jnp.cos(ang)
    ab_im = mag * jnp.sin(ang)
    den = lr * lr + li * li
    zr = ab_re - 1.0
    f_re = (zr * lr + ab_im * li) / den
    f_im = (ab_im * lr - zr * li) / den
    br = bre_ref[...]
    bi = bim_ref[...]
    are_ref[...] = ab_re
    aim_ref[...] = ab_im
    bbre_ref[...] = f_re * br - f_im * bi
    bbim_ref[...] = f_re * bi + f_im * br


def _s5_discretise(lam_re, lam_im, log_step, b_re, b_im):
    g, p = lam_re.shape
    n = g * p
    col = lambda a: a.reshape(n, 1)
    ls = jnp.broadcast_to(log_step[:, None], (g, p))
    rows = 1024
    full = lambda w: pl.BlockSpec((rows, w), lambda i: (i, 0))
    return pl.pallas_call(
        _s5_disc_kernel,
        grid=(n // rows,),
        in_specs=[full(1), full(1), full(1), full(SSM_GROUP), full(SSM_GROUP)],
        out_specs=[full(1), full(1), full(SSM_GROUP), full(SSM_GROUP)],
        out_shape=[jax.ShapeDtypeStruct((n, 1), F32), jax.ShapeDtypeStruct((n, 1), F32),
                   jax.ShapeDtypeStruct((n, SSM_GROUP), F32), jax.ShapeDtypeStruct((n, SSM_GROUP), F32)],
        compiler_params=_cparams(("parallel",)),
        name="s5_discretise",
    )(col(lam_re), col(lam_im), col(ls), b_re.reshape(n, SSM_GROUP), b_im.reshape(n, SSM_GROUP))


def _cmul(ar, ai, br, bi):
    return ar * br - ai * bi, ar * bi + ai * br


def _s5_kernel(u_ref, bw_ref, cw_ref, a_ref, d_ref, y_ref, bu_scr, x_scr, pow_scr, carry_scr):
    tau = pl.program_id(2)
    Lb = u_ref.shape[0]
    seg = Lb // SUBLANES
    ns = S5_NSTATE
    ar = a_ref[0:1, :]
    ai = a_ref[1:2, :]

    @pl.when(tau == 0)
    def _():
        carry_scr[...] = jnp.zeros_like(carry_scr)
        pr, pi = ar, ai
        for i in range(seg):
            pow_scr[i:i + 1, 0:ns] = pr
            pow_scr[i:i + 1, ns:2 * ns] = pi
            pr, pi = _cmul(pr, pi, ar, ai)

    u_perm = jnp.concatenate([u_ref[pl.ds(i, SUBLANES, stride=seg), :] for i in range(seg)], axis=0)
    bu_scr[...] = _dot(u_perm.astype(BF16), bw_ref[...])

    ar8 = jnp.broadcast_to(ar, (SUBLANES, ns))
    ai8 = jnp.broadcast_to(ai, (SUBLANES, ns))

    def scan_body(i, carry):
        xr, xi = carry
        r0 = pl.multiple_of(i * SUBLANES, SUBLANES)
        nr = ar8 * xr - ai8 * xi + bu_scr[pl.ds(r0, SUBLANES), 0:ns]
        ni = ar8 * xi + ai8 * xr + bu_scr[pl.ds(r0, SUBLANES), ns:2 * ns]
        x_scr[pl.ds(r0, SUBLANES), 0:ns] = nr
        x_scr[pl.ds(r0, SUBLANES), ns:2 * ns] = ni
        return nr, ni

    zeros = jnp.zeros((SUBLANES, ns), F32)
    er, ei = lax.fori_loop(0, seg, scan_body, (zeros, zeros), unroll=8)

    alr = pow_scr[seg - 1:seg, 0:ns]
    ali = pow_scr[seg - 1:seg, ns:2 * ns]
    pr = carry_scr[0:1, 0:ns]
    pi = carry_scr[0:1, ns:2 * ns]
    prs, pis = [], []
    for j in range(SUBLANES):
        prs.append(pr)
        pis.append(pi)
        mr, mi = _cmul(alr, ali, pr, pi)
        pr = er[j:j + 1, :] + mr
        pi = ei[j:j + 1, :] + mi
    carry_scr[0:1, 0:ns] = pr
    carry_scr[0:1, ns:2 * ns] = pi
    p_re = jnp.concatenate(prs, axis=0)
    p_im = jnp.concatenate(pis, axis=0)

    def fix_body(i, carry):
        r0 = pl.multiple_of(i * SUBLANES, SUBLANES)
        wr = jnp.broadcast_to(pow_scr[pl.ds(i, 1), 0:ns], (SUBLANES, ns))
        wi = jnp.broadcast_to(pow_scr[pl.ds(i, 1), ns:2 * ns], (SUBLANES, ns))
        mr, mi = _cmul(wr, wi, p_re, p_im)
        x_scr[pl.ds(r0, SUBLANES), 0:ns] += mr
        x_scr[pl.ds(r0, SUBLANES), ns:2 * ns] += mi
        return carry

    lax.fori_loop(0, seg, fix_body, 0, unroll=8)

    y = _dot(x_scr[...].astype(BF16), cw_ref[...]) + d_ref[...] * u_perm
    for i in range(seg):
        y_ref[pl.ds(i, SUBLANES, stride=seg), :] = y[i * SUBLANES:(i + 1) * SUBLANES, :]


def _s5_scan(h3, bw, cw, a_rows, d_skip):
    b, s, w = h3.shape
    Lb = min(S5_BLOCK, s)
    nb = w // LANES
    ns2 = 2 * S5_NSTATE
    return pl.pallas_call(
        _s5_kernel,
        grid=(b, nb, s // Lb),
        in_specs=[pl.BlockSpec((None, Lb, LANES), lambda i, k, c: (i, c, k)),
                  pl.BlockSpec((None, LANES, ns2), lambda i, k, c: (k, 0, 0)),
                  pl.BlockSpec((None, ns2, LANES), lambda i, k, c: (k, 0, 0)),
                  pl.BlockSpec((None, 2, S5_NSTATE), lambda i, k, c: (k, 0, 0)),
                  pl.BlockSpec((1, LANES), lambda i, k, c: (0, k))],
        out_specs=pl.BlockSpec((None, Lb, LANES), lambda i, k, c: (i, c, k)),
        out_shape=jax.ShapeDtypeStruct((b, s, w), F32),
        scratch_shapes=[pltpu.VMEM((Lb, ns2), F32), pltpu.VMEM((Lb, ns2), F32),
                        pltpu.VMEM((Lb // SUBLANES, ns2), F32), pltpu.VMEM((SUBLANES, ns2), F32)],
        compiler_params=_cparams(("parallel", "parallel", "arbitrary")),
        name="s5_scan",
    )(h3, bw, cw, a_rows, d_skip)


def _glu_kernel(y_ref, x_ref, wv_ref, wg_ref, bv_ref, bg_ref, o_ref, a_scr):
    @pl.when(pl.program_id(1) == 0)
    def _():
        a_scr[...] = jax.nn.gelu(y_ref[...]).astype(BF16)

    a = a_scr[...]
    val = _dot(a, wv_ref[...]) + bv_ref[...]
    gate = _dot(a, wg_ref[...]) + bg_ref[...]
    o_ref[...] = x_ref[...] + val * _sigmoid(gate)


def _glu(y2, x2, w_bf, bias):
    t, d = x2.shape
    tm = min(512, t)
    tn = 512
    nj = d // tn
    return pl.pallas_call(
        _glu_kernel,
        grid=(t // tm, nj),
        in_specs=[pl.BlockSpec((tm, d), lambda i, j: (i, 0)),
                  pl.BlockSpec((tm, tn), lambda i, j: (i, j)),
                  pl.BlockSpec((d, tn), lambda i, j: (0, j)),
                  pl.BlockSpec((d, tn), lambda i, j: (0, nj + j)),
                  pl.BlockSpec((1, tn), lambda i, j: (0, j)),
                  pl.BlockSpec((1, tn), lambda i, j: (0, nj + j))],
        out_specs=pl.BlockSpec((tm, tn), lambda i, j: (i, j)),
        out_shape=jax.ShapeDtypeStruct((t, d), F32),
        scratch_shapes=[pltpu.VMEM((tm, d), BF16)],
        compiler_params=_cparams(("parallel", "arbitrary")),
        name="glu",
    )(y2, x2, w_bf, w_bf, bias, bias)


def _block_diag(w):
    nb, gb, r, c = w.shape
    eye = jnp.eye(gb, dtype=w.dtype)
    return (w[:, :, :, None, :] * eye[None, :, None, :, None]).reshape(nb, gb * r, gb * c)


def _mixer_s5(h3, x2, lam_re, lam_im, log_step, b_re, b_im, c_re, c_im, d_skip, w_glu, b_glu):
    b, s, w = h3.shape
    g, p = lam_re.shape
    gb = S5_GROUPS_PER_BLOCK
    nb = g // gb
    a_re, a_im, bb_re, bb_im = _s5_discretise(lam_re, lam_im, log_step, b_re, b_im)
    a_rows = jnp.stack([a_re.reshape(nb, gb * p), a_im.reshape(nb, gb * p)], axis=1)
    bt = lambda m: jnp.swapaxes(m.reshape(nb, gb, p, SSM_GROUP), 2, 3)
    bw = jnp.concatenate([_block_diag(bt(bb_re)), _block_diag(bt(bb_im))], axis=2).astype(BF16)
    ct = lambda m: jnp.swapaxes(m.reshape(nb, gb, SSM_GROUP, p), 2, 3)
    cw = jnp.concatenate([_block_diag(ct(c_re)), -_block_diag(ct(c_im))], axis=1).astype(BF16)
    y = _s5_scan(h3, bw, cw, a_rows, d_skip.reshape(1, w))
    return _glu(y.reshape(b * s, w), x2, w_glu.astype(BF16), b_glu.reshape(1, -1))


def _mixer_ab(x2, bsz, norm_w, w_in, a_i_bias, a_f_bias, a_norm, b_conv, b_a_log, b_dt_bias, b_norm, w_out):
    t, d = x2.shape
    s = t // bsz
    zeros = lambda n: jnp.zeros((n,), F32)
    gate_cols = jnp.concatenate([w_in[:, 3072:3080], w_in[:, 7176:7192],
                                 jnp.zeros((d, LANES - G_END), F32)], axis=1)
    w_all = jnp.concatenate([w_in[:, :3072], w_in[:, 3080:7176], gate_cols], axis=1).astype(BF16)
    proj = _inproj(x2, norm_w.reshape(1, d), w_all).reshape(bsz, s, N_PROJ)
    gates_t = jnp.swapaxes(proj[:, :, COL_GATES:COL_GATES + GATE_ROWS], 1, 2)
    bias = jnp.concatenate([a_i_bias, a_f_bias, zeros(B_HEADS), b_dt_bias, zeros(LANES - G_END)])
    alog = jnp.concatenate([zeros(G_BA), b_a_log, zeros(LANES - G_END)])
    pc, pr = _gates(proj, gates_t, bias.reshape(1, LANES), bias[:GATE_ROWS].reshape(GATE_ROWS, 1),
                    alog.reshape(1, LANES), alog[:GATE_ROWS].reshape(GATE_ROWS, 1))
    ya = _mlstm(proj, pc, pr, a_norm.reshape(1, -1))
    yb = _gdn(proj, pc, pr, b_conv, b_norm.reshape(1, -1))
    n_a = A_HEADS * A_V_DIM
    w_out_bf = w_out.astype(BF16)
    return _outproj(ya.reshape(t, n_a), yb.reshape(t, -1), x2, w_out_bf[:n_a], w_out_bf[n_a:])


def kernel(x, mix_norm, ab_w_in, mlstm_i_bias, mlstm_f_bias, mlstm_norm, gdn_conv, gdn_a_log, gdn_dt_bias, gdn_norm, ab_w_out, ssm_lambda_re, ssm_lambda_im, ssm_log_step, ssm_b_re, ssm_b_im, ssm_c_re, ssm_c_im, ssm_d, glu_w, glu_b, ffn_norm, router_group_w, router_group_b, router_expert_w, router_expert_b, expert_w_gate, expert_w_up, expert_w_down, final_norm):
    bsz, s, d = x.shape
    t = bsz * s
    x2 = x.reshape(t, d)
    x2 = _mixer_ab(x2, bsz, mix_norm[0], ab_w_in[0], mlstm_i_bias[0], mlstm_f_bias[0], mlstm_norm[0],
                   gdn_conv[0], gdn_a_log[0], gdn_dt_bias[0], gdn_norm[0], ab_w_out[0])
    x2, h3 = _hier_moe(x2, ffn_norm[0], router_group_w[0], router_group_b[0], router_expert_w[0],
                       router_expert_b[0], expert_w_gate, expert_w_up, expert_w_down, 0, mix_norm[1])
    x2 = _mixer_s5(h3.reshape(bsz, s, d), x2, ssm_lambda_re[0], ssm_lambda_im[0], ssm_log_step[0],
                   ssm_b_re[0], ssm_b_im[0], ssm_c_re[0], ssm_c_im[0], ssm_d[0], glu_w[0], glu_b[0])
    _, out = _hier_moe(x2, ffn_norm[1], router_group_w[1], router_group_b[1], router_expert_w[1],
                       router_expert_b[1], expert_w_gate, expert_w_up, expert_w_down, 1, final_norm)
    return out.reshape(bsz, s, d)
```

```python
import functools

import jax
import jax.numpy as jnp
from jax import lax
from jax.experimental import pallas as pl
from jax.experimental.pallas import tpu as pltpu

F32 = jnp.float32
BF16 = jnp.bfloat16
HIGHEST = lax.Precision.HIGHEST

EPS = 1e-6
D_MODEL = 2048
A_HEADS = 4
A_QK_DIM = 128
A_V_DIM = 256
A_GATE_CAP = 15.0
B_HEADS = 8
B_HEAD_DIM = 128
B_CONV = 4
SSM_GROUP = 16
SSM_STATE = 64
MOE_GROUPS = 8
MOE_PER_GROUP = 8
MOE_EXPERTS = 64
MOE_FF = 768

LANES = 128
SUBLANES = 8
VMEM_LIMIT = 56 * 1024 * 1024

COL_AQ, COL_AK, COL_AV, COL_AO = 0, 512, 1024, 2048
COL_BQ, COL_BK, COL_BV, COL_BZ = 3072, 4096, 5120, 6144
N_PROJ = 7168
G_AI, G_AF, G_BB, G_BA, G_END = 0, 4, 8, 16, 24
GATE_ROWS = 32

MLSTM_CHUNK = 256
GDN_BLOCK = MLSTM_CHUNK
GDN_CHUNK = 64
GDN_HEADS_PER_STEP = 4
S5_BLOCK = 512
S5_GROUPS_PER_BLOCK = 8
S5_NSTATE = S5_GROUPS_PER_BLOCK * SSM_STATE
MOE_ROWS = 512
MOE_KT = 4
MOE_KW = D_MODEL // MOE_KT
MOE_DMA_GROUP_LOG2 = 4
MOE_DMA_GROUP = 1 << MOE_DMA_GROUP_LOG2


def _cparams(sem):
    return pltpu.CompilerParams(dimension_semantics=sem, vmem_limit_bytes=VMEM_LIMIT)


def _softcap(t, cap):
    return cap * jnp.tanh(t / cap)


def _log_sigmoid(t):
    return jnp.minimum(t, 0.0) - jnp.log(1.0 + jnp.exp(-jnp.abs(t)))


def _softplus(t):
    return jnp.maximum(t, 0.0) + jnp.log(1.0 + jnp.exp(-jnp.abs(t)))


def _sigmoid(t):
    return 1.0 / (1.0 + jnp.exp(-t))


def _silu(t):
    return t * _sigmoid(t)


def _pick_col(x, idx):
    lane = lax.broadcasted_iota(jnp.int32, x.shape, 1)
    return jnp.sum(jnp.where(lane == idx, x, 0.0), axis=-1, keepdims=True)


def _dot(a, b):
    return jnp.dot(a, b, preferred_element_type=F32)


def _dot_hi(a, b):
    return jnp.dot(a, b, precision=HIGHEST, preferred_element_type=F32)


def _dot_nt(a, b):
    return lax.dot_general(a, b, (((1,), (1,)), ((), ())), preferred_element_type=F32)


def _inproj_kernel(x_ref, g_ref, w_ref, wg_ref, o_ref, og_ref, h_scr):
    @pl.when(pl.program_id(1) == 0)
    def _():
        x = x_ref[...]
        ms = jnp.mean(x * x, axis=-1, keepdims=True)
        h_scr[...] = (x * lax.rsqrt(ms + EPS) * g_ref[...]).astype(BF16)
        og_ref[...] = _dot(h_scr[...], wg_ref[...])

    o_ref[...] = _dot(h_scr[...], w_ref[...])


def _inproj(x2, g, w_bf, wg_bf):
    t, d = x2.shape
    n = w_bf.shape[1]
    tm = min(1024, t)
    tn = 512
    return pl.pallas_call(
        _inproj_kernel,
        grid=(t // tm, n // tn),
        in_specs=[pl.BlockSpec((tm, d), lambda i, j: (i, 0)),
                  pl.BlockSpec((1, d), lambda i, j: (0, 0)),
                  pl.BlockSpec((d, tn), lambda i, j: (0, j)),
                  pl.BlockSpec((d, LANES), lambda i, j: (0, 0))],
        out_specs=[pl.BlockSpec((tm, tn), lambda i, j: (i, j)),
                   pl.BlockSpec((tm, LANES), lambda i, j: (i, 0))],
        out_shape=[jax.ShapeDtypeStruct((t, n), F32), jax.ShapeDtypeStruct((t, LANES), F32)],
        scratch_shapes=[pltpu.VMEM((tm, d), BF16)],
        compiler_params=_cparams(("parallel", "arbitrary")),
        name="inproj",
    )(x2, g, w_bf, wg_bf)


def _gate_values(g, alog, idx, cum_a, cum_b):
    sc = _softcap(g, A_GATE_CAP)
    cum_logf = cum_a(_log_sigmoid(sc))
    cum_g = cum_b(-jnp.exp(alog) * _softplus(g))
    return jnp.where(idx < G_AF, sc,
                     jnp.where(idx < G_BB, cum_logf,
                               jnp.where(idx < G_BA, _sigmoid(g), jnp.where(idx < G_END, cum_g, 0.0))))


def _gates_kernel(gc_ref, gr_ref, bc_ref, br_ref, alc_ref, alr_ref, pc_ref, pr_ref):
    L = gc_ref.shape[0]
    row = lax.broadcasted_iota(jnp.int32, (L, L), 0)
    col = lax.broadcasted_iota(jnp.int32, (L, L), 1)
    same = (row // GDN_CHUNK) == (col // GDN_CHUNK)
    tril = (row >= col).astype(F32)
    triu = (row <= col).astype(F32)
    blk_tril = (same & (row >= col)).astype(F32)
    blk_triu = (same & (row <= col)).astype(F32)
    lane = lax.broadcasted_iota(jnp.int32, (L, LANES), 1)
    pc_ref[...] = _gate_values(gc_ref[...] + bc_ref[...], alc_ref[...], lane,
                               lambda v: _dot_hi(tril, v), lambda v: _dot_hi(blk_tril, v))
    sub = lax.broadcasted_iota(jnp.int32, (GATE_ROWS, L), 0)
    pr_ref[...] = _gate_values(gr_ref[...] + br_ref[...], alr_ref[...], sub,
                               lambda v: _dot_hi(v, triu), lambda v: _dot_hi(v, blk_triu))


def _gates(gates, gates_t, bias_col, bias_row, alog_col, alog_row):
    b, s, _ = gates.shape
    L = min(MLSTM_CHUNK, s)
    return pl.pallas_call(
        _gates_kernel,
        grid=(b, s // L),
        in_specs=[pl.BlockSpec((None, L, LANES), lambda i, c: (i, c, 0)),
                  pl.BlockSpec((None, GATE_ROWS, L), lambda i, c: (i, 0, c)),
                  pl.BlockSpec((1, LANES), lambda i, c: (0, 0)),
                  pl.BlockSpec((GATE_ROWS, 1), lambda i, c: (0, 0)),
                  pl.BlockSpec((1, LANES), lambda i, c: (0, 0)),
                  pl.BlockSpec((GATE_ROWS, 1), lambda i, c: (0, 0))],
        out_specs=[pl.BlockSpec((None, L, LANES), lambda i, c: (i, c, 0)),
                   pl.BlockSpec((None, GATE_ROWS, L), lambda i, c: (i, 0, c))],
        out_shape=[jax.ShapeDtypeStruct((b, s, LANES), F32),
                   jax.ShapeDtypeStruct((b, GATE_ROWS, s), F32)],
        compiler_params=_cparams(("parallel", "parallel")),
        name="gates",
    )(gates, gates_t, bias_col, bias_row, alog_col, alog_row)


def _mlstm_kernel(q_ref, k_ref, v_ref, o_ref, pc_ref, pr_ref, nw_ref, out_ref, c_scr, n_scr, m_scr):
    h = pl.program_id(1)
    c = pl.program_id(2)
    L = q_ref.shape[0]

    @pl.when(c == 0)
    def _():
        c_scr[...] = jnp.zeros_like(c_scr)
        n_scr[...] = jnp.zeros_like(n_scr)
        m_scr[...] = jnp.zeros_like(m_scr)

    row = lax.broadcasted_iota(jnp.int32, (L, L), 0)
    col = lax.broadcasted_iota(jnp.int32, (L, L), 1)
    causal = row >= col

    pc = pc_ref[...]
    i_col = _pick_col(pc, G_AI + h)
    b_col = _pick_col(pc, G_AF + h)
    i_row = pr_ref[pl.ds(G_AI + h, 1), :]
    b_row = pr_ref[pl.ds(G_AF + h, 1), :]
    b_last = b_col[L - 1:L, :]

    m_prev = m_scr[...]
    log_d = jnp.where(causal, b_col - b_row + i_row, -jnp.inf)
    log_inter = b_col + m_prev
    m_t = jnp.maximum(log_inter, jnp.max(log_d, axis=-1, keepdims=True))
    dmat = jnp.exp(log_d - m_t)
    inter = jnp.exp(log_inter - m_t)

    q = q_ref[...]
    k = k_ref[...] * (A_QK_DIM ** -0.5)
    qb = q.astype(BF16)
    kb = k.astype(BF16)
    vb = v_ref[...].astype(BF16)
    scores = _dot_nt(qb, kb) * dmat
    c_mat = c_scr[...]
    n_vec = n_scr[...]
    num = inter * _dot(qb, c_mat.astype(BF16)) + _dot(scores.astype(BF16), vb)
    den = inter * jnp.sum(q * n_vec, axis=-1, keepdims=True) + jnp.sum(scores, axis=-1, keepdims=True)
    hh = num / jnp.maximum(jnp.abs(den), jnp.exp(-m_t))
    hh = hh * lax.rsqrt(jnp.mean(hh * hh, axis=-1, keepdims=True) + EPS) * nw_ref[...]
    out_ref[...] = hh * _sigmoid(o_ref[...])

    le_col = b_last - b_col + i_col
    m_new = jnp.maximum(b_last + m_prev, jnp.max(le_col, axis=0, keepdims=True))
    carry_scale = jnp.exp(b_last + m_prev - m_new)
    kw = k * jnp.exp(le_col - m_new)
    c_scr[...] = c_mat * carry_scale + _dot(kw.T.astype(BF16), vb)
    n_scr[...] = n_vec * carry_scale + jnp.sum(kw, axis=0, keepdims=True)
    m_scr[...] = m_new


def _mlstm(proj, pc, pr, norm_w):
    b, s, _ = proj.shape
    L = min(MLSTM_CHUNK, s)
    qb, kb = COL_AQ // A_QK_DIM, COL_AK // A_QK_DIM
    vb, ob = COL_AV // A_V_DIM, COL_AO // A_V_DIM
    return pl.pallas_call(
        _mlstm_kernel,
        grid=(b, A_HEADS, s // L),
        in_specs=[pl.BlockSpec((None, L, A_QK_DIM), lambda i, h, c: (i, c, qb + h)),
                  pl.BlockSpec((None, L, A_QK_DIM), lambda i, h, c: (i, c, kb + h)),
                  pl.BlockSpec((None, L, A_V_DIM), lambda i, h, c: (i, c, vb + h)),
                  pl.BlockSpec((None, L, A_V_DIM), lambda i, h, c: (i, c, ob + h)),
                  pl.BlockSpec((None, L, LANES), lambda i, h, c: (i, c, 0)),
                  pl.BlockSpec((None, GATE_ROWS, L), lambda i, h, c: (i, 0, c)),
                  pl.BlockSpec((1, A_V_DIM), lambda i, h, c: (0, h))],
        out_specs=pl.BlockSpec((None, L, A_V_DIM), lambda i, h, c: (i, c, h)),
        out_shape=jax.ShapeDtypeStruct((b, s, A_HEADS * A_V_DIM), F32),
        scratch_shapes=[pltpu.VMEM((A_QK_DIM, A_V_DIM), F32),
                        pltpu.VMEM((1, A_QK_DIM), F32),
                        pltpu.VMEM((1, 1), F32)],
        compiler_params=_cparams(("parallel", "parallel", "arbitrary")),
        name="mlstm",
    )(proj, proj, proj, proj, pc, pr, norm_w)


def _causal_conv_silu(x, tail, w):
    row8 = lax.broadcasted_iota(jnp.int32, (SUBLANES, x.shape[1]), 0)
    acc = x * w[B_CONV - 1:B_CONV, :]
    for d in range(1, B_CONV):
        rolled = pltpu.roll(x, d, 0)
        head = jnp.where(row8 < d, pltpu.roll(tail, d, 0), rolled[0:SUBLANES, :])
        shifted = jnp.concatenate([head, rolled[SUBLANES:, :]], axis=0)
        acc = acc + shifted * w[B_CONV - 1 - d:B_CONV - d, :]
    return _silu(acc)


def _l2norm(t):
    return t * lax.rsqrt(jnp.sum(t * t, axis=-1, keepdims=True) + EPS)


def _gdn_kernel(q_ref, k_ref, v_ref, z_ref, pc_ref, pr_ref, wq_ref, wk_ref, wv_ref, nw_ref, out_ref,
                s_scr, tq_scr, tk_scr, tv_scr):
    hp = pl.program_id(1)
    c = pl.program_id(2)
    Lb = q_ref.shape[0]
    hd = B_HEAD_DIM

    @pl.when(c == 0)
    def _():
        s_scr[...] = jnp.zeros_like(s_scr)
        tq_scr[...] = jnp.zeros_like(tq_scr)
        tk_scr[...] = jnp.zeros_like(tk_scr)
        tv_scr[...] = jnp.zeros_like(tv_scr)

    xq, xk, xv = q_ref[...], k_ref[...], v_ref[...]
    cq = _causal_conv_silu(xq, tq_scr[...], wq_ref[...])
    ck = _causal_conv_silu(xk, tk_scr[...], wk_ref[...])
    cv = _causal_conv_silu(xv, tv_scr[...], wv_ref[...])
    tq_scr[...] = xq[Lb - SUBLANES:, :]
    tk_scr[...] = xk[Lb - SUBLANES:, :]
    tv_scr[...] = xv[Lb - SUBLANES:, :]
    pc = pc_ref[...]
    nw = nw_ref[...]
    heads = [hp * GDN_HEADS_PER_STEP + hh for hh in range(GDN_HEADS_PER_STEP)]
    per_head = lambda x: jnp.stack([x[:, hh * hd:(hh + 1) * hd] for hh in range(GDN_HEADS_PER_STEP)], axis=0)
    hb, s_scr[...] = _gdn_heads(
        _l2norm(per_head(cq)) * (hd ** -0.5), _l2norm(per_head(ck)), per_head(cv),
        jnp.stack([_pick_col(pc, G_BB + h) for h in heads], axis=0),
        jnp.stack([_pick_col(pc, G_BA + h) for h in heads], axis=0),
        jnp.stack([pr_ref[pl.ds(G_BA + h, 1), :] for h in heads], axis=0), s_scr[...])
    hb = hb * lax.rsqrt(jnp.mean(hb * hb, axis=-1, keepdims=True) + EPS) * nw
    for hh in range(GDN_HEADS_PER_STEP):
        cols = slice(hh * hd, (hh + 1) * hd)
        out_ref[:, cols] = hb[hh] * _silu(z_ref[:, cols])


def _gdn_heads(q, k, v, beta, dec_col, dec_row, state):
    nh, Lb, _ = q.shape
    C = GDN_CHUNK
    nsub = Lb // C
    bmm = lambda a, b: jnp.einsum("bij,bjk->bik", a.astype(BF16), b.astype(BF16), preferred_element_type=F32)
    bmm_nt = lambda a, b: jnp.einsum("bik,bjk->bij", a.astype(BF16), b.astype(BF16), preferred_element_type=F32)
    r64 = lax.broadcasted_iota(jnp.int32, (C, C), 0)
    c64 = lax.broadcasted_iota(jnp.int32, (C, C), 1)
    causal = r64 >= c64
    strict = r64 > c64
    eye = (r64 == c64).astype(F32)

    kbeta = k * beta
    edec = jnp.exp(dec_col)
    q_dec = (q * edec).astype(BF16)
    vbeta = (v * beta).astype(BF16)
    kbdec = (kbeta * edec).astype(BF16)
    kb16 = k.astype(BF16)
    qb16 = q.astype(BF16)
    kbeta16 = kbeta.astype(BF16)

    segs, nmats = [], []
    for j in range(nsub):
        sl = slice(j * C, (j + 1) * C)
        seg = jnp.exp(jnp.where(causal, dec_col[:, sl, :] - dec_row[:, :, sl], -jnp.inf))
        a_low = jnp.where(strict, bmm_nt(kbeta16[:, sl, :], kb16[:, sl, :]) * seg, 0.0)
        segs.append(seg)
        nmats.append(-a_low)
    nmat = jnp.concatenate(nmats, axis=0)
    tmat = eye[None] + nmat
    npow = bmm(nmat, nmat)
    for _ in range(4):
        tmat, npow = tmat + bmm(tmat, npow), bmm(npow, npow)
    tmat = tmat + bmm(tmat, npow)

    outs = []
    for j in range(nsub):
        sl = slice(j * C, (j + 1) * C)
        t16 = tmat[j * nh:(j + 1) * nh].astype(BF16)
        u = bmm(t16, vbeta[:, sl, :])
        w = bmm(t16, kbdec[:, sl, :])
        attn = bmm_nt(qb16[:, sl, :], kb16[:, sl, :]) * segs[j]
        s16 = state.astype(BF16)
        v_new = u - bmm(w, s16)
        o = bmm(q_dec[:, sl, :], s16) + bmm(attn, v_new)
        d_last = dec_col[:, (j + 1) * C - 1:(j + 1) * C, :]
        k_end = k[:, sl, :] * jnp.exp(d_last - dec_col[:, sl, :])
        state = state * jnp.exp(d_last) + bmm(jnp.swapaxes(k_end, 1, 2), v_new)
        outs.append(o)
    return jnp.concatenate(outs, axis=1), state


def _gdn(proj, pc, pr, conv_w, norm_w):
    b, s, _ = proj.shape
    Lb = min(GDN_BLOCK, s)
    hd = B_HEAD_DIM
    wd = GDN_HEADS_PER_STEP * hd
    npair = B_HEADS // GDN_HEADS_PER_STEP
    qb, kb, vb, zb = COL_BQ // wd, COL_BK // wd, COL_BV // wd, COL_BZ // wd
    blk = lambda off: pl.BlockSpec((None, Lb, wd), lambda i, h, c: (i, c, off + h))
    return pl.pallas_call(
        _gdn_kernel,
        grid=(b, npair, s // Lb),
        in_specs=[blk(qb), blk(kb), blk(vb), blk(zb),
                  pl.BlockSpec((None, Lb, LANES), lambda i, h, c: (i, c, 0)),
                  pl.BlockSpec((None, GATE_ROWS, Lb), lambda i, h, c: (i, 0, c)),
                  pl.BlockSpec((B_CONV, wd), lambda i, h, c: (0, h)),
                  pl.BlockSpec((B_CONV, wd), lambda i, h, c: (0, npair + h)),
                  pl.BlockSpec((B_CONV, wd), lambda i, h, c: (0, 2 * npair + h)),
                  pl.BlockSpec((1, hd), lambda i, h, c: (0, 0))],
        out_specs=pl.BlockSpec((None, Lb, wd), lambda i, h, c: (i, c, h)),
        out_shape=jax.ShapeDtypeStruct((b, s, B_HEADS * hd), F32),
        scratch_shapes=[pltpu.VMEM((GDN_HEADS_PER_STEP, hd, hd), F32),
                        pltpu.VMEM((SUBLANES, wd), F32),
                        pltpu.VMEM((SUBLANES, wd), F32),
                        pltpu.VMEM((SUBLANES, wd), F32)],
        compiler_params=_cparams(("parallel", "parallel", "arbitrary")),
        name="gdn",
    )(proj, proj, proj, proj, pc, pr, conv_w, conv_w, conv_w, norm_w)


def _outproj_kernel(ya_ref, yb_ref, x_ref, wa_ref, wb_ref, o_ref):
    acc = _dot(ya_ref[...].astype(BF16), wa_ref[...]) + _dot(yb_ref[...].astype(BF16), wb_ref[...])
    o_ref[...] = x_ref[...] + acc


def _outproj(ya, yb, x2, wa, wb):
    t, d = x2.shape
    ka, kb = ya.shape[1], yb.shape[1]
    tm = min(512, t)
    tn = 512
    return pl.pallas_call(
        _outproj_kernel,
        grid=(t // tm, d // tn),
        in_specs=[pl.BlockSpec((tm, ka), lambda i, j: (i, 0)),
                  pl.BlockSpec((tm, kb), lambda i, j: (i, 0)),
                  pl.BlockSpec((tm, tn), lambda i, j: (i, j)),
                  pl.BlockSpec((ka, tn), lambda i, j: (0, j)),
                  pl.BlockSpec((kb, tn), lambda i, j: (0, j))],
        out_specs=pl.BlockSpec((tm, tn), lambda i, j: (i, j)),
        out_shape=jax.ShapeDtypeStruct((t, d), F32),
        compiler_params=_cparams(("parallel", "arbitrary")),
        name="outproj",
    )(ya, yb, x2, wa, wb)


def _router_kernel(x_ref, g_ref, w_ref, b_ref, h_ref, ids_ref, gate_ref):
    x = x_ref[...]
    ms = jnp.mean(x * x, axis=-1, keepdims=True)
    h = x * lax.rsqrt(ms + EPS) * g_ref[...]
    h_ref[...] = h
    logits = _dot_hi(h, w_ref[...]) + b_ref[...]
    lane = lax.broadcasted_iota(jnp.int32, logits.shape, 1)
    neg = -jnp.inf
    big = jnp.int32(1 << 20)
    is_g = (lane >= MOE_EXPERTS) & (lane < MOE_EXPERTS + MOE_GROUPS)
    gl = jnp.where(is_g, logits, neg)
    gmax = jnp.max(gl, axis=-1, keepdims=True)
    g_lane = jnp.min(jnp.where(gl == gmax, lane, big), axis=-1, keepdims=True)
    g_idx = g_lane - MOE_EXPERTS
    g_w = 1.0 / jnp.sum(jnp.exp(gl - gmax), axis=-1, keepdims=True)
    in_grp = (lane >= g_idx * MOE_PER_GROUP) & (lane < (g_idx + 1) * MOE_PER_GROUP)
    el = jnp.where(in_grp, logits, neg)
    e0 = jnp.max(el, axis=-1, keepdims=True)
    l0 = jnp.min(jnp.where(el == e0, lane, big), axis=-1, keepdims=True)
    el1 = jnp.where(lane == l0, neg, el)
    e1 = jnp.max(el1, axis=-1, keepdims=True)
    l1 = jnp.min(jnp.where(el1 == e1, lane, big), axis=-1, keepdims=True)
    r = jnp.exp(e1 - e0)
    p0 = 1.0 / (1.0 + r)
    p1 = r / (1.0 + r)
    ids_ref[...] = jnp.where(lane == 0, l0, jnp.where(lane == 1, l1, 0))
    gate_ref[...] = jnp.where(lane == 0, g_w * p0, jnp.where(lane == 1, g_w * p1, 0.0))


def _router(x2, g, w, bias):
    t, d = x2.shape
    tm = min(256, t)
    return pl.pallas_call(
        _router_kernel,
        grid=(t // tm,),
        in_specs=[pl.BlockSpec((tm, d), lambda i: (i, 0)),
                  pl.BlockSpec((1, d), lambda i: (0, 0)),
                  pl.BlockSpec((d, LANES), lambda i: (0, 0)),
                  pl.BlockSpec((1, LANES), lambda i: (0, 0))],
        out_specs=[pl.BlockSpec((tm, d), lambda i: (i, 0)),
                   pl.BlockSpec((tm, LANES), lambda i: (i, 0)),
                   pl.BlockSpec((tm, LANES), lambda i: (i, 0))],
        out_shape=[jax.ShapeDtypeStruct((t, d), F32),
                   jax.ShapeDtypeStruct((t, LANES), jnp.int32),
                   jax.ShapeDtypeStruct((t, LANES), F32)],
        compiler_params=_cparams(("parallel",)),
        name="router",
    )(x2, g, w, bias)


def _moe_plan(expert, n_blocks):
    t = expert.shape[0]
    n_assign = 2 * t
    flat_e = expert.reshape(n_assign)
    onehot = (flat_e[:, None] == jnp.arange(MOE_EXPERTS, dtype=jnp.int32)[None, :]).astype(jnp.int32)
    csum = jnp.cumsum(onehot, axis=0)
    rank = jnp.sum(onehot * csum, axis=1) - 1
    counts = csum[-1]
    nblk = (counts + MOE_ROWS - 1) // MOE_ROWS
    blk_end = jnp.cumsum(nblk)
    blk_start = blk_end - nblk
    dest = jnp.sum(onehot * blk_start[None, :], axis=1) * MOE_ROWS + rank
    spare = 2 * (t + jnp.arange(n_blocks * MOE_ROWS, dtype=jnp.int32) % MOE_DMA_GROUP)
    row_dst = spare.at[dest].set(jnp.arange(n_assign, dtype=jnp.int32))
    bid = jnp.arange(n_blocks, dtype=jnp.int32)
    total = blk_end[-1]
    be = jnp.minimum(jnp.sum((bid[:, None] >= blk_end[None, :]).astype(jnp.int32), axis=1), MOE_EXPERTS - 1)
    used = bid < total
    blk_n = jnp.where(used, jnp.clip(counts[be] - (bid - blk_start[be]) * MOE_ROWS, 0, MOE_ROWS), 0)
    blk_e = jnp.where(used, be, be[jnp.maximum(total - 1, 0)])
    return row_dst, blk_e.astype(jnp.int32), blk_n.astype(jnp.int32)


def _moe_kernel(blk_e_ref, blk_n_ref, dst_ref, h_hbm, wg_ref, wu_ref, wd_ref, yk_hbm,
                xf_scr, xb_scr, g_scr, u_scr, y_scr, z_scr, gsem, ssem):
    b = pl.program_id(0)
    k = pl.program_id(1)
    nb = pl.num_programs(0)
    n = blk_n_ref[b]
    R = xf_scr.shape[0]

    t = h_hbm.shape[0]

    def gather_copy(blk, r):
        tok = jnp.minimum(lax.shift_right_logical(dst_ref[blk * R + r], 1), t - 1)
        return pltpu.make_async_copy(h_hbm.at[pl.ds(tok, 1), :], xf_scr.at[pl.ds(r, 1), :], gsem)

    def scatter_copy(blk, r):
        dst = dst_ref[blk * R + r]
        tok = lax.shift_right_logical(dst, 1)
        return pltpu.make_async_copy(y_scr.at[pl.ds(r, 1), :], yk_hbm.at[dst & 1, pl.ds(tok, 1), :], ssem)

    def for_rows(blk, fn):
        def body(g, carry):
            for j in range(MOE_DMA_GROUP):
                fn(blk, g * MOE_DMA_GROUP + j)
            return carry
        lax.fori_loop(0, lax.shift_right_logical(blk_n_ref[blk] + (MOE_DMA_GROUP - 1), MOE_DMA_GROUP_LOG2), body, 0)

    @pl.when((b == 0) & (k == 0))
    def _():
        xf_scr[...] = jnp.zeros_like(xf_scr)
        for_rows(0, lambda blk, r: gather_copy(blk, r).start())
        z_scr[...] = jnp.zeros_like(z_scr)
        spare = [pltpu.make_async_copy(z_scr, yk_hbm.at[s, pl.ds(t, MOE_DMA_GROUP), :], ssem) for s in range(2)]
        for cp in spare:
            cp.start()
        for cp in spare:
            cp.wait()

    @pl.when(k == 0)
    def _():
        for_rows(b, lambda blk, r: gather_copy(blk, r).wait())
        for kk in range(MOE_KT):
            xb_scr[kk] = xf_scr[:, kk * MOE_KW:(kk + 1) * MOE_KW].astype(BF16)

    @pl.when((k == 1) & (b + 1 < nb))
    def _():
        for_rows(b + 1, lambda blk, r: gather_copy(blk, r).start())

    @pl.when(n > 0)
    def _():
        x = xb_scr[k]
        pg = _dot(x, wg_ref[...].astype(BF16))
        pu = _dot(x, wu_ref[...].astype(BF16))

        @pl.when(k == 0)
        def _():
            g_scr[...] = pg
            u_scr[...] = pu

        @pl.when(k != 0)
        def _():
            g_scr[...] += pg
            u_scr[...] += pu

    @pl.when(k == MOE_KT - 1)
    def _():
        @pl.when(b > 0)
        def _():
            for_rows(b - 1, lambda blk, r: scatter_copy(blk, r).wait())

        @pl.when(n > 0)
        def _():
            hmid = _silu(g_scr[...]) * u_scr[...]
            y_scr[...] = _dot(hmid.astype(BF16), wd_ref[...].astype(BF16))
            for_rows(b, lambda blk, r: scatter_copy(blk, r).start())

        @pl.when(b == nb - 1)
        def _():
            for_rows(b, lambda blk, r: scatter_copy(blk, r).wait())


def _moe_experts(h2, row_dst, blk_e, blk_n, w_gate, w_up, w_down, layer, n_blocks):
    t, d = h2.shape
    R = MOE_ROWS
    last = MOE_KT - 1

    def w_in_map(b, k, be, bn, dst):
        return (layer, be[b], jnp.where(bn[b] > 0, k, last), 0)

    def w_out_map(b, k, be, bn, dst):
        return (layer, be[b], 0, 0)

    grid_spec = pltpu.PrefetchScalarGridSpec(
        num_scalar_prefetch=3,
        grid=(n_blocks, MOE_KT),
        in_specs=[pl.BlockSpec(memory_space=pl.ANY),
                  pl.BlockSpec((None, None, MOE_KW, MOE_FF), w_in_map),
                  pl.BlockSpec((None, None, MOE_KW, MOE_FF), w_in_map),
                  pl.BlockSpec((None, None, MOE_FF, d), w_out_map)],
        out_specs=pl.BlockSpec(memory_space=pl.ANY),
        scratch_shapes=[pltpu.VMEM((R, d), F32), pltpu.VMEM((MOE_KT, R, MOE_KW), BF16),
                        pltpu.VMEM((R, MOE_FF), F32), pltpu.VMEM((R, MOE_FF), F32), pltpu.VMEM((R, d), F32),
                        pltpu.VMEM((MOE_DMA_GROUP, d), F32),
                        pltpu.SemaphoreType.DMA(()), pltpu.SemaphoreType.DMA(())],
    )
    return pl.pallas_call(
        _moe_kernel,
        grid_spec=grid_spec,
        out_shape=jax.ShapeDtypeStruct((2, t + MOE_DMA_GROUP, d), F32),
        compiler_params=_cparams(("arbitrary", "arbitrary")),
        name="moe_experts",
    )(blk_e, blk_n, row_dst, h2, w_gate, w_up, w_down)


def _combine_kernel(y0_ref, y1_ref, x_ref, gate_ref, g_ref, o_ref, hn_ref):
    gate = gate_ref[...]
    x = x_ref[...] + gate[:, 0:1] * y0_ref[...] + gate[:, 1:2] * y1_ref[...]
    o_ref[...] = x
    ms = jnp.mean(x * x, axis=-1, keepdims=True)
    hn_ref[...] = x * lax.rsqrt(ms + EPS) * g_ref[...]


def _moe_combine(yk, x2, gate, next_norm):
    t, d = x2.shape
    tm = min(256, t)
    return pl.pallas_call(
        _combine_kernel,
        grid=(t // tm,),
        in_specs=[pl.BlockSpec((None, tm, d), lambda i: (0, i, 0)),
                  pl.BlockSpec((None, tm, d), lambda i: (1, i, 0)),
                  pl.BlockSpec((tm, d), lambda i: (i, 0)),
                  pl.BlockSpec((tm, LANES), lambda i: (i, 0)),
                  pl.BlockSpec((1, d), lambda i: (0, 0))],
        out_specs=[pl.BlockSpec((tm, d), lambda i: (i, 0)),
                   pl.BlockSpec((tm, d), lambda i: (i, 0))],
        out_shape=[jax.ShapeDtypeStruct((t, d), F32), jax.ShapeDtypeStruct((t, d), F32)],
        compiler_params=_cparams(("parallel",)),
        name="moe_combine",
    )(yk, yk, x2, gate, next_norm)


def _hier_moe(x2, ffn_norm, w_rg, b_rg, w_re, b_re, w_gate, w_up, w_down, layer, next_norm):
    t, d = x2.shape
    pad = LANES - MOE_EXPERTS - MOE_GROUPS
    w_r = jnp.concatenate([w_re, w_rg, jnp.zeros((d, pad), F32)], axis=1)
    b_r = jnp.concatenate([b_re, b_rg, jnp.zeros((pad,), F32)]).reshape(1, LANES)
    h2, ids, gate = _router(x2, ffn_norm.reshape(1, d), w_r, b_r)
    n_blocks = (2 * t) // MOE_ROWS + MOE_EXPERTS
    row_dst, blk_e, blk_n = _moe_plan(ids[:, :2], n_blocks)
    yk = _moe_experts(h2, row_dst, blk_e, blk_n, w_gate, w_up, w_down, layer, n_blocks)
    return _moe_combine(yk, x2, gate, next_norm.reshape(1, d))


def _s5_disc_kernel(lre_ref, lim_ref, ls_ref, bre_ref, bim_ref, are_ref, aim_ref, bbre_ref, bbim_ref):
    lr = jnp.minimum(lre_ref[...], -1e-4)
    li = lim_ref[...]
    step = jnp.exp(ls_ref[...])
    mag = jnp.exp(lr * step)
    ang = li * step
    ab_re = mag * jnp.cos(ang)
    ab_im = mag * jnp.sin(ang)
    den = lr * lr + li * li
    zr = ab_re - 1.0
    f_re = (zr * lr + ab_im * li) / den
    f_im = (ab_im * lr - zr * li) / den
    br = bre_ref[...]
    bi = bim_ref[...]
    are_ref[...] = ab_re
    aim_ref[...] = ab_im
    bbre_ref[...] = f_re * br - f_im * bi
    bbim_ref[...] = f_re * bi + f_im * br


def _s5_discretise(lam_re, lam_im, log_step, b_re, b_im):
    g, p = lam_re.shape
    n = g * p
    col = lambda a: a.reshape(n, 1)
    ls = jnp.broadcast_to(log_step[:, None], (g, p))
    rows = 1024
    full = lambda w: pl.BlockSpec((rows, w), lambda i: (i, 0))
    return pl.pallas_call(
        _s5_disc_kernel,
        grid=(n // rows,),
        in_specs=[full(1), full(1), full(1), full(SSM_GROUP), full(SSM_GROUP)],
        out_specs=[full(1), full(1), full(SSM_GROUP), full(SSM_GROUP)],
        out_shape=[jax.ShapeDtypeStruct((n, 1), F32), jax.ShapeDtypeStruct((n, 1), F32),
                   jax.ShapeDtypeStruct((n, SSM_GROUP), F32), jax.ShapeDtypeStruct((n, SSM_GROUP), F32)],
        compiler_params=_cparams(("parallel",)),
        name="s5_discretise",
    )(col(lam_re), col(lam_im), col(ls), b_re.reshape(n, SSM_GROUP), b_im.reshape(n, SSM_GROUP))


def _cmul(ar, ai, br, bi):
    return ar * br - ai * bi, ar * bi + ai * br


def _s5_kernel(u_ref, bw_ref, cw_ref, a_ref, d_ref, y_ref, bu_scr, x_scr, pow_scr, carry_scr):
    tau = pl.program_id(2)
    Lb = u_ref.shape[0]
    seg = Lb // SUBLANES
    ns = S5_NSTATE
    ar = a_ref[0:1, :]
    ai = a_ref[1:2, :]

    @pl.when(tau == 0)
    def _():
        carry_scr[...] = jnp.zeros_like(carry_scr)
        pr, pi = ar, ai
        for i in range(seg):
            pow_scr[i:i + 1, 0:ns] = pr
            pow_scr[i:i + 1, ns:2 * ns] = pi
            pr, pi = _cmul(pr, pi, ar, ai)

    u_perm = jnp.concatenate([u_ref[pl.ds(i, SUBLANES, stride=seg), :] for i in range(seg)], axis=0)
    bu_scr[...] = _dot(u_perm.astype(BF16), bw_ref[...])

    ar8 = jnp.broadcast_to(ar, (SUBLANES, ns))
    ai8 = jnp.broadcast_to(ai, (SUBLANES, ns))

    def scan_body(i, carry):
        xr, xi = carry
        r0 = pl.multiple_of(i * SUBLANES, SUBLANES)
        nr = ar8 * xr - ai8 * xi + bu_scr[pl.ds(r0, SUBLANES), 0:ns]
        ni = ar8 * xi + ai8 * xr + bu_scr[pl.ds(r0, SUBLANES), ns:2 * ns]
        x_scr[pl.ds(r0, SUBLANES), 0:ns] = nr
        x_scr[pl.ds(r0, SUBLANES), ns:2 * ns] = ni
        return nr, ni

    zeros = jnp.zeros((SUBLANES, ns), F32)
    er, ei = lax.fori_loop(0, seg, scan_body, (zeros, zeros), unroll=8)

    alr = pow_scr[seg - 1:seg, 0:ns]
    ali = pow_scr[seg - 1:seg, ns:2 * ns]
    pr = carry_scr[0:1, 0:ns]
    pi = carry_scr[0:1, ns:2 * ns]
    prs, pis = [], []
    for j in range(SUBLANES):
        prs.append(pr)
        pis.append(pi)
        mr, mi = _cmul(alr, ali, pr, pi)
        pr = er[j:j + 1, :] + mr
        pi = ei[j:j + 1, :] + mi
    carry_scr[0:1, 0:ns] = pr
    carry_scr[0:1, ns:2 * ns] = pi
    p_re = jnp.concatenate(prs, axis=0)
    p_im = jnp.concatenate(pis, axis=0)

    def fix_body(i, carry):
        r0 = pl.multiple_of(i * SUBLANES, SUBLANES)
        wr = jnp.broadcast_to(pow_scr[pl.ds(i, 1), 0:ns], (SUBLANES, ns))
        wi = jnp.broadcast_to(pow_scr[pl.ds(i, 1), ns:2 * ns], (SUBLANES, ns))
        mr, mi = _cmul(wr, wi, p_re, p_im)
        x_scr[pl.ds(r0, SUBLANES), 0:ns] += mr
        x_scr[pl.ds(r0, SUBLANES), ns:2 * ns] += mi
        return carry

    lax.fori_loop(0, seg, fix_body, 0, unroll=8)

    y = _dot(x_scr[...].astype(BF16), cw_ref[...]) + d_ref[...] * u_perm
    for i in range(seg):
        y_ref[pl.ds(i, SUBLANES, stride=seg), :] = y[i * SUBLANES:(i + 1) * SUBLANES, :]


def _s5_scan(h3, bw, cw, a_rows, d_skip):
    b, s, w = h3.shape
    Lb = min(S5_BLOCK, s)
    nb = w // LANES
    ns2 = 2 * S5_NSTATE
    return pl.pallas_call(
        _s5_kernel,
        grid=(b, nb, s // Lb),
        in_specs=[pl.BlockSpec((None, Lb, LANES), lambda i, k, c: (i, c, k)),
                  pl.BlockSpec((None, LANES, ns2), lambda i, k, c: (k, 0, 0)),
                  pl.BlockSpec((None, ns2, LANES), lambda i, k, c: (k, 0, 0)),
                  pl.BlockSpec((None, 2, S5_NSTATE), lambda i, k, c: (k, 0, 0)),
                  pl.BlockSpec((1, LANES), lambda i, k, c: (0, k))],
        out_specs=pl.BlockSpec((None, Lb, LANES), lambda i, k, c: (i, c, k)),
        out_shape=jax.ShapeDtypeStruct((b, s, w), F32),
        scratch_shapes=[pltpu.VMEM((Lb, ns2), F32), pltpu.VMEM((Lb, ns2), F32),
                        pltpu.VMEM((Lb // SUBLANES, ns2), F32), pltpu.VMEM((SUBLANES, ns2), F32)],
        compiler_params=_cparams(("parallel", "parallel", "arbitrary")),
        name="s5_scan",
    )(h3, bw, cw, a_rows, d_skip)


def _glu_kernel(y_ref, x_ref, wv_ref, wg_ref, bv_ref, bg_ref, o_ref, a_scr):
    @pl.when(pl.program_id(1) == 0)
    def _():
        a_scr[...] = jax.nn.gelu(y_ref[...]).astype(BF16)

    a = a_scr[...]
    val = _dot(a, wv_ref[...]) + bv_ref[...]
    gate = _dot(a, wg_ref[...]) + bg_ref[...]
    o_ref[...] = x_ref[...] + val * _sigmoid(gate)


def _glu(y2, x2, w_bf, bias):
    t, d = x2.shape
    tm = min(512, t)
    tn = 512
    nj = d // tn
    return pl.pallas_call(
        _glu_kernel,
        grid=(t // tm, nj),
        in_specs=[pl.BlockSpec((tm, d), lambda i, j: (i, 0)),
                  pl.BlockSpec((tm, tn), lambda i, j: (i, j)),
                  pl.BlockSpec((d, tn), lambda i, j: (0, j)),
                  pl.BlockSpec((d, tn), lambda i, j: (0, nj + j)),
                  pl.BlockSpec((1, tn), lambda i, j: (0, j)),
                  pl.BlockSpec((1, tn), lambda i, j: (0, nj + j))],
        out_specs=pl.BlockSpec((tm, tn), lambda i, j: (i, j)),
        out_shape=jax.ShapeDtypeStruct((t, d), F32),
        scratch_shapes=[pltpu.VMEM((tm, d), BF16)],
        compiler_params=_cparams(("parallel", "arbitrary")),
        name="glu",
    )(y2, x2, w_bf, w_bf, bias, bias)


def _block_diag(w):
    nb, gb, r, c = w.shape
    eye = jnp.eye(gb, dtype=w.dtype)
    return (w[:, :, :, None, :] * eye[None, :, None, :, None]).reshape(nb, gb * r, gb * c)


def _mixer_s5(h3, x2, lam_re, lam_im, log_step, b_re, b_im, c_re, c_im, d_skip, w_glu, b_glu):
    b, s, w = h3.shape
    g, p = lam_re.shape
    gb = S5_GROUPS_PER_BLOCK
    nb = g // gb
    a_re, a_im, bb_re, bb_im = _s5_discretise(lam_re, lam_im, log_step, b_re, b_im)
    a_rows = jnp.stack([a_re.reshape(nb, gb * p), a_im.reshape(nb, gb * p)], axis=1)
    bt = lambda m: jnp.swapaxes(m.reshape(nb, gb, p, SSM_GROUP), 2, 3)
    bw = jnp.concatenate([_block_diag(bt(bb_re)), _block_diag(bt(bb_im))], axis=2).astype(BF16)
    ct = lambda m: jnp.swapaxes(m.reshape(nb, gb, SSM_GROUP, p), 2, 3)
    cw = jnp.concatenate([_block_diag(ct(c_re)), -_block_diag(ct(c_im))], axis=1).astype(BF16)
    y = _s5_scan(h3, bw, cw, a_rows, d_skip.reshape(1, w))
    return _glu(y.reshape(b * s, w), x2, w_glu.astype(BF16), b_glu.reshape(1, -1))


def _mixer_ab(x2, bsz, norm_w, w_in, a_i_bias, a_f_bias, a_norm, b_conv, b_a_log, b_dt_bias, b_norm, w_out):
    t, d = x2.shape
    s = t // bsz
    zeros = lambda n: jnp.zeros((n,), F32)
    n_a = 2 * A_HEADS * A_QK_DIM + 2 * A_HEADS * A_V_DIM
    n_b = 4 * B_HEADS * B_HEAD_DIM
    a_end = n_a + 2 * A_HEADS
    w_gates = jnp.concatenate([w_in[:, n_a:a_end], w_in[:, a_end + n_b:],
                               jnp.zeros((d, LANES - G_END), F32)], axis=1).astype(BF16)
    w_wide = jnp.concatenate([w_in[:, :n_a], w_in[:, a_end:a_end + n_b]], axis=1).astype(BF16)
    proj, gates = _inproj(x2, norm_w.reshape(1, d), w_wide, w_gates)
    proj = proj.reshape(bsz, s, N_PROJ)
    gates = gates.reshape(bsz, s, LANES)
    gates_t = jnp.swapaxes(gates[:, :, :GATE_ROWS], 1, 2)
    bias = jnp.concatenate([a_i_bias, a_f_bias, zeros(B_HEADS), b_dt_bias, zeros(LANES - G_END)])
    alog = jnp.concatenate([zeros(G_BA), b_a_log, zeros(LANES - G_END)])
    pc, pr = _gates(gates, gates_t, bias.reshape(1, LANES), bias[:GATE_ROWS].reshape(GATE_ROWS, 1),
                    alog.reshape(1, LANES), alog[:GATE_ROWS].reshape(GATE_ROWS, 1))
    ya = _mlstm(proj, pc, pr, a_norm.reshape(1, -1))
    yb = _gdn(proj, pc, pr, b_conv, b_norm.reshape(1, -1))
    n_ya = A_HEADS * A_V_DIM
    w_out_bf = w_out.astype(BF16)
    return _outproj(ya.reshape(t, n_ya), yb.reshape(t, -1), x2, w_out_bf[:n_ya], w_out_bf[n_ya:])


def kernel(x, mix_norm, ab_w_in, mlstm_i_bias, mlstm_f_bias, mlstm_norm, gdn_conv, gdn_a_log, gdn_dt_bias, gdn_norm, ab_w_out, ssm_lambda_re, ssm_lambda_im, ssm_log_step, ssm_b_re, ssm_b_im, ssm_c_re, ssm_c_im, ssm_d, glu_w, glu_b, ffn_norm, router_group_w, router_group_b, router_expert_w, router_expert_b, expert_w_gate, expert_w_up, expert_w_down, final_norm):
    bsz, s, d = x.shape
    t = bsz * s
    x2 = x.reshape(t, d)
    x2 = _mixer_ab(x2, bsz, mix_norm[0], ab_w_in[0], mlstm_i_bias[0], mlstm_f_bias[0], mlstm_norm[0],
                   gdn_conv[0], gdn_a_log[0], gdn_dt_bias[0], gdn_norm[0], ab_w_out[0])
    x2, h3 = _hier_moe(x2, ffn_norm[0], router_group_w[0], router_group_b[0], router_expert_w[0],
                       router_expert_b[0], expert_w_gate, expert_w_up, expert_w_down, 0, mix_norm[1])
    x2 = _mixer_s5(h3.reshape(bsz, s, d), x2, ssm_lambda_re[0], ssm_lambda_im[0], ssm_log_step[0],
                   ssm_b_re[0], ssm_b_im[0], ssm_c_re[0], ssm_c_im[0], ssm_d[0], glu_w[0], glu_b[0])
    _, out = _hier_moe(x2, ffn_norm[1], router_group_w[1], router_group_b[1], router_expert_w[1],
                       router_expert_b[1], expert_w_gate, expert_w_up, expert_w_down, 1, final_norm)
    return out.reshape(bsz, s, d)
```

```python
import functools

import jax
import jax.numpy as jnp
from jax import lax
from jax.experimental import pallas as pl
from jax.experimental.pallas import tpu as pltpu

F32 = jnp.float32
BF16 = jnp.bfloat16
HIGHEST = lax.Precision.HIGHEST

EPS = 1e-6
D_MODEL = 2048
A_HEADS = 4
A_QK_DIM = 128
A_V_DIM = 256
A_GATE_CAP = 15.0
B_HEADS = 8
B_HEAD_DIM = 128
B_CONV = 4
SSM_GROUP = 16
SSM_STATE = 64
MOE_GROUPS = 8
MOE_PER_GROUP = 8
MOE_EXPERTS = 64
MOE_FF = 768

LANES = 128
SUBLANES = 8
VMEM_LIMIT = 56 * 1024 * 1024

COL_AQ, COL_AK, COL_AV, COL_AO = 0, 512, 1024, 2048
COL_BQ, COL_BK, COL_BV, COL_BZ = 3072, 4096, 5120, 6144
N_PROJ = 7168
G_AI, G_AF, G_BB, G_BA, G_END = 0, 4, 8, 16, 24
GATE_ROWS = 32

MLSTM_CHUNK = 256
GDN_BLOCK = MLSTM_CHUNK
GDN_CHUNK = 64
GDN_HEADS_PER_STEP = 4
S5_BLOCK = 512
S5_GROUPS_PER_BLOCK = 8
S5_NSTATE = S5_GROUPS_PER_BLOCK * SSM_STATE
MOE_ROWS = 512
MOE_SUB = 256
MOE_KT = 2
MOE_KW = D_MODEL // MOE_KT
MOE_DMA_GROUP_LOG2 = 4
MOE_DMA_GROUP = 1 << MOE_DMA_GROUP_LOG2


def _cparams(sem):
    return pltpu.CompilerParams(dimension_semantics=sem, vmem_limit_bytes=VMEM_LIMIT)


def _softcap(t, cap):
    return cap * jnp.tanh(t / cap)


def _log_sigmoid(t):
    return jnp.minimum(t, 0.0) - jnp.log(1.0 + jnp.exp(-jnp.abs(t)))


def _softplus(t):
    return jnp.maximum(t, 0.0) + jnp.log(1.0 + jnp.exp(-jnp.abs(t)))


def _sigmoid(t):
    return 1.0 / (1.0 + jnp.exp(-t))


def _silu(t):
    return t * _sigmoid(t)


def _pick_col(x, idx):
    lane = lax.broadcasted_iota(jnp.int32, x.shape, 1)
    return jnp.sum(jnp.where(lane == idx, x, 0.0), axis=-1, keepdims=True)


def _dot(a, b):
    return jnp.dot(a, b, preferred_element_type=F32)


def _dot_hi(a, b):
    return jnp.dot(a, b, precision=HIGHEST, preferred_element_type=F32)


def _dot_nt(a, b):
    return lax.dot_general(a, b, (((1,), (1,)), ((), ())), preferred_element_type=F32)


def _inproj_kernel(x_ref, g_ref, w_ref, wg_ref, o_ref, og_ref, h_scr):
    @pl.when(pl.program_id(1) == 0)
    def _():
        x = x_ref[...]
        ms = jnp.mean(x * x, axis=-1, keepdims=True)
        h_scr[...] = (x * lax.rsqrt(ms + EPS) * g_ref[...]).astype(BF16)
        og_ref[...] = _dot(h_scr[...], wg_ref[...])

    o_ref[...] = _dot(h_scr[...], w_ref[...])


def _inproj(x2, g, w_bf, wg_bf):
    t, d = x2.shape
    n = w_bf.shape[1]
    tm = min(1024, t)
    tn = 512
    return pl.pallas_call(
        _inproj_kernel,
        grid=(t // tm, n // tn),
        in_specs=[pl.BlockSpec((tm, d), lambda i, j: (i, 0)),
                  pl.BlockSpec((1, d), lambda i, j: (0, 0)),
                  pl.BlockSpec((d, tn), lambda i, j: (0, j)),
                  pl.BlockSpec((d, LANES), lambda i, j: (0, 0))],
        out_specs=[pl.BlockSpec((tm, tn), lambda i, j: (i, j)),
                   pl.BlockSpec((tm, LANES), lambda i, j: (i, 0))],
        out_shape=[jax.ShapeDtypeStruct((t, n), F32), jax.ShapeDtypeStruct((t, LANES), F32)],
        scratch_shapes=[pltpu.VMEM((tm, d), BF16)],
        compiler_params=_cparams(("parallel", "arbitrary")),
        name="inproj",
    )(x2, g, w_bf, wg_bf)


def _gate_values(g, alog, idx, cum_a, cum_b):
    sc = _softcap(g, A_GATE_CAP)
    cum_logf = cum_a(_log_sigmoid(sc))
    cum_g = cum_b(-jnp.exp(alog) * _softplus(g))
    return jnp.where(idx < G_AF, sc,
                     jnp.where(idx < G_BB, cum_logf,
                               jnp.where(idx < G_BA, _sigmoid(g), jnp.where(idx < G_END, cum_g, 0.0))))


def _gates_kernel(gc_ref, gr_ref, bc_ref, br_ref, alc_ref, alr_ref, pc_ref, pr_ref):
    L = gc_ref.shape[0]
    row = lax.broadcasted_iota(jnp.int32, (L, L), 0)
    col = lax.broadcasted_iota(jnp.int32, (L, L), 1)
    same = (row // GDN_CHUNK) == (col // GDN_CHUNK)
    tril = (row >= col).astype(F32)
    triu = (row <= col).astype(F32)
    blk_tril = (same & (row >= col)).astype(F32)
    blk_triu = (same & (row <= col)).astype(F32)
    lane = lax.broadcasted_iota(jnp.int32, (L, LANES), 1)
    pc_ref[...] = _gate_values(gc_ref[...] + bc_ref[...], alc_ref[...], lane,
                               lambda v: _dot_hi(tril, v), lambda v: _dot_hi(blk_tril, v))
    sub = lax.broadcasted_iota(jnp.int32, (GATE_ROWS, L), 0)
    pr_ref[...] = _gate_values(gr_ref[...] + br_ref[...], alr_ref[...], sub,
                               lambda v: _dot_hi(v, triu), lambda v: _dot_hi(v, blk_triu))


def _gates(gates, gates_t, bias_col, bias_row, alog_col, alog_row):
    b, s, _ = gates.shape
    L = min(MLSTM_CHUNK, s)
    return pl.pallas_call(
        _gates_kernel,
        grid=(b, s // L),
        in_specs=[pl.BlockSpec((None, L, LANES), lambda i, c: (i, c, 0)),
                  pl.BlockSpec((None, GATE_ROWS, L), lambda i, c: (i, 0, c)),
                  pl.BlockSpec((1, LANES), lambda i, c: (0, 0)),
                  pl.BlockSpec((GATE_ROWS, 1), lambda i, c: (0, 0)),
                  pl.BlockSpec((1, LANES), lambda i, c: (0, 0)),
                  pl.BlockSpec((GATE_ROWS, 1), lambda i, c: (0, 0))],
        out_specs=[pl.BlockSpec((None, L, LANES), lambda i, c: (i, c, 0)),
                   pl.BlockSpec((None, GATE_ROWS, L), lambda i, c: (i, 0, c))],
        out_shape=[jax.ShapeDtypeStruct((b, s, LANES), F32),
                   jax.ShapeDtypeStruct((b, GATE_ROWS, s), F32)],
        compiler_params=_cparams(("parallel", "parallel")),
        name="gates",
    )(gates, gates_t, bias_col, bias_row, alog_col, alog_row)


def _mlstm_kernel(q_ref, k_ref, v_ref, o_ref, pc_ref, pr_ref, nw_ref, out_ref, c_scr, n_scr, m_scr):
    h = pl.program_id(1)
    c = pl.program_id(2)
    L = q_ref.shape[0]

    @pl.when(c == 0)
    def _():
        c_scr[...] = jnp.zeros_like(c_scr)
        n_scr[...] = jnp.zeros_like(n_scr)
        m_scr[...] = jnp.zeros_like(m_scr)

    row = lax.broadcasted_iota(jnp.int32, (L, L), 0)
    col = lax.broadcasted_iota(jnp.int32, (L, L), 1)
    causal = row >= col

    pc = pc_ref[...]
    i_col = _pick_col(pc, G_AI + h)
    b_col = _pick_col(pc, G_AF + h)
    i_row = pr_ref[pl.ds(G_AI + h, 1), :]
    b_row = pr_ref[pl.ds(G_AF + h, 1), :]
    b_last = b_col[L - 1:L, :]

    m_prev = m_scr[...]
    log_d = jnp.where(causal, b_col - b_row + i_row, -jnp.inf)
    log_inter = b_col + m_prev
    m_t = jnp.maximum(log_inter, jnp.max(log_d, axis=-1, keepdims=True))
    dmat = jnp.exp(log_d - m_t)
    inter = jnp.exp(log_inter - m_t)

    q = q_ref[...]
    k = k_ref[...] * (A_QK_DIM ** -0.5)
    qb = q.astype(BF16)
    kb = k.astype(BF16)
    vb = v_ref[...].astype(BF16)
    scores = _dot_nt(qb, kb) * dmat
    c_mat = c_scr[...]
    n_vec = n_scr[...]
    num = inter * _dot(qb, c_mat.astype(BF16)) + _dot(scores.astype(BF16), vb)
    den = inter * jnp.sum(q * n_vec, axis=-1, keepdims=True) + jnp.sum(scores, axis=-1, keepdims=True)
    hh = num / jnp.maximum(jnp.abs(den), jnp.exp(-m_t))
    hh = hh * lax.rsqrt(jnp.mean(hh * hh, axis=-1, keepdims=True) + EPS) * nw_ref[...]
    out_ref[...] = hh * _sigmoid(o_ref[...])

    le_col = b_last - b_col + i_col
    m_new = jnp.maximum(b_last + m_prev, jnp.max(le_col, axis=0, keepdims=True))
    carry_scale = jnp.exp(b_last + m_prev - m_new)
    kw = k * jnp.exp(le_col - m_new)
    c_scr[...] = c_mat * carry_scale + _dot(kw.T.astype(BF16), vb)
    n_scr[...] = n_vec * carry_scale + jnp.sum(kw, axis=0, keepdims=True)
    m_scr[...] = m_new


def _mlstm(proj, pc, pr, norm_w):
    b, s, _ = proj.shape
    L = min(MLSTM_CHUNK, s)
    qb, kb = COL_AQ // A_QK_DIM, COL_AK // A_QK_DIM
    vb, ob = COL_AV // A_V_DIM, COL_AO // A_V_DIM
    return pl.pallas_call(
        _mlstm_kernel,
        grid=(b, A_HEADS, s // L),
        in_specs=[pl.BlockSpec((None, L, A_QK_DIM), lambda i, h, c: (i, c, qb + h)),
                  pl.BlockSpec((None, L, A_QK_DIM), lambda i, h, c: (i, c, kb + h)),
                  pl.BlockSpec((None, L, A_V_DIM), lambda i, h, c: (i, c, vb + h)),
                  pl.BlockSpec((None, L, A_V_DIM), lambda i, h, c: (i, c, ob + h)),
                  pl.BlockSpec((None, L, LANES), lambda i, h, c: (i, c, 0)),
                  pl.BlockSpec((None, GATE_ROWS, L), lambda i, h, c: (i, 0, c)),
                  pl.BlockSpec((1, A_V_DIM), lambda i, h, c: (0, h))],
        out_specs=pl.BlockSpec((None, L, A_V_DIM), lambda i, h, c: (i, c, h)),
        out_shape=jax.ShapeDtypeStruct((b, s, A_HEADS * A_V_DIM), F32),
        scratch_shapes=[pltpu.VMEM((A_QK_DIM, A_V_DIM), F32),
                        pltpu.VMEM((1, A_QK_DIM), F32),
                        pltpu.VMEM((1, 1), F32)],
        compiler_params=_cparams(("parallel", "parallel", "arbitrary")),
        name="mlstm",
    )(proj, proj, proj, proj, pc, pr, norm_w)


def _causal_conv_silu(x, tail, w):
    row8 = lax.broadcasted_iota(jnp.int32, (SUBLANES, x.shape[1]), 0)
    acc = x * w[B_CONV - 1:B_CONV, :]
    for d in range(1, B_CONV):
        rolled = pltpu.roll(x, d, 0)
        head = jnp.where(row8 < d, pltpu.roll(tail, d, 0), rolled[0:SUBLANES, :])
        shifted = jnp.concatenate([head, rolled[SUBLANES:, :]], axis=0)
        acc = acc + shifted * w[B_CONV - 1 - d:B_CONV - d, :]
    return _silu(acc)


def _l2norm(t):
    return t * lax.rsqrt(jnp.sum(t * t, axis=-1, keepdims=True) + EPS)


def _gdn_kernel(q_ref, k_ref, v_ref, z_ref, pc_ref, pr_ref, wq_ref, wk_ref, wv_ref, nw_ref, out_ref,
                s_scr, tq_scr, tk_scr, tv_scr):
    hp = pl.program_id(1)
    c = pl.program_id(2)
    Lb = q_ref.shape[0]
    hd = B_HEAD_DIM

    @pl.when(c == 0)
    def _():
        s_scr[...] = jnp.zeros_like(s_scr)
        tq_scr[...] = jnp.zeros_like(tq_scr)
        tk_scr[...] = jnp.zeros_like(tk_scr)
        tv_scr[...] = jnp.zeros_like(tv_scr)

    xq, xk, xv = q_ref[...], k_ref[...], v_ref[...]
    cq = _causal_conv_silu(xq, tq_scr[...], wq_ref[...])
    ck = _causal_conv_silu(xk, tk_scr[...], wk_ref[...])
    cv = _causal_conv_silu(xv, tv_scr[...], wv_ref[...])
    tq_scr[...] = xq[Lb - SUBLANES:, :]
    tk_scr[...] = xk[Lb - SUBLANES:, :]
    tv_scr[...] = xv[Lb - SUBLANES:, :]
    pc = pc_ref[...]
    nw = nw_ref[...]
    heads = [hp * GDN_HEADS_PER_STEP + hh for hh in range(GDN_HEADS_PER_STEP)]
    per_head = lambda x: jnp.stack([x[:, hh * hd:(hh + 1) * hd] for hh in range(GDN_HEADS_PER_STEP)], axis=0)
    hb, s_scr[...] = _gdn_heads(
        _l2norm(per_head(cq)) * (hd ** -0.5), _l2norm(per_head(ck)), per_head(cv),
        jnp.stack([_pick_col(pc, G_BB + h) for h in heads], axis=0),
        jnp.stack([_pick_col(pc, G_BA + h) for h in heads], axis=0),
        jnp.stack([pr_ref[pl.ds(G_BA + h, 1), :] for h in heads], axis=0), s_scr[...])
    hb = hb * lax.rsqrt(jnp.mean(hb * hb, axis=-1, keepdims=True) + EPS) * nw
    for hh in range(GDN_HEADS_PER_STEP):
        cols = slice(hh * hd, (hh + 1) * hd)
        out_ref[:, cols] = hb[hh] * _silu(z_ref[:, cols])


def _gdn_heads(q, k, v, beta, dec_col, dec_row, state):
    nh, Lb, _ = q.shape
    C = GDN_CHUNK
    nsub = Lb // C
    bmm = lambda a, b: jnp.einsum("bij,bjk->bik", a.astype(BF16), b.astype(BF16), preferred_element_type=F32)
    bmm_nt = lambda a, b: jnp.einsum("bik,bjk->bij", a.astype(BF16), b.astype(BF16), preferred_element_type=F32)
    r64 = lax.broadcasted_iota(jnp.int32, (C, C), 0)
    c64 = lax.broadcasted_iota(jnp.int32, (C, C), 1)
    causal = r64 >= c64
    strict = r64 > c64
    eye = (r64 == c64).astype(F32)

    kbeta = k * beta
    edec = jnp.exp(dec_col)
    q_dec = (q * edec).astype(BF16)
    vbeta = (v * beta).astype(BF16)
    kbdec = (kbeta * edec).astype(BF16)
    kb16 = k.astype(BF16)
    qb16 = q.astype(BF16)
    kbeta16 = kbeta.astype(BF16)

    segs, nmats = [], []
    for j in range(nsub):
        sl = slice(j * C, (j + 1) * C)
        seg = jnp.exp(jnp.where(causal, dec_col[:, sl, :] - dec_row[:, :, sl], -jnp.inf))
        a_low = jnp.where(strict, bmm_nt(kbeta16[:, sl, :], kb16[:, sl, :]) * seg, 0.0)
        segs.append(seg)
        nmats.append(-a_low)
    nmat = jnp.concatenate(nmats, axis=0)
    tmat = eye[None] + nmat
    npow = bmm(nmat, nmat)
    for _ in range(4):
        tmat, npow = tmat + bmm(tmat, npow), bmm(npow, npow)
    tmat = tmat + bmm(tmat, npow)

    outs = []
    for j in range(nsub):
        sl = slice(j * C, (j + 1) * C)
        t16 = tmat[j * nh:(j + 1) * nh].astype(BF16)
        u = bmm(t16, vbeta[:, sl, :])
        w = bmm(t16, kbdec[:, sl, :])
        attn = bmm_nt(qb16[:, sl, :], kb16[:, sl, :]) * segs[j]
        s16 = state.astype(BF16)
        v_new = u - bmm(w, s16)
        o = bmm(q_dec[:, sl, :], s16) + bmm(attn, v_new)
        d_last = dec_col[:, (j + 1) * C - 1:(j + 1) * C, :]
        k_end = k[:, sl, :] * jnp.exp(d_last - dec_col[:, sl, :])
        state = state * jnp.exp(d_last) + bmm(jnp.swapaxes(k_end, 1, 2), v_new)
        outs.append(o)
    return jnp.concatenate(outs, axis=1), state


def _gdn(proj, pc, pr, conv_w, norm_w):
    b, s, _ = proj.shape
    Lb = min(GDN_BLOCK, s)
    hd = B_HEAD_DIM
    wd = GDN_HEADS_PER_STEP * hd
    npair = B_HEADS // GDN_HEADS_PER_STEP
    qb, kb, vb, zb = COL_BQ // wd, COL_BK // wd, COL_BV // wd, COL_BZ // wd
    blk = lambda off: pl.BlockSpec((None, Lb, wd), lambda i, h, c: (i, c, off + h))
    return pl.pallas_call(
        _gdn_kernel,
        grid=(b, npair, s // Lb),
        in_specs=[blk(qb), blk(kb), blk(vb), blk(zb),
                  pl.BlockSpec((None, Lb, LANES), lambda i, h, c: (i, c, 0)),
                  pl.BlockSpec((None, GATE_ROWS, Lb), lambda i, h, c: (i, 0, c)),
                  pl.BlockSpec((B_CONV, wd), lambda i, h, c: (0, h)),
                  pl.BlockSpec((B_CONV, wd), lambda i, h, c: (0, npair + h)),
                  pl.BlockSpec((B_CONV, wd), lambda i, h, c: (0, 2 * npair + h)),
                  pl.BlockSpec((1, hd), lambda i, h, c: (0, 0))],
        out_specs=pl.BlockSpec((None, Lb, wd), lambda i, h, c: (i, c, h)),
        out_shape=jax.ShapeDtypeStruct((b, s, B_HEADS * hd), F32),
        scratch_shapes=[pltpu.VMEM((GDN_HEADS_PER_STEP, hd, hd), F32),
                        pltpu.VMEM((SUBLANES, wd), F32),
                        pltpu.VMEM((SUBLANES, wd), F32),
                        pltpu.VMEM((SUBLANES, wd), F32)],
        compiler_params=_cparams(("parallel", "parallel", "arbitrary")),
        name="gdn",
    )(proj, proj, proj, proj, pc, pr, conv_w, conv_w, conv_w, norm_w)


def _outproj_kernel(ya_ref, yb_ref, x_ref, wa_ref, wb_ref, o_ref):
    acc = _dot(ya_ref[...].astype(BF16), wa_ref[...]) + _dot(yb_ref[...].astype(BF16), wb_ref[...])
    o_ref[...] = x_ref[...] + acc


def _outproj(ya, yb, x2, wa, wb):
    t, d = x2.shape
    ka, kb = ya.shape[1], yb.shape[1]
    tm = min(512, t)
    return pl.pallas_call(
        _outproj_kernel,
        grid=(t // tm,),
        in_specs=[pl.BlockSpec((tm, ka), lambda i: (i, 0)),
                  pl.BlockSpec((tm, kb), lambda i: (i, 0)),
                  pl.BlockSpec((tm, d), lambda i: (i, 0)),
                  pl.BlockSpec((ka, d), lambda i: (0, 0)),
                  pl.BlockSpec((kb, d), lambda i: (0, 0))],
        out_specs=pl.BlockSpec((tm, d), lambda i: (i, 0)),
        out_shape=jax.ShapeDtypeStruct((t, d), F32),
        compiler_params=_cparams(("parallel",)),
        name="outproj",
    )(ya, yb, x2, wa, wb)


def _router_kernel(x_ref, g_ref, whi_ref, wlo_ref, b_ref, h_ref, ids_ref, gate_ref):
    x = x_ref[...]
    ms = jnp.mean(x * x, axis=-1, keepdims=True)
    h = x * lax.rsqrt(ms + EPS) * g_ref[...]
    h_ref[...] = h
    h_hi = h.astype(BF16)
    h_lo = (h - h_hi.astype(F32)).astype(BF16)
    w_hi = whi_ref[...]
    logits = _dot(h_hi, w_hi) + _dot(h_lo, w_hi) + _dot(h_hi, wlo_ref[...]) + b_ref[...]
    lane = lax.broadcasted_iota(jnp.int32, logits.shape, 1)
    neg = -jnp.inf
    big = jnp.int32(1 << 20)
    is_g = (lane >= MOE_EXPERTS) & (lane < MOE_EXPERTS + MOE_GROUPS)
    gl = jnp.where(is_g, logits, neg)
    gmax = jnp.max(gl, axis=-1, keepdims=True)
    g_lane = jnp.min(jnp.where(gl == gmax, lane, big), axis=-1, keepdims=True)
    g_idx = g_lane - MOE_EXPERTS
    g_w = 1.0 / jnp.sum(jnp.exp(gl - gmax), axis=-1, keepdims=True)
    in_grp = (lane >= g_idx * MOE_PER_GROUP) & (lane < (g_idx + 1) * MOE_PER_GROUP)
    el = jnp.where(in_grp, logits, neg)
    e0 = jnp.max(el, axis=-1, keepdims=True)
    l0 = jnp.min(jnp.where(el == e0, lane, big), axis=-1, keepdims=True)
    el1 = jnp.where(lane == l0, neg, el)
    e1 = jnp.max(el1, axis=-1, keepdims=True)
    l1 = jnp.min(jnp.where(el1 == e1, lane, big), axis=-1, keepdims=True)
    r = jnp.exp(e1 - e0)
    p0 = 1.0 / (1.0 + r)
    p1 = r / (1.0 + r)
    ids_ref[...] = jnp.where(lane == 0, l0, jnp.where(lane == 1, l1, 0))
    gate_ref[...] = jnp.where(lane == 0, g_w * p0, jnp.where(lane == 1, g_w * p1, 0.0))


def _router(x2, g, w, bias):
    t, d = x2.shape
    tm = min(256, t)
    w_hi = w.astype(BF16)
    w_lo = (w - w_hi.astype(F32)).astype(BF16)
    return pl.pallas_call(
        _router_kernel,
        grid=(t // tm,),
        in_specs=[pl.BlockSpec((tm, d), lambda i: (i, 0)),
                  pl.BlockSpec((1, d), lambda i: (0, 0)),
                  pl.BlockSpec((d, LANES), lambda i: (0, 0)),
                  pl.BlockSpec((d, LANES), lambda i: (0, 0)),
                  pl.BlockSpec((1, LANES), lambda i: (0, 0))],
        out_specs=[pl.BlockSpec((tm, d), lambda i: (i, 0)),
                   pl.BlockSpec((tm, LANES), lambda i: (i, 0)),
                   pl.BlockSpec((tm, LANES), lambda i: (i, 0))],
        out_shape=[jax.ShapeDtypeStruct((t, d), F32),
                   jax.ShapeDtypeStruct((t, LANES), jnp.int32),
                   jax.ShapeDtypeStruct((t, LANES), F32)],
        compiler_params=_cparams(("parallel",)),
        name="router",
    )(x2, g, w_hi, w_lo, bias)


def _moe_plan(expert, n_blocks):
    t = expert.shape[0]
    n_assign = 2 * t
    flat_e = expert.reshape(n_assign)
    onehot = (flat_e[:, None] == jnp.arange(MOE_EXPERTS, dtype=jnp.int32)[None, :]).astype(jnp.int32)
    csum = jnp.cumsum(onehot, axis=0)
    rank = jnp.sum(onehot * csum, axis=1) - 1
    counts = csum[-1]
    nblk = (counts + MOE_ROWS - 1) // MOE_ROWS
    blk_end = jnp.cumsum(nblk)
    blk_start = blk_end - nblk
    dest = jnp.sum(onehot * blk_start[None, :], axis=1) * MOE_ROWS + rank
    spare = 2 * (t + jnp.arange(n_blocks * MOE_ROWS, dtype=jnp.int32) % MOE_DMA_GROUP)
    row_dst = spare.at[dest].set(jnp.arange(n_assign, dtype=jnp.int32))
    bid = jnp.arange(n_blocks, dtype=jnp.int32)
    total = blk_end[-1]
    be = jnp.minimum(jnp.sum((bid[:, None] >= blk_end[None, :]).astype(jnp.int32), axis=1), MOE_EXPERTS - 1)
    used = bid < total
    blk_n = jnp.where(used, jnp.clip(counts[be] - (bid - blk_start[be]) * MOE_ROWS, 0, MOE_ROWS), 0)
    blk_e = jnp.where(used, be, be[jnp.maximum(total - 1, 0)])
    return row_dst, blk_e.astype(jnp.int32), blk_n.astype(jnp.int32)


def _moe_kernel(blk_e_ref, blk_n_ref, dst_ref, h_hbm, wg_ref, wu_ref, wd_ref, yk_hbm,
                xf_scr, xb_scr, g_scr, u_scr, y_scr, z_scr, gsem, ssem):
    b = pl.program_id(0)
    k = pl.program_id(1)
    nb = pl.num_programs(0)
    n = blk_n_ref[b]
    R = xf_scr.shape[0]

    t = h_hbm.shape[0]

    def gather_copy(blk, r):
        tok = jnp.minimum(lax.shift_right_logical(dst_ref[blk * R + r], 1), t - 1)
        return pltpu.make_async_copy(h_hbm.at[pl.ds(tok, 1), :], xf_scr.at[pl.ds(r, 1), :], gsem)

    def scatter_copy(blk, r):
        dst = dst_ref[blk * R + r]
        tok = lax.shift_right_logical(dst, 1)
        return pltpu.make_async_copy(y_scr.at[pl.ds(r, 1), :], yk_hbm.at[dst & 1, pl.ds(tok, 1), :], ssem)

    def for_rows(blk, fn):
        def body(g, carry):
            for j in range(MOE_DMA_GROUP):
                fn(blk, g * MOE_DMA_GROUP + j)
            return carry
        lax.fori_loop(0, lax.shift_right_logical(blk_n_ref[blk] + (MOE_DMA_GROUP - 1), MOE_DMA_GROUP_LOG2), body, 0)

    @pl.when((b == 0) & (k == 0))
    def _():
        xf_scr[...] = jnp.zeros_like(xf_scr)
        for_rows(0, lambda blk, r: gather_copy(blk, r).start())
        z_scr[...] = jnp.zeros_like(z_scr)
        spare = [pltpu.make_async_copy(z_scr, yk_hbm.at[s, pl.ds(t, MOE_DMA_GROUP), :], ssem) for s in range(2)]
        for cp in spare:
            cp.start()
        for cp in spare:
            cp.wait()

    @pl.when(k == 0)
    def _():
        for_rows(b, lambda blk, r: gather_copy(blk, r).wait())
        for kk in range(MOE_KT):
            xb_scr[kk] = xf_scr[:, kk * MOE_KW:(kk + 1) * MOE_KW].astype(BF16)

    @pl.when((k == 1) & (b + 1 < nb))
    def _():
        for_rows(b + 1, lambda blk, r: gather_copy(blk, r).start())

    def for_sub_blocks(fn):
        for sb in range(R // MOE_SUB):
            @pl.when(sb * MOE_SUB < n)
            def _():
                fn(slice(sb * MOE_SUB, (sb + 1) * MOE_SUB))

    @pl.when(n > 0)
    def _():
        wg = wg_ref[...].astype(BF16)
        wu = wu_ref[...].astype(BF16)

        def accumulate(rows):
            x = xb_scr[k, rows, :]
            pg = _dot(x, wg)
            pu = _dot(x, wu)

            @pl.when(k == 0)
            def _():
                g_scr[rows, :] = pg
                u_scr[rows, :] = pu

            @pl.when(k != 0)
            def _():
                g_scr[rows, :] += pg
                u_scr[rows, :] += pu

        for_sub_blocks(accumulate)

    @pl.when(k == MOE_KT - 1)
    def _():
        @pl.when(b > 0)
        def _():
            for_rows(b - 1, lambda blk, r: scatter_copy(blk, r).wait())

        @pl.when(n > 0)
        def _():
            wd = wd_ref[...].astype(BF16)

            def down(rows):
                hmid = _silu(g_scr[rows, :]) * u_scr[rows, :]
                y_scr[rows, :] = _dot(hmid.astype(BF16), wd)

            for_sub_blocks(down)
            for_rows(b, lambda blk, r: scatter_copy(blk, r).start())

        @pl.when(b == nb - 1)
        def _():
            for_rows(b, lambda blk, r: scatter_copy(blk, r).wait())


def _moe_experts(h2, row_dst, blk_e, blk_n, w_gate, w_up, w_down, layer, n_blocks):
    t, d = h2.shape
    R = MOE_ROWS
    last = MOE_KT - 1

    def w_in_map(b, k, be, bn, dst):
        return (layer, be[b], jnp.where(bn[b] > 0, k, last), 0)

    def w_out_map(b, k, be, bn, dst):
        return (layer, be[b], 0, 0)

    grid_spec = pltpu.PrefetchScalarGridSpec(
        num_scalar_prefetch=3,
        grid=(n_blocks, MOE_KT),
        in_specs=[pl.BlockSpec(memory_space=pl.ANY),
                  pl.BlockSpec((None, None, MOE_KW, MOE_FF), w_in_map),
                  pl.BlockSpec((None, None, MOE_KW, MOE_FF), w_in_map),
                  pl.BlockSpec((None, None, MOE_FF, d), w_out_map)],
        out_specs=pl.BlockSpec(memory_space=pl.ANY),
        scratch_shapes=[pltpu.VMEM((R, d), F32), pltpu.VMEM((MOE_KT, R, MOE_KW), BF16),
                        pltpu.VMEM((R, MOE_FF), F32), pltpu.VMEM((R, MOE_FF), F32), pltpu.VMEM((R, d), F32),
                        pltpu.VMEM((MOE_DMA_GROUP, d), F32),
                        pltpu.SemaphoreType.DMA(()), pltpu.SemaphoreType.DMA(())],
    )
    return pl.pallas_call(
        _moe_kernel,
        grid_spec=grid_spec,
        out_shape=jax.ShapeDtypeStruct((2, t + MOE_DMA_GROUP, d), F32),
        compiler_params=_cparams(("arbitrary", "arbitrary")),
        name="moe_experts",
    )(blk_e, blk_n, row_dst, h2, w_gate, w_up, w_down)


def _combine_kernel(y0_ref, y1_ref, x_ref, gate_ref, g_ref, *out_refs):
    gate = gate_ref[...]
    x = x_ref[...] + gate[:, 0:1] * y0_ref[...] + gate[:, 1:2] * y1_ref[...]
    if len(out_refs) == 2:
        out_refs[0][...] = x
    ms = jnp.mean(x * x, axis=-1, keepdims=True)
    out_refs[-1][...] = x * lax.rsqrt(ms + EPS) * g_ref[...]


def _moe_combine(yk, x2, gate, next_norm, want_stream):
    t, d = x2.shape
    tm = min(256, t)
    n_out = 2 if want_stream else 1
    return pl.pallas_call(
        _combine_kernel,
        grid=(t // tm,),
        in_specs=[pl.BlockSpec((None, tm, d), lambda i: (0, i, 0)),
                  pl.BlockSpec((None, tm, d), lambda i: (1, i, 0)),
                  pl.BlockSpec((tm, d), lambda i: (i, 0)),
                  pl.BlockSpec((tm, LANES), lambda i: (i, 0)),
                  pl.BlockSpec((1, d), lambda i: (0, 0))],
        out_specs=[pl.BlockSpec((tm, d), lambda i: (i, 0))] * n_out,
        out_shape=[jax.ShapeDtypeStruct((t, d), F32)] * n_out,
        compiler_params=_cparams(("parallel",)),
        name="moe_combine",
    )(yk, yk, x2, gate, next_norm)


def _hier_moe(x2, ffn_norm, w_rg, b_rg, w_re, b_re, w_gate, w_up, w_down, layer, next_norm, want_stream=True):
    t, d = x2.shape
    pad = LANES - MOE_EXPERTS - MOE_GROUPS
    w_r = jnp.concatenate([w_re, w_rg, jnp.zeros((d, pad), F32)], axis=1)
    b_r = jnp.concatenate([b_re, b_rg, jnp.zeros((pad,), F32)]).reshape(1, LANES)
    h2, ids, gate = _router(x2, ffn_norm.reshape(1, d), w_r, b_r)
    n_blocks = (2 * t) // MOE_ROWS + MOE_EXPERTS
    row_dst, blk_e, blk_n = _moe_plan(ids[:, :2], n_blocks)
    yk = _moe_experts(h2, row_dst, blk_e, blk_n, w_gate, w_up, w_down, layer, n_blocks)
    return _moe_combine(yk, x2, gate, next_norm.reshape(1, d), want_stream)


def _s5_disc_kernel(lre_ref, lim_ref, ls_ref, bre_ref, bim_ref, are_ref, aim_ref, bbre_ref, bbim_ref):
    lr = jnp.minimum(lre_ref[...], -1e-4)
    li = lim_ref[...]
    step = jnp.exp(ls_ref[...])
    mag = jnp.exp(lr * step)
    ang = li * step
    ab_re = mag * jnp.cos(ang)
    ab_im = mag * jnp.sin(ang)
    den = lr * lr + li * li
    zr = ab_re - 1.0
    f_re = (zr * lr + ab_im * li) / den
    f_im = (ab_im * lr - zr * li) / den
    br = bre_ref[...]
    bi = bim_ref[...]
    are_ref[...] = ab_re
    aim_ref[...] = ab_im
    bbre_ref[...] = f_re * br - f_im * bi
    bbim_ref[...] = f_re * bi + f_im * br


def _s5_discretise(lam_re, lam_im, log_step, b_re, b_im):
    g, p = lam_re.shape
    n = g * p
    col = lambda a: a.reshape(n, 1)
    ls = jnp.broadcast_to(log_step[:, None], (g, p))
    rows = 1024
    full = lambda w: pl.BlockSpec((rows, w), lambda i: (i, 0))
    return pl.pallas_call(
        _s5_disc_kernel,
        grid=(n // rows,),
        in_specs=[full(1), full(1), full(1), full(SSM_GROUP), full(SSM_GROUP)],
        out_specs=[full(1), full(1), full(SSM_GROUP), full(SSM_GROUP)],
        out_shape=[jax.ShapeDtypeStruct((n, 1), F32), jax.ShapeDtypeStruct((n, 1), F32),
                   jax.ShapeDtypeStruct((n, SSM_GROUP), F32), jax.ShapeDtypeStruct((n, SSM_GROUP), F32)],
        compiler_params=_cparams(("parallel",)),
        name="s5_discretise",
    )(col(lam_re), col(lam_im), col(ls), b_re.reshape(n, SSM_GROUP), b_im.reshape(n, SSM_GROUP))


def _cmul(ar, ai, br, bi):
    return ar * br - ai * bi, ar * bi + ai * br


def _s5_kernel(u_ref, bw_ref, cw_ref, a_ref, d_ref, y_ref, bu_scr, x_scr, xb_scr, pow_scr, carry_scr):
    tau = pl.program_id(2)
    Lb = u_ref.shape[0]
    seg = Lb // SUBLANES
    ns = S5_NSTATE
    ar = a_ref[0:1, :]
    ai = a_ref[1:2, :]

    @pl.when(tau == 0)
    def _():
        carry_scr[...] = jnp.zeros_like(carry_scr)
        pr, pi = ar, ai
        for i in range(seg):
            pow_scr[i:i + 1, 0:ns] = pr
            pow_scr[i:i + 1, ns:2 * ns] = pi
            pr, pi = _cmul(pr, pi, ar, ai)

    u_perm = jnp.concatenate([u_ref[pl.ds(i, SUBLANES, stride=seg), :] for i in range(seg)], axis=0)
    bu_scr[...] = _dot(u_perm.astype(BF16), bw_ref[...])

    ar8 = jnp.broadcast_to(ar, (SUBLANES, ns))
    ai8 = jnp.broadcast_to(ai, (SUBLANES, ns))

    def scan_body(i, carry):
        xr, xi = carry
        r0 = pl.multiple_of(i * SUBLANES, SUBLANES)
        nr = ar8 * xr - ai8 * xi + bu_scr[pl.ds(r0, SUBLANES), 0:ns]
        ni = ar8 * xi + ai8 * xr + bu_scr[pl.ds(r0, SUBLANES), ns:2 * ns]
        x_scr[pl.ds(r0, SUBLANES), 0:ns] = nr
        x_scr[pl.ds(r0, SUBLANES), ns:2 * ns] = ni
        return nr, ni

    zeros = jnp.zeros((SUBLANES, ns), F32)
    er, ei = lax.fori_loop(0, seg, scan_body, (zeros, zeros), unroll=8)

    alr = pow_scr[seg - 1:seg, 0:ns]
    ali = pow_scr[seg - 1:seg, ns:2 * ns]
    pr = carry_scr[0:1, 0:ns]
    pi = carry_scr[0:1, ns:2 * ns]
    prs, pis = [], []
    for j in range(SUBLANES):
        prs.append(pr)
        pis.append(pi)
        mr, mi = _cmul(alr, ali, pr, pi)
        pr = er[j:j + 1, :] + mr
        pi = ei[j:j + 1, :] + mi
    carry_scr[0:1, 0:ns] = pr
    carry_scr[0:1, ns:2 * ns] = pi
    p_re = jnp.concatenate(prs + prs, axis=0)
    p_im = jnp.concatenate(pis + pis, axis=0)
    pack = 2 * SUBLANES

    def fix_body(i2, carry):
        r0 = pl.multiple_of(i2 * pack, pack)
        bc = lambda i, lo: jnp.broadcast_to(pow_scr[pl.ds(i, 1), lo:lo + ns], (SUBLANES, ns))
        wr = jnp.concatenate([bc(2 * i2, 0), bc(2 * i2 + 1, 0)], axis=0)
        wi = jnp.concatenate([bc(2 * i2, ns), bc(2 * i2 + 1, ns)], axis=0)
        mr, mi = _cmul(wr, wi, p_re, p_im)
        xb_scr[pl.ds(r0, pack), 0:ns] = (x_scr[pl.ds(r0, pack), 0:ns] + mr).astype(BF16)
        xb_scr[pl.ds(r0, pack), ns:2 * ns] = (x_scr[pl.ds(r0, pack), ns:2 * ns] + mi).astype(BF16)
        return carry

    lax.fori_loop(0, seg // 2, fix_body, 0, unroll=4)

    y = _dot(xb_scr[...], cw_ref[...]) + d_ref[...] * u_perm
    for i in range(seg):
        y_ref[pl.ds(i, SUBLANES, stride=seg), :] = y[i * SUBLANES:(i + 1) * SUBLANES, :]


def _s5_scan(h3, bw, cw, a_rows, d_skip):
    b, s, w = h3.shape
    Lb = min(S5_BLOCK, s)
    nb = w // LANES
    ns2 = 2 * S5_NSTATE
    return pl.pallas_call(
        _s5_kernel,
        grid=(b, nb, s // Lb),
        in_specs=[pl.BlockSpec((None, Lb, LANES), lambda i, k, c: (i, c, k)),
                  pl.BlockSpec((None, LANES, ns2), lambda i, k, c: (k, 0, 0)),
                  pl.BlockSpec((None, ns2, LANES), lambda i, k, c: (k, 0, 0)),
                  pl.BlockSpec((None, 2, S5_NSTATE), lambda i, k, c: (k, 0, 0)),
                  pl.BlockSpec((1, LANES), lambda i, k, c: (0, k))],
        out_specs=pl.BlockSpec((None, Lb, LANES), lambda i, k, c: (i, c, k)),
        out_shape=jax.ShapeDtypeStruct((b, s, w), F32),
        scratch_shapes=[pltpu.VMEM((Lb, ns2), F32), pltpu.VMEM((Lb, ns2), F32), pltpu.VMEM((Lb, ns2), BF16),
                        pltpu.VMEM((Lb // SUBLANES, ns2), F32), pltpu.VMEM((SUBLANES, ns2), F32)],
        compiler_params=_cparams(("parallel", "parallel", "arbitrary")),
        name="s5_scan",
    )(h3, bw, cw, a_rows, d_skip)


def _glu_kernel(y_ref, x_ref, wv_ref, wg_ref, bv_ref, bg_ref, o_ref, a_scr):
    @pl.when(pl.program_id(1) == 0)
    def _():
        a_scr[...] = jax.nn.gelu(y_ref[...]).astype(BF16)

    a = a_scr[...]
    val = _dot(a, wv_ref[...]) + bv_ref[...]
    gate = _dot(a, wg_ref[...]) + bg_ref[...]
    o_ref[...] = x_ref[...] + val * _sigmoid(gate)


def _glu(y2, x2, w_bf, bias):
    t, d = x2.shape
    tm = min(512, t)
    tn = 512
    nj = d // tn
    return pl.pallas_call(
        _glu_kernel,
        grid=(t // tm, nj),
        in_specs=[pl.BlockSpec((tm, d), lambda i, j: (i, 0)),
                  pl.BlockSpec((tm, tn), lambda i, j: (i, j)),
                  pl.BlockSpec((d, tn), lambda i, j: (0, j)),
                  pl.BlockSpec((d, tn), lambda i, j: (0, nj + j)),
                  pl.BlockSpec((1, tn), lambda i, j: (0, j)),
                  pl.BlockSpec((1, tn), lambda i, j: (0, nj + j))],
        out_specs=pl.BlockSpec((tm, tn), lambda i, j: (i, j)),
        out_shape=jax.ShapeDtypeStruct((t, d), F32),
        scratch_shapes=[pltpu.VMEM((tm, d), BF16)],
        compiler_params=_cparams(("parallel", "arbitrary")),
        name="glu",
    )(y2, x2, w_bf, w_bf, bias, bias)


def _block_diag(w):
    nb, gb, r, c = w.shape
    eye = jnp.eye(gb, dtype=w.dtype)
    return (w[:, :, :, None, :] * eye[None, :, None, :, None]).reshape(nb, gb * r, gb * c)


def _mixer_s5(h3, x2, lam_re, lam_im, log_step, b_re, b_im, c_re, c_im, d_skip, w_glu, b_glu):
    b, s, w = h3.shape
    g, p = lam_re.shape
    gb = S5_GROUPS_PER_BLOCK
    nb = g // gb
    a_re, a_im, bb_re, bb_im = _s5_discretise(lam_re, lam_im, log_step, b_re, b_im)
    a_rows = jnp.stack([a_re.reshape(nb, gb * p), a_im.reshape(nb, gb * p)], axis=1)
    bt = lambda m: jnp.swapaxes(m.reshape(nb, gb, p, SSM_GROUP), 2, 3)
    bw = jnp.concatenate([_block_diag(bt(bb_re)), _block_diag(bt(bb_im))], axis=2).astype(BF16)
    ct = lambda m: jnp.swapaxes(m.reshape(nb, gb, SSM_GROUP, p), 2, 3)
    cw = jnp.concatenate([_block_diag(ct(c_re)), -_block_diag(ct(c_im))], axis=1).astype(BF16)
    y = _s5_scan(h3, bw, cw, a_rows, d_skip.reshape(1, w))
    return _glu(y.reshape(b * s, w), x2, w_glu.astype(BF16), b_glu.reshape(1, -1))


def _mixer_ab(x2, bsz, norm_w, w_in, a_i_bias, a_f_bias, a_norm, b_conv, b_a_log, b_dt_bias, b_norm, w_out):
    t, d = x2.shape
    s = t // bsz
    zeros = lambda n: jnp.zeros((n,), F32)
    n_a = 2 * A_HEADS * A_QK_DIM + 2 * A_HEADS * A_V_DIM
    n_b = 4 * B_HEADS * B_HEAD_DIM
    a_end = n_a + 2 * A_HEADS
    w_gates = jnp.concatenate([w_in[:, n_a:a_end], w_in[:, a_end + n_b:],
                               jnp.zeros((d, LANES - G_END), F32)], axis=1).astype(BF16)
    w_wide = jnp.concatenate([w_in[:, :n_a], w_in[:, a_end:a_end + n_b]], axis=1).astype(BF16)
    proj, gates = _inproj(x2, norm_w.reshape(1, d), w_wide, w_gates)
    proj = proj.reshape(bsz, s, N_PROJ)
    gates = gates.reshape(bsz, s, LANES)
    gates_t = jnp.swapaxes(gates[:, :, :GATE_ROWS], 1, 2)
    bias = jnp.concatenate([a_i_bias, a_f_bias, zeros(B_HEADS), b_dt_bias, zeros(LANES - G_END)])
    alog = jnp.concatenate([zeros(G_BA), b_a_log, zeros(LANES - G_END)])
    pc, pr = _gates(gates, gates_t, bias.reshape(1, LANES), bias[:GATE_ROWS].reshape(GATE_ROWS, 1),
                    alog.reshape(1, LANES), alog[:GATE_ROWS].reshape(GATE_ROWS, 1))
    ya = _mlstm(proj, pc, pr, a_norm.reshape(1, -1))
    yb = _gdn(proj, pc, pr, b_conv, b_norm.reshape(1, -1))
    n_ya = A_HEADS * A_V_DIM
    w_out_bf = w_out.astype(BF16)
    return _outproj(ya.reshape(t, n_ya), yb.reshape(t, -1), x2, w_out_bf[:n_ya], w_out_bf[n_ya:])


def kernel(x, mix_norm, ab_w_in, mlstm_i_bias, mlstm_f_bias, mlstm_norm, gdn_conv, gdn_a_log, gdn_dt_bias, gdn_norm, ab_w_out, ssm_lambda_re, ssm_lambda_im, ssm_log_step, ssm_b_re, ssm_b_im, ssm_c_re, ssm_c_im, ssm_d, glu_w, glu_b, ffn_norm, router_group_w, router_group_b, router_expert_w, router_expert_b, expert_w_gate, expert_w_up, expert_w_down, final_norm):
    bsz, s, d = x.shape
    t = bsz * s
    x2 = x.reshape(t, d)
    x2 = _mixer_ab(x2, bsz, mix_norm[0], ab_w_in[0], mlstm_i_bias[0], mlstm_f_bias[0], mlstm_norm[0],
                   gdn_conv[0], gdn_a_log[0], gdn_dt_bias[0], gdn_norm[0], ab_w_out[0])
    x2, h3 = _hier_moe(x2, ffn_norm[0], router_group_w[0], router_group_b[0], router_expert_w[0],
                       router_expert_b[0], expert_w_gate, expert_w_up, expert_w_down, 0, mix_norm[1])
    x2 = _mixer_s5(h3.reshape(bsz, s, d), x2, ssm_lambda_re[0], ssm_lambda_im[0], ssm_log_step[0],
                   ssm_b_re[0], ssm_b_im[0], ssm_c_re[0], ssm_c_im[0], ssm_d[0], glu_w[0], glu_b[0])
    out, = _hier_moe(x2, ffn_norm[1], router_group_w[1], router_group_b[1], router_expert_w[1],
                     router_expert_b[1], expert_w_gate, expert_w_up, expert_w_down, 1, final_norm, want_stream=False)
    return out.reshape(bsz, s, d)
```

```python
import functools

import jax
import jax.numpy as jnp
from jax import lax
from jax.experimental import pallas as pl
from jax.experimental.pallas import tpu as pltpu

F32 = jnp.float32
BF16 = jnp.bfloat16
HIGHEST = lax.Precision.HIGHEST

EPS = 1e-6
D_MODEL = 2048
A_HEADS = 4
A_QK_DIM = 128
A_V_DIM = 256
A_GATE_CAP = 15.0
B_HEADS = 8
B_HEAD_DIM = 128
B_CONV = 4
SSM_GROUP = 16
SSM_STATE = 64
MOE_GROUPS = 8
MOE_PER_GROUP = 8
MOE_EXPERTS = 64
MOE_FF = 768

LANES = 128
SUBLANES = 8
VMEM_LIMIT = 56 * 1024 * 1024

COL_AQ, COL_AK, COL_AV, COL_AO = 0, 512, 1024, 2048
COL_BQ, COL_BK, COL_BV, COL_BZ = 3072, 4096, 5120, 6144
N_PROJ = 7168
G_AI, G_AF, G_BB, G_BA, G_END = 0, 4, 8, 16, 24
GATE_ROWS = 32

MLSTM_CHUNK = 256
GDN_BLOCK = MLSTM_CHUNK
GDN_CHUNK = 64
GDN_HEADS_PER_STEP = 4
S5_BLOCK = 512
S5_GROUPS_PER_BLOCK = 8
S5_NSTATE = S5_GROUPS_PER_BLOCK * SSM_STATE
MOE_ROWS = 512
MOE_SUB = 256
MOE_KT = 2
MOE_KW = D_MODEL // MOE_KT
MOE_DMA_GROUP_LOG2 = 4
MOE_DMA_GROUP = 1 << MOE_DMA_GROUP_LOG2


def _cparams(sem):
    return pltpu.CompilerParams(dimension_semantics=sem, vmem_limit_bytes=VMEM_LIMIT)


def _softcap(t, cap):
    return cap * jnp.tanh(t / cap)


def _log_sigmoid(t):
    return jnp.minimum(t, 0.0) - jnp.log(1.0 + jnp.exp(-jnp.abs(t)))


def _softplus(t):
    return jnp.maximum(t, 0.0) + jnp.log(1.0 + jnp.exp(-jnp.abs(t)))


def _sigmoid(t):
    return 1.0 / (1.0 + jnp.exp(-t))


def _silu(t):
    return t * _sigmoid(t)


def _pick_col(x, idx):
    lane = lax.broadcasted_iota(jnp.int32, x.shape, 1)
    return jnp.sum(jnp.where(lane == idx, x, 0.0), axis=-1, keepdims=True)


def _dot(a, b):
    return jnp.dot(a, b, preferred_element_type=F32)


def _dot_hi(a, b):
    return jnp.dot(a, b, precision=HIGHEST, preferred_element_type=F32)


def _inproj_kernel(x_ref, g_ref, w_ref, wg_ref, o_ref, og_ref, h_scr):
    @pl.when(pl.program_id(1) == 0)
    def _():
        x = x_ref[...]
        ms = jnp.mean(x * x, axis=-1, keepdims=True)
        h_scr[...] = (x * lax.rsqrt(ms + EPS) * g_ref[...]).astype(BF16)
        og_ref[...] = _dot(h_scr[...], wg_ref[...])

    o_ref[...] = _dot(h_scr[...], w_ref[...])


def _inproj(x2, g, w_bf, wg_bf):
    t, d = x2.shape
    n = w_bf.shape[1]
    tm = min(1024, t)
    tn = 512
    return pl.pallas_call(
        _inproj_kernel,
        grid=(t // tm, n // tn),
        in_specs=[pl.BlockSpec((tm, d), lambda i, j: (i, 0)),
                  pl.BlockSpec((1, d), lambda i, j: (0, 0)),
                  pl.BlockSpec((d, tn), lambda i, j: (0, j)),
                  pl.BlockSpec((d, LANES), lambda i, j: (0, 0))],
        out_specs=[pl.BlockSpec((tm, tn), lambda i, j: (i, j)),
                   pl.BlockSpec((tm, LANES), lambda i, j: (i, 0))],
        out_shape=[jax.ShapeDtypeStruct((t, n), F32), jax.ShapeDtypeStruct((t, LANES), F32)],
        scratch_shapes=[pltpu.VMEM((tm, d), BF16)],
        compiler_params=_cparams(("parallel", "arbitrary")),
        name="inproj",
    )(x2, g, w_bf, wg_bf)


def _gate_values(g, alog, idx, cum_a, cum_b):
    sc = _softcap(g, A_GATE_CAP)
    cum_logf = cum_a(_log_sigmoid(sc))
    cum_g = cum_b(-jnp.exp(alog) * _softplus(g))
    return jnp.where(idx < G_AF, sc,
                     jnp.where(idx < G_BB, cum_logf,
                               jnp.where(idx < G_BA, _sigmoid(g), jnp.where(idx < G_END, cum_g, 0.0))))


def _gates_kernel(gc_ref, gr_ref, bc_ref, br_ref, alc_ref, alr_ref, pc_ref, pr_ref):
    L = gc_ref.shape[0]
    row = lax.broadcasted_iota(jnp.int32, (L, L), 0)
    col = lax.broadcasted_iota(jnp.int32, (L, L), 1)
    same = (row // GDN_CHUNK) == (col // GDN_CHUNK)
    tril = (row >= col).astype(F32)
    triu = (row <= col).astype(F32)
    blk_tril = (same & (row >= col)).astype(F32)
    blk_triu = (same & (row <= col)).astype(F32)
    lane = lax.broadcasted_iota(jnp.int32, (L, LANES), 1)
    pc_ref[...] = _gate_values(gc_ref[...] + bc_ref[...], alc_ref[...], lane,
                               lambda v: _dot_hi(tril, v), lambda v: _dot_hi(blk_tril, v))
    sub = lax.broadcasted_iota(jnp.int32, (GATE_ROWS, L), 0)
    pr_ref[...] = _gate_values(gr_ref[...] + br_ref[...], alr_ref[...], sub,
                               lambda v: _dot_hi(v, triu), lambda v: _dot_hi(v, blk_triu))


def _gates(gates, gates_t, bias_col, bias_row, alog_col, alog_row):
    b, s, _ = gates.shape
    L = min(MLSTM_CHUNK, s)
    return pl.pallas_call(
        _gates_kernel,
        grid=(b, s // L),
        in_specs=[pl.BlockSpec((None, L, LANES), lambda i, c: (i, c, 0)),
                  pl.BlockSpec((None, GATE_ROWS, L), lambda i, c: (i, 0, c)),
                  pl.BlockSpec((1, LANES), lambda i, c: (0, 0)),
                  pl.BlockSpec((GATE_ROWS, 1), lambda i, c: (0, 0)),
                  pl.BlockSpec((1, LANES), lambda i, c: (0, 0)),
                  pl.BlockSpec((GATE_ROWS, 1), lambda i, c: (0, 0))],
        out_specs=[pl.BlockSpec((None, L, LANES), lambda i, c: (i, c, 0)),
                   pl.BlockSpec((None, GATE_ROWS, L), lambda i, c: (i, 0, c))],
        out_shape=[jax.ShapeDtypeStruct((b, s, LANES), F32),
                   jax.ShapeDtypeStruct((b, GATE_ROWS, s), F32)],
        compiler_params=_cparams(("parallel", "parallel")),
        name="gates",
    )(gates, gates_t, bias_col, bias_row, alog_col, alog_row)


def _mlstm_kernel(q_ref, k_ref, v_ref, o_ref, pc_ref, pr_ref, nw_ref, out_ref, c_scr, n_scr, m_scr):
    c = pl.program_id(1)
    L = q_ref.shape[0]
    nh, dk, dv = A_HEADS, A_QK_DIM, A_V_DIM

    @pl.when(c == 0)
    def _():
        c_scr[...] = jnp.zeros_like(c_scr)
        n_scr[...] = jnp.zeros_like(n_scr)
        m_scr[...] = jnp.zeros_like(m_scr)

    row = lax.broadcasted_iota(jnp.int32, (L, L), 0)
    col = lax.broadcasted_iota(jnp.int32, (L, L), 1)
    causal = row >= col
    per_head = lambda fn: jnp.stack([fn(h) for h in range(nh)], axis=0)
    bmm = lambda a, b: jnp.einsum("bij,bjk->bik", a.astype(BF16), b.astype(BF16), preferred_element_type=F32)
    bmm_nt = lambda a, b: jnp.einsum("bik,bjk->bij", a.astype(BF16), b.astype(BF16), preferred_element_type=F32)

    pc = pc_ref[...]
    i_col = per_head(lambda h: _pick_col(pc, G_AI + h))
    b_col = per_head(lambda h: _pick_col(pc, G_AF + h))
    i_row = per_head(lambda h: pr_ref[G_AI + h:G_AI + h + 1, :])
    b_row = per_head(lambda h: pr_ref[G_AF + h:G_AF + h + 1, :])
    b_last = b_col[:, L - 1:L, :]

    m_prev = m_scr[...]
    log_d = jnp.where(causal, b_col - b_row + i_row, -jnp.inf)
    log_inter = b_col + m_prev
    m_t = jnp.maximum(log_inter, jnp.max(log_d, axis=-1, keepdims=True))
    dmat = jnp.exp(log_d - m_t)
    inter = jnp.exp(log_inter - m_t)

    q = per_head(lambda h: q_ref[:, h * dk:(h + 1) * dk])
    k = per_head(lambda h: k_ref[:, h * dk:(h + 1) * dk]) * (dk ** -0.5)
    vb = per_head(lambda h: v_ref[:, h * dv:(h + 1) * dv]).astype(BF16)
    scores = bmm_nt(q, k) * dmat
    c_mat = c_scr[...]
    n_vec = n_scr[...]
    num = inter * bmm(q, c_mat) + bmm(scores, vb)
    den = inter * jnp.sum(q * n_vec, axis=-1, keepdims=True) + jnp.sum(scores, axis=-1, keepdims=True)
    hh = num / jnp.maximum(jnp.abs(den), jnp.exp(-m_t))
    hh = hh * lax.rsqrt(jnp.mean(hh * hh, axis=-1, keepdims=True) + EPS)
    for h in range(nh):
        cols = slice(h * dv, (h + 1) * dv)
        out_ref[:, cols] = hh[h] * nw_ref[:, cols] * _sigmoid(o_ref[:, cols])

    le_col = b_last - b_col + i_col
    m_new = jnp.maximum(b_last + m_prev, jnp.max(le_col, axis=1, keepdims=True))
    carry_scale = jnp.exp(b_last + m_prev - m_new)
    kw = k * jnp.exp(le_col - m_new)
    c_scr[...] = c_mat * carry_scale + bmm(jnp.swapaxes(kw, 1, 2), vb)
    n_scr[...] = n_vec * carry_scale + jnp.sum(kw, axis=1, keepdims=True)
    m_scr[...] = m_new


def _mlstm(proj, pc, pr, norm_w):
    b, s, _ = proj.shape
    L = min(MLSTM_CHUNK, s)
    wqk, wv = A_HEADS * A_QK_DIM, A_HEADS * A_V_DIM
    return pl.pallas_call(
        _mlstm_kernel,
        grid=(b, s // L),
        in_specs=[pl.BlockSpec((None, L, wqk), lambda i, c: (i, c, COL_AQ // wqk)),
                  pl.BlockSpec((None, L, wqk), lambda i, c: (i, c, COL_AK // wqk)),
                  pl.BlockSpec((None, L, wv), lambda i, c: (i, c, COL_AV // wv)),
                  pl.BlockSpec((None, L, wv), lambda i, c: (i, c, COL_AO // wv)),
                  pl.BlockSpec((None, L, LANES), lambda i, c: (i, c, 0)),
                  pl.BlockSpec((None, GATE_ROWS, L), lambda i, c: (i, 0, c)),
                  pl.BlockSpec((1, wv), lambda i, c: (0, 0))],
        out_specs=pl.BlockSpec((None, L, wv), lambda i, c: (i, c, 0)),
        out_shape=jax.ShapeDtypeStruct((b, s, wv), F32),
        scratch_shapes=[pltpu.VMEM((A_HEADS, A_QK_DIM, A_V_DIM), F32),
                        pltpu.VMEM((A_HEADS, 1, A_QK_DIM), F32),
                        pltpu.VMEM((A_HEADS, 1, 1), F32)],
        compiler_params=_cparams(("parallel", "arbitrary")),
        name="mlstm",
    )(proj, proj, proj, proj, pc, pr, norm_w)


def _causal_conv_silu(x, tail, w):
    row8 = lax.broadcasted_iota(jnp.int32, (SUBLANES, x.shape[1]), 0)
    acc = x * w[B_CONV - 1:B_CONV, :]
    for d in range(1, B_CONV):
        rolled = pltpu.roll(x, d, 0)
        head = jnp.where(row8 < d, pltpu.roll(tail, d, 0), rolled[0:SUBLANES, :])
        shifted = jnp.concatenate([head, rolled[SUBLANES:, :]], axis=0)
        acc = acc + shifted * w[B_CONV - 1 - d:B_CONV - d, :]
    return _silu(acc)


def _l2norm(t):
    return t * lax.rsqrt(jnp.sum(t * t, axis=-1, keepdims=True) + EPS)


def _gdn_kernel(q_ref, k_ref, v_ref, z_ref, pc_ref, pr_ref, wq_ref, wk_ref, wv_ref, nw_ref, out_ref,
                s_scr, tq_scr, tk_scr, tv_scr):
    hp = pl.program_id(1)
    c = pl.program_id(2)
    Lb = q_ref.shape[0]
    hd = B_HEAD_DIM

    @pl.when(c == 0)
    def _():
        s_scr[...] = jnp.zeros_like(s_scr)
        tq_scr[...] = jnp.zeros_like(tq_scr)
        tk_scr[...] = jnp.zeros_like(tk_scr)
        tv_scr[...] = jnp.zeros_like(tv_scr)

    xq, xk, xv = q_ref[...], k_ref[...], v_ref[...]
    cq = _causal_conv_silu(xq, tq_scr[...], wq_ref[...])
    ck = _causal_conv_silu(xk, tk_scr[...], wk_ref[...])
    cv = _causal_conv_silu(xv, tv_scr[...], wv_ref[...])
    tq_scr[...] = xq[Lb - SUBLANES:, :]
    tk_scr[...] = xk[Lb - SUBLANES:, :]
    tv_scr[...] = xv[Lb - SUBLANES:, :]
    pc = pc_ref[...]
    nw = nw_ref[...]
    heads = [hp * GDN_HEADS_PER_STEP + hh for hh in range(GDN_HEADS_PER_STEP)]
    per_head = lambda x: jnp.stack([x[:, hh * hd:(hh + 1) * hd] for hh in range(GDN_HEADS_PER_STEP)], axis=0)
    hb, s_scr[...] = _gdn_heads(
        _l2norm(per_head(cq)) * (hd ** -0.5), _l2norm(per_head(ck)), per_head(cv),
        jnp.stack([_pick_col(pc, G_BB + h) for h in heads], axis=0),
        jnp.stack([_pick_col(pc, G_BA + h) for h in heads], axis=0),
        jnp.stack([pr_ref[pl.ds(G_BA + h, 1), :] for h in heads], axis=0), s_scr[...])
    hb = hb * lax.rsqrt(jnp.mean(hb * hb, axis=-1, keepdims=True) + EPS) * nw
    for hh in range(GDN_HEADS_PER_STEP):
        cols = slice(hh * hd, (hh + 1) * hd)
        out_ref[:, cols] = hb[hh] * _silu(z_ref[:, cols])


def _gdn_heads(q, k, v, beta, dec_col, dec_row, state):
    nh, Lb, _ = q.shape
    C = GDN_CHUNK
    nsub = Lb // C
    bmm = lambda a, b: jnp.einsum("bij,bjk->bik", a.astype(BF16), b.astype(BF16), preferred_element_type=F32)
    bmm_nt = lambda a, b: jnp.einsum("bik,bjk->bij", a.astype(BF16), b.astype(BF16), preferred_element_type=F32)
    r64 = lax.broadcasted_iota(jnp.int32, (C, C), 0)
    c64 = lax.broadcasted_iota(jnp.int32, (C, C), 1)
    causal = r64 >= c64
    strict = r64 > c64
    eye = (r64 == c64).astype(F32)

    kbeta = k * beta
    edec = jnp.exp(dec_col)
    q_dec = (q * edec).astype(BF16)
    vbeta = (v * beta).astype(BF16)
    kbdec = (kbeta * edec).astype(BF16)
    kb16 = k.astype(BF16)
    qb16 = q.astype(BF16)
    kbeta16 = kbeta.astype(BF16)

    segs, nmats = [], []
    for j in range(nsub):
        sl = slice(j * C, (j + 1) * C)
        seg = jnp.exp(jnp.where(causal, dec_col[:, sl, :] - dec_row[:, :, sl], -jnp.inf))
        a_low = jnp.where(strict, bmm_nt(kbeta16[:, sl, :], kb16[:, sl, :]) * seg, 0.0)
        segs.append(seg)
        nmats.append(-a_low)
    nmat = jnp.concatenate(nmats, axis=0)
    tmat = eye[None] + nmat
    npow = bmm(nmat, nmat)
    for _ in range(4):
        tmat, npow = tmat + bmm(tmat, npow), bmm(npow, npow)
    tmat = tmat + bmm(tmat, npow)

    outs = []
    for j in range(nsub):
        sl = slice(j * C, (j + 1) * C)
        t16 = tmat[j * nh:(j + 1) * nh].astype(BF16)
        u = bmm(t16, vbeta[:, sl, :])
        w = bmm(t16, kbdec[:, sl, :])
        attn = bmm_nt(qb16[:, sl, :], kb16[:, sl, :]) * segs[j]
        s16 = state.astype(BF16)
        v_new = u - bmm(w, s16)
        o = bmm(q_dec[:, sl, :], s16) + bmm(attn, v_new)
        d_last = dec_col[:, (j + 1) * C - 1:(j + 1) * C, :]
        k_end = k[:, sl, :] * jnp.exp(d_last - dec_col[:, sl, :])
        state = state * jnp.exp(d_last) + bmm(jnp.swapaxes(k_end, 1, 2), v_new)
        outs.append(o)
    return jnp.concatenate(outs, axis=1), state


def _gdn(proj, pc, pr, conv_w, norm_w):
    b, s, _ = proj.shape
    Lb = min(GDN_BLOCK, s)
    hd = B_HEAD_DIM
    wd = GDN_HEADS_PER_STEP * hd
    npair = B_HEADS // GDN_HEADS_PER_STEP
    qb, kb, vb, zb = COL_BQ // wd, COL_BK // wd, COL_BV // wd, COL_BZ // wd
    blk = lambda off: pl.BlockSpec((None, Lb, wd), lambda i, h, c: (i, c, off + h))
    return pl.pallas_call(
        _gdn_kernel,
        grid=(b, npair, s // Lb),
        in_specs=[blk(qb), blk(kb), blk(vb), blk(zb),
                  pl.BlockSpec((None, Lb, LANES), lambda i, h, c: (i, c, 0)),
                  pl.BlockSpec((None, GATE_ROWS, Lb), lambda i, h, c: (i, 0, c)),
                  pl.BlockSpec((B_CONV, wd), lambda i, h, c: (0, h)),
                  pl.BlockSpec((B_CONV, wd), lambda i, h, c: (0, npair + h)),
                  pl.BlockSpec((B_CONV, wd), lambda i, h, c: (0, 2 * npair + h)),
                  pl.BlockSpec((1, hd), lambda i, h, c: (0, 0))],
        out_specs=pl.BlockSpec((None, Lb, wd), lambda i, h, c: (i, c, h)),
        out_shape=jax.ShapeDtypeStruct((b, s, B_HEADS * hd), F32),
        scratch_shapes=[pltpu.VMEM((GDN_HEADS_PER_STEP, hd, hd), F32),
                        pltpu.VMEM((SUBLANES, wd), F32),
                        pltpu.VMEM((SUBLANES, wd), F32),
                        pltpu.VMEM((SUBLANES, wd), F32)],
        compiler_params=_cparams(("parallel", "parallel", "arbitrary")),
        name="gdn",
    )(proj, proj, proj, proj, pc, pr, conv_w, conv_w, conv_w, norm_w)


def _outproj_kernel(ya_ref, yb_ref, x_ref, wa_ref, wb_ref, o_ref):
    acc = _dot(ya_ref[...].astype(BF16), wa_ref[...]) + _dot(yb_ref[...].astype(BF16), wb_ref[...])
    o_ref[...] = x_ref[...] + acc


def _outproj(ya, yb, x2, wa, wb):
    t, d = x2.shape
    ka, kb = ya.shape[1], yb.shape[1]
    tm = min(512, t)
    return pl.pallas_call(
        _outproj_kernel,
        grid=(t // tm,),
        in_specs=[pl.BlockSpec((tm, ka), lambda i: (i, 0)),
                  pl.BlockSpec((tm, kb), lambda i: (i, 0)),
                  pl.BlockSpec((tm, d), lambda i: (i, 0)),
                  pl.BlockSpec((ka, d), lambda i: (0, 0)),
                  pl.BlockSpec((kb, d), lambda i: (0, 0))],
        out_specs=pl.BlockSpec((tm, d), lambda i: (i, 0)),
        out_shape=jax.ShapeDtypeStruct((t, d), F32),
        compiler_params=_cparams(("parallel",)),
        name="outproj",
    )(ya, yb, x2, wa, wb)


def _router_kernel(x_ref, g_ref, whi_ref, wlo_ref, b_ref, h_ref, ids_ref, gate_ref):
    x = x_ref[...]
    ms = jnp.mean(x * x, axis=-1, keepdims=True)
    h = x * lax.rsqrt(ms + EPS) * g_ref[...]
    h_ref[...] = h
    h_hi = h.astype(BF16)
    h_lo = (h - h_hi.astype(F32)).astype(BF16)
    w_hi = whi_ref[...]
    logits = _dot(h_hi, w_hi) + _dot(h_lo, w_hi) + _dot(h_hi, wlo_ref[...]) + b_ref[...]
    lane = lax.broadcasted_iota(jnp.int32, logits.shape, 1)
    neg = -jnp.inf
    big = jnp.int32(1 << 20)
    is_g = (lane >= MOE_EXPERTS) & (lane < MOE_EXPERTS + MOE_GROUPS)
    gl = jnp.where(is_g, logits, neg)
    gmax = jnp.max(gl, axis=-1, keepdims=True)
    g_lane = jnp.min(jnp.where(gl == gmax, lane, big), axis=-1, keepdims=True)
    g_idx = g_lane - MOE_EXPERTS
    g_w = 1.0 / jnp.sum(jnp.exp(gl - gmax), axis=-1, keepdims=True)
    in_grp = (lane >= g_idx * MOE_PER_GROUP) & (lane < (g_idx + 1) * MOE_PER_GROUP)
    el = jnp.where(in_grp, logits, neg)
    e0 = jnp.max(el, axis=-1, keepdims=True)
    l0 = jnp.min(jnp.where(el == e0, lane, big), axis=-1, keepdims=True)
    el1 = jnp.where(lane == l0, neg, el)
    e1 = jnp.max(el1, axis=-1, keepdims=True)
    l1 = jnp.min(jnp.where(el1 == e1, lane, big), axis=-1, keepdims=True)
    r = jnp.exp(e1 - e0)
    p0 = 1.0 / (1.0 + r)
    p1 = r / (1.0 + r)
    ids_ref[...] = jnp.where(lane == 0, l0, jnp.where(lane == 1, l1, 0))
    gate_ref[...] = jnp.where(lane == 0, g_w * p0, jnp.where(lane == 1, g_w * p1, 0.0))


def _router(x2, g, w, bias):
    t, d = x2.shape
    tm = min(256, t)
    w_hi = w.astype(BF16)
    w_lo = (w - w_hi.astype(F32)).astype(BF16)
    return pl.pallas_call(
        _router_kernel,
        grid=(t // tm,),
        in_specs=[pl.BlockSpec((tm, d), lambda i: (i, 0)),
                  pl.BlockSpec((1, d), lambda i: (0, 0)),
                  pl.BlockSpec((d, LANES), lambda i: (0, 0)),
                  pl.BlockSpec((d, LANES), lambda i: (0, 0)),
                  pl.BlockSpec((1, LANES), lambda i: (0, 0))],
        out_specs=[pl.BlockSpec((tm, d), lambda i: (i, 0)),
                   pl.BlockSpec((tm, LANES), lambda i: (i, 0)),
                   pl.BlockSpec((tm, LANES), lambda i: (i, 0))],
        out_shape=[jax.ShapeDtypeStruct((t, d), F32),
                   jax.ShapeDtypeStruct((t, LANES), jnp.int32),
                   jax.ShapeDtypeStruct((t, LANES), F32)],
        compiler_params=_cparams(("parallel",)),
        name="router",
    )(x2, g, w_hi, w_lo, bias)


def _moe_plan(expert, n_blocks):
    t = expert.shape[0]
    n_assign = 2 * t
    flat_e = expert.reshape(n_assign)
    onehot = (flat_e[:, None] == jnp.arange(MOE_EXPERTS, dtype=jnp.int32)[None, :]).astype(jnp.int32)
    csum = jnp.cumsum(onehot, axis=0)
    rank = jnp.sum(onehot * csum, axis=1) - 1
    counts = csum[-1]
    nblk = (counts + MOE_ROWS - 1) // MOE_ROWS
    blk_end = jnp.cumsum(nblk)
    blk_start = blk_end - nblk
    dest = jnp.sum(onehot * blk_start[None, :], axis=1) * MOE_ROWS + rank
    spare = 2 * (t + jnp.arange(n_blocks * MOE_ROWS, dtype=jnp.int32) % MOE_DMA_GROUP)
    row_dst = spare.at[dest].set(jnp.arange(n_assign, dtype=jnp.int32))
    bid = jnp.arange(n_blocks, dtype=jnp.int32)
    total = blk_end[-1]
    be = jnp.minimum(jnp.sum((bid[:, None] >= blk_end[None, :]).astype(jnp.int32), axis=1), MOE_EXPERTS - 1)
    used = bid < total
    blk_n = jnp.where(used, jnp.clip(counts[be] - (bid - blk_start[be]) * MOE_ROWS, 0, MOE_ROWS), 0)
    blk_e = jnp.where(used, be, be[jnp.maximum(total - 1, 0)])
    return row_dst, blk_e.astype(jnp.int32), blk_n.astype(jnp.int32)


def _moe_kernel(blk_e_ref, blk_n_ref, dst_ref, h_hbm, wg_ref, wu_ref, wd_ref, yk_hbm,
                xf_scr, xb_scr, g_scr, u_scr, y_scr, z_scr, gsem, ssem):
    b = pl.program_id(0)
    k = pl.program_id(1)
    nb = pl.num_programs(0)
    n = blk_n_ref[b]
    R = MOE_ROWS
    G = MOE_DMA_GROUP
    t, d = h_hbm.shape


    def gather_copy(blk, g, j):
        tok = jnp.minimum(lax.shift_right_logical(dst_ref[blk * R + g * G + j], 1), t - 1)
        return pltpu.make_async_copy(h_hbm.at[pl.ds(tok, 1), :], xf_scr.at[g, pl.ds(j, 1), :], gsem)

    def scatter_copy(blk, g, j):
        dst = dst_ref[blk * R + g * G + j]
        tok = lax.shift_right_logical(dst, 1)
        return pltpu.make_async_copy(y_scr.at[g, pl.ds(j, 1), :], yk_hbm.at[dst & 1, pl.ds(tok, 1), :], ssem)

    def for_rows(blk, fn):
        def body(g, carry):
            for j in range(G):
                fn(blk, g, j)
            return carry
        lax.fori_loop(0, lax.shift_right_logical(blk_n_ref[blk] + (G - 1), MOE_DMA_GROUP_LOG2), body, 0)

    @pl.when((b == 0) & (k == 0))
    def _():
        xf_scr[...] = jnp.zeros_like(xf_scr)
        for_rows(0, lambda *a: gather_copy(*a).start())
        z_scr[...] = jnp.zeros_like(z_scr)
        spare = [pltpu.make_async_copy(z_scr, yk_hbm.at[s, pl.ds(t, MOE_DMA_GROUP), :], ssem) for s in range(2)]
        for cp in spare:
            cp.start()
        for cp in spare:
            cp.wait()

    @pl.when(k == 0)
    def _():
        for_rows(b, lambda *a: gather_copy(*a).wait())
        for kk in range(MOE_KT):
            xb_scr[kk] = xf_scr[:, :, kk * MOE_KW:(kk + 1) * MOE_KW].reshape(R, MOE_KW).astype(BF16)

    @pl.when((k == 1) & (b + 1 < nb))
    def _():
        for_rows(b + 1, lambda *a: gather_copy(*a).start())

    def for_sub_blocks(fn):
        for sb in range(R // MOE_SUB):
            @pl.when(sb * MOE_SUB < n)
            def _():
                fn(slice(sb * MOE_SUB, (sb + 1) * MOE_SUB))

    @pl.when(n > 0)
    def _():
        wg = wg_ref[...].astype(BF16)
        wu = wu_ref[...].astype(BF16)

        def accumulate(rows):
            x = xb_scr[k, rows, :]
            pg = _dot(x, wg)
            pu = _dot(x, wu)

            @pl.when(k == 0)
            def _():
                g_scr[rows, :] = pg
                u_scr[rows, :] = pu

            @pl.when(k != 0)
            def _():
                g_scr[rows, :] += pg
                u_scr[rows, :] += pu

        for_sub_blocks(accumulate)

    @pl.when(k == MOE_KT - 1)
    def _():
        @pl.when(b > 0)
        def _():
            for_rows(b - 1, lambda *a: scatter_copy(*a).wait())

        @pl.when(n > 0)
        def _():
            wd = wd_ref[...].astype(BF16)

            def down(rows):
                hmid = _silu(g_scr[rows, :]) * u_scr[rows, :]
                y = _dot(hmid.astype(BF16), wd)
                y_scr[rows.start // G:rows.stop // G] = y.reshape(MOE_SUB // G, G, d)

            for_sub_blocks(down)
            for_rows(b, lambda *a: scatter_copy(*a).start())

        @pl.when(b == nb - 1)
        def _():
            for_rows(b, lambda *a: scatter_copy(*a).wait())


def _moe_experts(h2, row_dst, blk_e, blk_n, w_gate, w_up, w_down, layer, n_blocks):
    t, d = h2.shape
    R = MOE_ROWS
    last = MOE_KT - 1

    def w_in_map(b, k, be, bn, dst):
        return (layer, be[b], jnp.where(bn[b] > 0, k, last), 0)

    def w_out_map(b, k, be, bn, dst):
        return (layer, be[b], 0, 0)

    grid_spec = pltpu.PrefetchScalarGridSpec(
        num_scalar_prefetch=3,
        grid=(n_blocks, MOE_KT),
        in_specs=[pl.BlockSpec(memory_space=pl.ANY),
                  pl.BlockSpec((None, None, MOE_KW, MOE_FF), w_in_map),
                  pl.BlockSpec((None, None, MOE_KW, MOE_FF), w_in_map),
                  pl.BlockSpec((None, None, MOE_FF, d), w_out_map)],
        out_specs=pl.BlockSpec(memory_space=pl.ANY),
        scratch_shapes=[pltpu.VMEM((R // MOE_DMA_GROUP, MOE_DMA_GROUP, d), F32),
                        pltpu.VMEM((MOE_KT, R, MOE_KW), BF16),
                        pltpu.VMEM((R, MOE_FF), F32), pltpu.VMEM((R, MOE_FF), F32),
                        pltpu.VMEM((R // MOE_DMA_GROUP, MOE_DMA_GROUP, d), F32),
                        pltpu.VMEM((MOE_DMA_GROUP, d), F32),
                        pltpu.SemaphoreType.DMA(()), pltpu.SemaphoreType.DMA(())],
    )
    return pl.pallas_call(
        _moe_kernel,
        grid_spec=grid_spec,
        out_shape=jax.ShapeDtypeStruct((2, t + MOE_DMA_GROUP, d), F32),
        compiler_params=_cparams(("arbitrary", "arbitrary")),
        name="moe_experts",
    )(blk_e, blk_n, row_dst, h2, w_gate, w_up, w_down)


def _combine_kernel(y0_ref, y1_ref, x_ref, gate_ref, g_ref, *out_refs):
    gate = gate_ref[...]
    x = x_ref[...] + gate[:, 0:1] * y0_ref[...] + gate[:, 1:2] * y1_ref[...]
    if len(out_refs) == 2:
        out_refs[0][...] = x
    ms = jnp.mean(x * x, axis=-1, keepdims=True)
    out_refs[-1][...] = x * lax.rsqrt(ms + EPS) * g_ref[...]


def _moe_combine(yk, x2, gate, next_norm, want_stream):
    t, d = x2.shape
    tm = min(256, t)
    n_out = 2 if want_stream else 1
    return pl.pallas_call(
        _combine_kernel,
        grid=(t // tm,),
        in_specs=[pl.BlockSpec((None, tm, d), lambda i: (0, i, 0)),
                  pl.BlockSpec((None, tm, d), lambda i: (1, i, 0)),
                  pl.BlockSpec((tm, d), lambda i: (i, 0)),
                  pl.BlockSpec((tm, LANES), lambda i: (i, 0)),
                  pl.BlockSpec((1, d), lambda i: (0, 0))],
        out_specs=[pl.BlockSpec((tm, d), lambda i: (i, 0))] * n_out,
        out_shape=[jax.ShapeDtypeStruct((t, d), F32)] * n_out,
        compiler_params=_cparams(("parallel",)),
        name="moe_combine",
    )(yk, yk, x2, gate, next_norm)


def _hier_moe(x2, ffn_norm, w_rg, b_rg, w_re, b_re, w_gate, w_up, w_down, layer, next_norm, want_stream=True):
    t, d = x2.shape
    pad = LANES - MOE_EXPERTS - MOE_GROUPS
    w_r = jnp.concatenate([w_re, w_rg, jnp.zeros((d, pad), F32)], axis=1)
    b_r = jnp.concatenate([b_re, b_rg, jnp.zeros((pad,), F32)]).reshape(1, LANES)
    h2, ids, gate = _router(x2, ffn_norm.reshape(1, d), w_r, b_r)
    n_blocks = (2 * t) // MOE_ROWS + MOE_EXPERTS
    row_dst, blk_e, blk_n = _moe_plan(ids[:, :2], n_blocks)
    yk = _moe_experts(h2, row_dst, blk_e, blk_n, w_gate, w_up, w_down, layer, n_blocks)
    return _moe_combine(yk, x2, gate, next_norm.reshape(1, d), want_stream)


def _s5_disc_kernel(lre_ref, lim_ref, ls_ref, bre_ref, bim_ref, are_ref, aim_ref, bbre_ref, bbim_ref):
    lr = jnp.minimum(lre_ref[...], -1e-4)
    li = lim_ref[...]
    step = jnp.exp(ls_ref[...])
    mag = jnp.exp(lr * step)
    ang = li * step
    ab_re = mag * jnp.cos(ang)
    ab_im = mag * jnp.sin(ang)
    den = lr * lr + li * li
    zr = ab_re - 1.0
    f_re = (zr * lr + ab_im * li) / den
    f_im = (ab_im * lr - zr * li) / den
    br = bre_ref[...]
    bi = bim_ref[...]
    are_ref[...] = ab_re
    aim_ref[...] = ab_im
    bbre_ref[...] = f_re * br - f_im * bi
    bbim_ref[...] = f_re * bi + f_im * br


def _s5_discretise(lam_re, lam_im, log_step, b_re, b_im):
    g, p = lam_re.shape
    n = g * p
    col = lambda a: a.reshape(n, 1)
    ls = jnp.broadcast_to(log_step[:, None], (g, p))
    rows = 1024
    full = lambda w: pl.BlockSpec((rows, w), lambda i: (i, 0))
    return pl.pallas_call(
        _s5_disc_kernel,
        grid=(n // rows,),
        in_specs=[full(1), full(1), full(1), full(SSM_GROUP), full(SSM_GROUP)],
        out_specs=[full(1), full(1), full(SSM_GROUP), full(SSM_GROUP)],
        out_shape=[jax.ShapeDtypeStruct((n, 1), F32), jax.ShapeDtypeStruct((n, 1), F32),
                   jax.ShapeDtypeStruct((n, SSM_GROUP), F32), jax.ShapeDtypeStruct((n, SSM_GROUP), F32)],
        compiler_params=_cparams(("parallel",)),
        name="s5_discretise",
    )(col(lam_re), col(lam_im), col(ls), b_re.reshape(n, SSM_GROUP), b_im.reshape(n, SSM_GROUP))


def _cmul(ar, ai, br, bi):
    return ar * br - ai * bi, ar * bi + ai * br


def _s5_kernel(u_ref, bw_ref, cw_ref, a_ref, d_ref, y_ref, bu_scr, x_scr, xb_scr, pow_scr, carry_scr, il_scr):
    tau = pl.program_id(2)
    Lb = u_ref.shape[0]
    seg = Lb // SUBLANES
    ns = S5_NSTATE
    ar = a_ref[0:1, :]
    ai = a_ref[1:2, :]

    @pl.when(tau == 0)
    def _():
        carry_scr[...] = jnp.zeros_like(carry_scr)
        pr, pi = ar, ai
        for i in range(seg):
            pow_scr[i:i + 1, 0:ns] = pr
            pow_scr[i:i + 1, ns:2 * ns] = pi
            pr, pi = _cmul(pr, pi, ar, ai)

    pitch = seg + SUBLANES
    for j in range(SUBLANES):
        il_scr[j * pitch:j * pitch + seg, :] = u_ref[j * seg:(j + 1) * seg, :]
    u_perm = jnp.concatenate([il_scr[pl.ds(i, SUBLANES, stride=pitch), :] for i in range(seg)], axis=0)
    bu_scr[...] = _dot(u_perm.astype(BF16), bw_ref[...])

    ar8 = jnp.broadcast_to(ar, (SUBLANES, ns))
    ai8 = jnp.broadcast_to(ai, (SUBLANES, ns))

    def scan_body(i, carry):
        xr, xi = carry
        r0 = pl.multiple_of(i * SUBLANES, SUBLANES)
        nr = ar8 * xr - ai8 * xi + bu_scr[pl.ds(r0, SUBLANES), 0:ns]
        ni = ar8 * xi + ai8 * xr + bu_scr[pl.ds(r0, SUBLANES), ns:2 * ns]
        x_scr[pl.ds(r0, SUBLANES), 0:ns] = nr
        x_scr[pl.ds(r0, SUBLANES), ns:2 * ns] = ni
        return nr, ni

    zeros = jnp.zeros((SUBLANES, ns), F32)
    er, ei = lax.fori_loop(0, seg, scan_body, (zeros, zeros), unroll=8)

    alr = pow_scr[seg - 1:seg, 0:ns]
    ali = pow_scr[seg - 1:seg, ns:2 * ns]
    pr = carry_scr[0:1, 0:ns]
    pi = carry_scr[0:1, ns:2 * ns]
    prs, pis = [], []
    for j in range(SUBLANES):
        prs.append(pr)
        pis.append(pi)
        mr, mi = _cmul(alr, ali, pr, pi)
        pr = er[j:j + 1, :] + mr
        pi = ei[j:j + 1, :] + mi
    carry_scr[0:1, 0:ns] = pr
    carry_scr[0:1, ns:2 * ns] = pi
    p_re = jnp.concatenate(prs + prs, axis=0)
    p_im = jnp.concatenate(pis + pis, axis=0)
    pack = 2 * SUBLANES

    def fix_body(i2, carry):
        r0 = pl.multiple_of(i2 * pack, pack)
        bc = lambda i, lo: jnp.broadcast_to(pow_scr[pl.ds(i, 1), lo:lo + ns], (SUBLANES, ns))
        wr = jnp.concatenate([bc(2 * i2, 0), bc(2 * i2 + 1, 0)], axis=0)
        wi = jnp.concatenate([bc(2 * i2, ns), bc(2 * i2 + 1, ns)], axis=0)
        mr, mi = _cmul(wr, wi, p_re, p_im)
        xb_scr[pl.ds(r0, pack), 0:ns] = (x_scr[pl.ds(r0, pack), 0:ns] + mr).astype(BF16)
        xb_scr[pl.ds(r0, pack), ns:2 * ns] = (x_scr[pl.ds(r0, pack), ns:2 * ns] + mi).astype(BF16)
        return carry

    lax.fori_loop(0, seg // 2, fix_body, 0, unroll=4)

    y = _dot(xb_scr[...], cw_ref[...]) + d_ref[...] * u_perm
    for i in range(seg):
        il_scr[pl.ds(i, SUBLANES, stride=pitch), :] = y[i * SUBLANES:(i + 1) * SUBLANES, :]
    for j in range(SUBLANES):
        y_ref[j * seg:(j + 1) * seg, :] = il_scr[j * pitch:j * pitch + seg, :]


def _s5_scan(h3, bw, cw, a_rows, d_skip):
    b, s, w = h3.shape
    Lb = min(S5_BLOCK, s)
    nb = w // LANES
    ns2 = 2 * S5_NSTATE
    return pl.pallas_call(
        _s5_kernel,
        grid=(b, nb, s // Lb),
        in_specs=[pl.BlockSpec((None, Lb, LANES), lambda i, k, c: (i, c, k)),
                  pl.BlockSpec((None, LANES, ns2), lambda i, k, c: (k, 0, 0)),
                  pl.BlockSpec((None, ns2, LANES), lambda i, k, c: (k, 0, 0)),
                  pl.BlockSpec((None, 2, S5_NSTATE), lambda i, k, c: (k, 0, 0)),
                  pl.BlockSpec((1, LANES), lambda i, k, c: (0, k))],
        out_specs=pl.BlockSpec((None, Lb, LANES), lambda i, k, c: (i, c, k)),
        out_shape=jax.ShapeDtypeStruct((b, s, w), F32),
        scratch_shapes=[pltpu.VMEM((Lb, ns2), F32), pltpu.VMEM((Lb, ns2), F32), pltpu.VMEM((Lb, ns2), BF16),
                        pltpu.VMEM((Lb // SUBLANES, ns2), F32), pltpu.VMEM((SUBLANES, ns2), F32),
                        pltpu.VMEM((Lb + SUBLANES * SUBLANES, LANES), F32)],
        compiler_params=_cparams(("parallel", "parallel", "arbitrary")),
        name="s5_scan",
    )(h3, bw, cw, a_rows, d_skip)


def _glu_kernel(y_ref, x_ref, wv_ref, wg_ref, bv_ref, bg_ref, o_ref, a_scr):
    @pl.when(pl.program_id(1) == 0)
    def _():
        a_scr[...] = jax.nn.gelu(y_ref[...]).astype(BF16)

    a = a_scr[...]
    val = _dot(a, wv_ref[...]) + bv_ref[...]
    gate = _dot(a, wg_ref[...]) + bg_ref[...]
    o_ref[...] = x_ref[...] + val * _sigmoid(gate)


def _glu(y2, x2, w_bf, bias):
    t, d = x2.shape
    tm = min(512, t)
    tn = 512
    nj = d // tn
    return pl.pallas_call(
        _glu_kernel,
        grid=(t // tm, nj),
        in_specs=[pl.BlockSpec((tm, d), lambda i, j: (i, 0)),
                  pl.BlockSpec((tm, tn), lambda i, j: (i, j)),
                  pl.BlockSpec((d, tn), lambda i, j: (0, j)),
                  pl.BlockSpec((d, tn), lambda i, j: (0, nj + j)),
                  pl.BlockSpec((1, tn), lambda i, j: (0, j)),
                  pl.BlockSpec((1, tn), lambda i, j: (0, nj + j))],
        out_specs=pl.BlockSpec((tm, tn), lambda i, j: (i, j)),
        out_shape=jax.ShapeDtypeStruct((t, d), F32),
        scratch_shapes=[pltpu.VMEM((tm, d), BF16)],
        compiler_params=_cparams(("parallel", "arbitrary")),
        name="glu",
    )(y2, x2, w_bf, w_bf, bias, bias)


def _block_diag(w):
    nb, gb, r, c = w.shape
    eye = jnp.eye(gb, dtype=w.dtype)
    return (w[:, :, :, None, :] * eye[None, :, None, :, None]).reshape(nb, gb * r, gb * c)


def _mixer_s5(h3, x2, lam_re, lam_im, log_step, b_re, b_im, c_re, c_im, d_skip, w_glu, b_glu):
    b, s, w = h3.shape
    g, p = lam_re.shape
    gb = S5_GROUPS_PER_BLOCK
    nb = g // gb
    a_re, a_im, bb_re, bb_im = _s5_discretise(lam_re, lam_im, log_step, b_re, b_im)
    a_rows = jnp.stack([a_re.reshape(nb, gb * p), a_im.reshape(nb, gb * p)], axis=1)
    bt = lambda m: jnp.swapaxes(m.reshape(nb, gb, p, SSM_GROUP), 2, 3)
    bw = jnp.concatenate([_block_diag(bt(bb_re)), _block_diag(bt(bb_im))], axis=2).astype(BF16)
    ct = lambda m: jnp.swapaxes(m.reshape(nb, gb, SSM_GROUP, p), 2, 3)
    cw = jnp.concatenate([_block_diag(ct(c_re)), -_block_diag(ct(c_im))], axis=1).astype(BF16)
    y = _s5_scan(h3, bw, cw, a_rows, d_skip.reshape(1, w))
    return _glu(y.reshape(b * s, w), x2, w_glu.astype(BF16), b_glu.reshape(1, -1))


def _mixer_ab(x2, bsz, norm_w, w_in, a_i_bias, a_f_bias, a_norm, b_conv, b_a_log, b_dt_bias, b_norm, w_out):
    t, d = x2.shape
    s = t // bsz
    zeros = lambda n: jnp.zeros((n,), F32)
    n_a = 2 * A_HEADS * A_QK_DIM + 2 * A_HEADS * A_V_DIM
    n_b = 4 * B_HEADS * B_HEAD_DIM
    a_end = n_a + 2 * A_HEADS
    w_gates = jnp.concatenate([w_in[:, n_a:a_end], w_in[:, a_end + n_b:],
                               jnp.zeros((d, LANES - G_END), F32)], axis=1).astype(BF16)
    w_wide = jnp.concatenate([w_in[:, :n_a], w_in[:, a_end:a_end + n_b]], axis=1).astype(BF16)
    proj, gates = _inproj(x2, norm_w.reshape(1, d), w_wide, w_gates)
    proj = proj.reshape(bsz, s, N_PROJ)
    gates = gates.reshape(bsz, s, LANES)
    gates_t = jnp.swapaxes(gates[:, :, :GATE_ROWS], 1, 2)
    bias = jnp.concatenate([a_i_bias, a_f_bias, zeros(B_HEADS), b_dt_bias, zeros(LANES - G_END)])
    alog = jnp.concatenate([zeros(G_BA), b_a_log, zeros(LANES - G_END)])
    pc, pr = _gates(gates, gates_t, bias.reshape(1, LANES), bias[:GATE_ROWS].reshape(GATE_ROWS, 1),
                    alog.reshape(1, LANES), alog[:GATE_ROWS].reshape(GATE_ROWS, 1))
    ya = _mlstm(proj, pc, pr, a_norm.reshape(1, -1))
    yb = _gdn(proj, pc, pr, b_conv, b_norm.reshape(1, -1))
    n_ya = A_HEADS * A_V_DIM
    w_out_bf = w_out.astype(BF16)
    return _outproj(ya.reshape(t, n_ya), yb.reshape(t, -1), x2, w_out_bf[:n_ya], w_out_bf[n_ya:])


def kernel(x, mix_norm, ab_w_in, mlstm_i_bias, mlstm_f_bias, mlstm_norm, gdn_conv, gdn_a_log, gdn_dt_bias, gdn_norm, ab_w_out, ssm_lambda_re, ssm_lambda_im, ssm_log_step, ssm_b_re, ssm_b_im, ssm_c_re, ssm_c_im, ssm_d, glu_w, glu_b, ffn_norm, router_group_w, router_group_b, router_expert_w, router_expert_b, expert_w_gate, expert_w_up, expert_w_down, final_norm):
    bsz, s, d = x.shape
    t = bsz * s
    x2 = x.reshape(t, d)
    x2 = _mixer_ab(x2, bsz, mix_norm[0], ab_w_in[0], mlstm_i_bias[0], mlstm_f_bias[0], mlstm_norm[0],
                   gdn_conv[0], gdn_a_log[0], gdn_dt_bias[0], gdn_norm[0], ab_w_out[0])
    x2, h3 = _hier_moe(x2, ffn_norm[0], router_group_w[0], router_group_b[0], router_expert_w[0],
                       router_expert_b[0], expert_w_gate, expert_w_up, expert_w_down, 0, mix_norm[1])
    x2 = _mixer_s5(h3.reshape(bsz, s, d), x2, ssm_lambda_re[0], ssm_lambda_im[0], ssm_log_step[0],
                   ssm_b_re[0], ssm_b_im[0], ssm_c_re[0], ssm_c_im[0], ssm_d[0], glu_w[0], glu_b[0])
    out, = _hier_moe(x2, ffn_norm[1], router_group_w[1], router_group_b[1], router_expert_w[1],
                     router_expert_b[1], expert_w_gate, expert_w_up, expert_w_down, 1, final_norm, want_stream=False)
    return out.reshape(bsz, s, d)
```

```python
import functools

import jax
import jax.numpy as jnp
from jax import lax
from jax.experimental import pallas as pl
from jax.experimental.pallas import tpu as pltpu

F32 = jnp.float32
BF16 = jnp.bfloat16
HIGHEST = lax.Precision.HIGHEST

EPS = 1e-6
D_MODEL = 2048
A_HEADS = 4
A_QK_DIM = 128
A_V_DIM = 256
A_GATE_CAP = 15.0
B_HEADS = 8
B_HEAD_DIM = 128
B_CONV = 4
SSM_GROUP = 16
SSM_STATE = 64
MOE_GROUPS = 8
MOE_PER_GROUP = 8
MOE_EXPERTS = 64
MOE_FF = 768

LANES = 128
SUBLANES = 8
VMEM_LIMIT = 56 * 1024 * 1024

COL_AQ, COL_AK, COL_AV, COL_AO = 0, 512, 1024, 2048
COL_BQ, COL_BK, COL_BV, COL_BZ = 3072, 4096, 5120, 6144
N_PROJ = 7168
G_AI, G_AF, G_BB, G_BA, G_END = 0, 4, 8, 16, 24
GATE_ROWS = 32

MLSTM_CHUNK = 256
GDN_BLOCK = MLSTM_CHUNK
GDN_CHUNK = 64
GDN_HEADS_PER_STEP = 4
S5_BLOCK = 512
S5_GROUPS_PER_BLOCK = 8
S5_NSTATE = S5_GROUPS_PER_BLOCK * SSM_STATE
MOE_ROWS = 512
MOE_SUB = 256
MOE_KT = 2
MOE_KW = D_MODEL // MOE_KT
MOE_DMA_GROUP_LOG2 = 4
MOE_DMA_GROUP = 1 << MOE_DMA_GROUP_LOG2


def _cparams(sem):
    return pltpu.CompilerParams(dimension_semantics=sem, vmem_limit_bytes=VMEM_LIMIT)


def _softcap(t, cap):
    return cap * jnp.tanh(t / cap)


def _log_sigmoid(t):
    return jnp.minimum(t, 0.0) - jnp.log(1.0 + jnp.exp(-jnp.abs(t)))


def _softplus(t):
    return jnp.maximum(t, 0.0) + jnp.log(1.0 + jnp.exp(-jnp.abs(t)))


def _sigmoid(t):
    return 1.0 / (1.0 + jnp.exp(-t))


def _silu(t):
    return t * _sigmoid(t)


def _pick_col(x, idx):
    lane = lax.broadcasted_iota(jnp.int32, x.shape, 1)
    return jnp.sum(jnp.where(lane == idx, x, 0.0), axis=-1, keepdims=True)


def _dot(a, b):
    return jnp.dot(a, b, preferred_element_type=F32)


def _dot_hi(a, b):
    return jnp.dot(a, b, precision=HIGHEST, preferred_element_type=F32)


def _inproj_kernel(x_ref, g_ref, w_ref, wg_ref, o_ref, og_ref, h_scr):
    @pl.when(pl.program_id(1) == 0)
    def _():
        x = x_ref[...]
        ms = jnp.mean(x * x, axis=-1, keepdims=True)
        h_scr[...] = (x * lax.rsqrt(ms + EPS) * g_ref[...]).astype(BF16)
        og_ref[...] = _dot(h_scr[...], wg_ref[...])

    o_ref[...] = _dot(h_scr[...], w_ref[...])


def _inproj(x2, g, w_bf, wg_bf):
    t, d = x2.shape
    n = w_bf.shape[1]
    tm = min(1024, t)
    tn = 512
    return pl.pallas_call(
        _inproj_kernel,
        grid=(t // tm, n // tn),
        in_specs=[pl.BlockSpec((tm, d), lambda i, j: (i, 0)),
                  pl.BlockSpec((1, d), lambda i, j: (0, 0)),
                  pl.BlockSpec((d, tn), lambda i, j: (0, j)),
                  pl.BlockSpec((d, LANES), lambda i, j: (0, 0))],
        out_specs=[pl.BlockSpec((tm, tn), lambda i, j: (i, j)),
                   pl.BlockSpec((tm, LANES), lambda i, j: (i, 0))],
        out_shape=[jax.ShapeDtypeStruct((t, n), F32), jax.ShapeDtypeStruct((t, LANES), F32)],
        scratch_shapes=[pltpu.VMEM((tm, d), BF16)],
        compiler_params=_cparams(("parallel", "arbitrary")),
        name="inproj",
    )(x2, g, w_bf, wg_bf)


def _gate_values(g, alog, idx, cum_a, cum_b):
    sc = _softcap(g, A_GATE_CAP)
    cum_logf = cum_a(_log_sigmoid(sc))
    cum_g = cum_b(-jnp.exp(alog) * _softplus(g))
    return jnp.where(idx < G_AF, sc,
                     jnp.where(idx < G_BB, cum_logf,
                               jnp.where(idx < G_BA, _sigmoid(g), jnp.where(idx < G_END, cum_g, 0.0))))


def _gates_kernel(gc_ref, gr_ref, bc_ref, br_ref, alc_ref, alr_ref, pc_ref, pr_ref):
    L = gc_ref.shape[0]
    row = lax.broadcasted_iota(jnp.int32, (L, L), 0)
    col = lax.broadcasted_iota(jnp.int32, (L, L), 1)
    same = (row // GDN_CHUNK) == (col // GDN_CHUNK)
    tril = (row >= col).astype(F32)
    triu = (row <= col).astype(F32)
    blk_tril = (same & (row >= col)).astype(F32)
    blk_triu = (same & (row <= col)).astype(F32)
    lane = lax.broadcasted_iota(jnp.int32, (L, LANES), 1)
    pc_ref[...] = _gate_values(gc_ref[...] + bc_ref[...], alc_ref[...], lane,
                               lambda v: _dot_hi(tril, v), lambda v: _dot_hi(blk_tril, v))
    sub = lax.broadcasted_iota(jnp.int32, (GATE_ROWS, L), 0)
    pr_ref[...] = _gate_values(gr_ref[...] + br_ref[...], alr_ref[...], sub,
                               lambda v: _dot_hi(v, triu), lambda v: _dot_hi(v, blk_triu))


def _gates(gates, gates_t, bias_col, bias_row, alog_col, alog_row):
    b, s, _ = gates.shape
    L = min(MLSTM_CHUNK, s)
    return pl.pallas_call(
        _gates_kernel,
        grid=(b, s // L),
        in_specs=[pl.BlockSpec((None, L, LANES), lambda i, c: (i, c, 0)),
                  pl.BlockSpec((None, GATE_ROWS, L), lambda i, c: (i, 0, c)),
                  pl.BlockSpec((1, LANES), lambda i, c: (0, 0)),
                  pl.BlockSpec((GATE_ROWS, 1), lambda i, c: (0, 0)),
                  pl.BlockSpec((1, LANES), lambda i, c: (0, 0)),
                  pl.BlockSpec((GATE_ROWS, 1), lambda i, c: (0, 0))],
        out_specs=[pl.BlockSpec((None, L, LANES), lambda i, c: (i, c, 0)),
                   pl.BlockSpec((None, GATE_ROWS, L), lambda i, c: (i, 0, c))],
        out_shape=[jax.ShapeDtypeStruct((b, s, LANES), F32),
                   jax.ShapeDtypeStruct((b, GATE_ROWS, s), F32)],
        compiler_params=_cparams(("parallel", "parallel")),
        name="gates",
    )(gates, gates_t, bias_col, bias_row, alog_col, alog_row)


def _mlstm_kernel(q_ref, k_ref, v_ref, o_ref, pc_ref, pr_ref, nw_ref, out_ref, c_scr, n_scr, m_scr):
    c = pl.program_id(1)
    L = q_ref.shape[0]
    nh, dk, dv = A_HEADS, A_QK_DIM, A_V_DIM

    @pl.when(c == 0)
    def _():
        c_scr[...] = jnp.zeros_like(c_scr)
        n_scr[...] = jnp.zeros_like(n_scr)
        m_scr[...] = jnp.zeros_like(m_scr)

    row = lax.broadcasted_iota(jnp.int32, (L, L), 0)
    col = lax.broadcasted_iota(jnp.int32, (L, L), 1)
    causal = row >= col
    per_head = lambda fn: jnp.stack([fn(h) for h in range(nh)], axis=0)
    bmm = lambda a, b: jnp.einsum("bij,bjk->bik", a.astype(BF16), b.astype(BF16), preferred_element_type=F32)
    bmm_nt = lambda a, b: jnp.einsum("bik,bjk->bij", a.astype(BF16), b.astype(BF16), preferred_element_type=F32)

    pc = pc_ref[...]
    i_col = per_head(lambda h: _pick_col(pc, G_AI + h))
    b_col = per_head(lambda h: _pick_col(pc, G_AF + h))
    i_row = per_head(lambda h: pr_ref[G_AI + h:G_AI + h + 1, :])
    b_row = per_head(lambda h: pr_ref[G_AF + h:G_AF + h + 1, :])
    b_last = b_col[:, L - 1:L, :]

    m_prev = m_scr[...]
    log_d = jnp.where(causal, b_col - b_row + i_row, -jnp.inf)
    log_inter = b_col + m_prev
    m_t = jnp.maximum(log_inter, jnp.max(log_d, axis=-1, keepdims=True))
    dmat = jnp.exp(log_d - m_t)
    inter = jnp.exp(log_inter - m_t)

    q = per_head(lambda h: q_ref[:, h * dk:(h + 1) * dk])
    k = per_head(lambda h: k_ref[:, h * dk:(h + 1) * dk]) * (dk ** -0.5)
    vb = per_head(lambda h: v_ref[:, h * dv:(h + 1) * dv]).astype(BF16)
    scores = bmm_nt(q, k) * dmat
    c_mat = c_scr[...]
    n_vec = n_scr[...]
    num = inter * bmm(q, c_mat) + bmm(scores, vb)
    den = inter * jnp.sum(q * n_vec, axis=-1, keepdims=True) + jnp.sum(scores, axis=-1, keepdims=True)
    hh = num / jnp.maximum(jnp.abs(den), jnp.exp(-m_t))
    hh = hh * lax.rsqrt(jnp.mean(hh * hh, axis=-1, keepdims=True) + EPS)
    for h in range(nh):
        cols = slice(h * dv, (h + 1) * dv)
        out_ref[:, cols] = hh[h] * nw_ref[:, cols] * _sigmoid(o_ref[:, cols])

    le_col = b_last - b_col + i_col
    m_new = jnp.maximum(b_last + m_prev, jnp.max(le_col, axis=1, keepdims=True))
    carry_scale = jnp.exp(b_last + m_prev - m_new)
    kw = k * jnp.exp(le_col - m_new)
    c_scr[...] = c_mat * carry_scale + bmm(jnp.swapaxes(kw, 1, 2), vb)
    n_scr[...] = n_vec * carry_scale + jnp.sum(kw, axis=1, keepdims=True)
    m_scr[...] = m_new


def _mlstm(proj, pc, pr, norm_w):
    b, s, _ = proj.shape
    L = min(MLSTM_CHUNK, s)
    wqk, wv = A_HEADS * A_QK_DIM, A_HEADS * A_V_DIM
    return pl.pallas_call(
        _mlstm_kernel,
        grid=(b, s // L),
        in_specs=[pl.BlockSpec((None, L, wqk), lambda i, c: (i, c, COL_AQ // wqk)),
                  pl.BlockSpec((None, L, wqk), lambda i, c: (i, c, COL_AK // wqk)),
                  pl.BlockSpec((None, L, wv), lambda i, c: (i, c, COL_AV // wv)),
                  pl.BlockSpec((None, L, wv), lambda i, c: (i, c, COL_AO // wv)),
                  pl.BlockSpec((None, L, LANES), lambda i, c: (i, c, 0)),
                  pl.BlockSpec((None, GATE_ROWS, L), lambda i, c: (i, 0, c)),
                  pl.BlockSpec((1, wv), lambda i, c: (0, 0))],
        out_specs=pl.BlockSpec((None, L, wv), lambda i, c: (i, c, 0)),
        out_shape=jax.ShapeDtypeStruct((b, s, wv), F32),
        scratch_shapes=[pltpu.VMEM((A_HEADS, A_QK_DIM, A_V_DIM), F32),
                        pltpu.VMEM((A_HEADS, 1, A_QK_DIM), F32),
                        pltpu.VMEM((A_HEADS, 1, 1), F32)],
        compiler_params=_cparams(("parallel", "arbitrary")),
        name="mlstm",
    )(proj, proj, proj, proj, pc, pr, norm_w)


def _causal_conv_silu(x, tail, w):
    row8 = lax.broadcasted_iota(jnp.int32, (SUBLANES, x.shape[1]), 0)
    acc = x * w[B_CONV - 1:B_CONV, :]
    for d in range(1, B_CONV):
        rolled = pltpu.roll(x, d, 0)
        head = jnp.where(row8 < d, pltpu.roll(tail, d, 0), rolled[0:SUBLANES, :])
        shifted = jnp.concatenate([head, rolled[SUBLANES:, :]], axis=0)
        acc = acc + shifted * w[B_CONV - 1 - d:B_CONV - d, :]
    return _silu(acc)


def _l2norm(t):
    return t * lax.rsqrt(jnp.sum(t * t, axis=-1, keepdims=True) + EPS)


def _gdn_kernel(q_ref, k_ref, v_ref, z_ref, pc_ref, pr_ref, wq_ref, wk_ref, wv_ref, nw_ref, out_ref,
                s_scr, tq_scr, tk_scr, tv_scr):
    hp = pl.program_id(1)
    c = pl.program_id(2)
    Lb = q_ref.shape[0]
    hd = B_HEAD_DIM

    @pl.when(c == 0)
    def _():
        s_scr[...] = jnp.zeros_like(s_scr)
        tq_scr[...] = jnp.zeros_like(tq_scr)
        tk_scr[...] = jnp.zeros_like(tk_scr)
        tv_scr[...] = jnp.zeros_like(tv_scr)

    xq, xk, xv = q_ref[...], k_ref[...], v_ref[...]
    cq = _causal_conv_silu(xq, tq_scr[...], wq_ref[...])
    ck = _causal_conv_silu(xk, tk_scr[...], wk_ref[...])
    cv = _causal_conv_silu(xv, tv_scr[...], wv_ref[...])
    tq_scr[...] = xq[Lb - SUBLANES:, :]
    tk_scr[...] = xk[Lb - SUBLANES:, :]
    tv_scr[...] = xv[Lb - SUBLANES:, :]
    pc = pc_ref[...]
    nw = nw_ref[...]
    heads = [hp * GDN_HEADS_PER_STEP + hh for hh in range(GDN_HEADS_PER_STEP)]
    per_head = lambda x: jnp.stack([x[:, hh * hd:(hh + 1) * hd] for hh in range(GDN_HEADS_PER_STEP)], axis=0)
    hb, s_scr[...] = _gdn_heads(
        _l2norm(per_head(cq)) * (hd ** -0.5), _l2norm(per_head(ck)), per_head(cv),
        jnp.stack([_pick_col(pc, G_BB + h) for h in heads], axis=0),
        jnp.stack([_pick_col(pc, G_BA + h) for h in heads], axis=0),
        jnp.stack([pr_ref[pl.ds(G_BA + h, 1), :] for h in heads], axis=0), s_scr[...])
    hb = hb * lax.rsqrt(jnp.mean(hb * hb, axis=-1, keepdims=True) + EPS) * nw
    for hh in range(GDN_HEADS_PER_STEP):
        cols = slice(hh * hd, (hh + 1) * hd)
        out_ref[:, cols] = hb[hh] * _silu(z_ref[:, cols])


def _gdn_heads(q, k, v, beta, dec_col, dec_row, state):
    nh, Lb, _ = q.shape
    C = GDN_CHUNK
    nsub = Lb // C
    bmm = lambda a, b: jnp.einsum("bij,bjk->bik", a.astype(BF16), b.astype(BF16), preferred_element_type=F32)
    bmm_nt = lambda a, b: jnp.einsum("bik,bjk->bij", a.astype(BF16), b.astype(BF16), preferred_element_type=F32)
    r64 = lax.broadcasted_iota(jnp.int32, (C, C), 0)
    c64 = lax.broadcasted_iota(jnp.int32, (C, C), 1)
    causal = r64 >= c64
    strict = r64 > c64
    eye = (r64 == c64).astype(F32)

    kbeta = k * beta
    edec = jnp.exp(dec_col)
    q_dec = (q * edec).astype(BF16)
    vbeta = (v * beta).astype(BF16)
    kbdec = (kbeta * edec).astype(BF16)
    kb16 = k.astype(BF16)
    qb16 = q.astype(BF16)
    kbeta16 = kbeta.astype(BF16)

    segs, nmats = [], []
    for j in range(nsub):
        sl = slice(j * C, (j + 1) * C)
        seg = jnp.exp(jnp.where(causal, dec_col[:, sl, :] - dec_row[:, :, sl], -jnp.inf))
        a_low = jnp.where(strict, bmm_nt(kbeta16[:, sl, :], kb16[:, sl, :]) * seg, 0.0)
        segs.append(seg)
        nmats.append(-a_low)
    nmat = jnp.concatenate(nmats, axis=0)
    tmat = eye[None] + nmat
    npow = bmm(nmat, nmat)
    for _ in range(4):
        tmat, npow = tmat + bmm(tmat, npow), bmm(npow, npow)
    tmat = tmat + bmm(tmat, npow)

    outs = []
    for j in range(nsub):
        sl = slice(j * C, (j + 1) * C)
        t16 = tmat[j * nh:(j + 1) * nh].astype(BF16)
        u = bmm(t16, vbeta[:, sl, :])
        w = bmm(t16, kbdec[:, sl, :])
        attn = bmm_nt(qb16[:, sl, :], kb16[:, sl, :]) * segs[j]
        s16 = state.astype(BF16)
        v_new = u - bmm(w, s16)
        o = bmm(q_dec[:, sl, :], s16) + bmm(attn, v_new)
        d_last = dec_col[:, (j + 1) * C - 1:(j + 1) * C, :]
        k_end = k[:, sl, :] * jnp.exp(d_last - dec_col[:, sl, :])
        state = state * jnp.exp(d_last) + bmm(jnp.swapaxes(k_end, 1, 2), v_new)
        outs.append(o)
    return jnp.concatenate(outs, axis=1), state


def _gdn(proj, pc, pr, conv_w, norm_w):
    b, s, _ = proj.shape
    Lb = min(GDN_BLOCK, s)
    hd = B_HEAD_DIM
    wd = GDN_HEADS_PER_STEP * hd
    npair = B_HEADS // GDN_HEADS_PER_STEP
    qb, kb, vb, zb = COL_BQ // wd, COL_BK // wd, COL_BV // wd, COL_BZ // wd
    blk = lambda off: pl.BlockSpec((None, Lb, wd), lambda i, h, c: (i, c, off + h))
    return pl.pallas_call(
        _gdn_kernel,
        grid=(b, npair, s // Lb),
        in_specs=[blk(qb), blk(kb), blk(vb), blk(zb),
                  pl.BlockSpec((None, Lb, LANES), lambda i, h, c: (i, c, 0)),
                  pl.BlockSpec((None, GATE_ROWS, Lb), lambda i, h, c: (i, 0, c)),
                  pl.BlockSpec((B_CONV, wd), lambda i, h, c: (0, h)),
                  pl.BlockSpec((B_CONV, wd), lambda i, h, c: (0, npair + h)),
                  pl.BlockSpec((B_CONV, wd), lambda i, h, c: (0, 2 * npair + h)),
                  pl.BlockSpec((1, hd), lambda i, h, c: (0, 0))],
        out_specs=pl.BlockSpec((None, Lb, wd), lambda i, h, c: (i, c, h)),
        out_shape=jax.ShapeDtypeStruct((b, s, B_HEADS * hd), F32),
        scratch_shapes=[pltpu.VMEM((GDN_HEADS_PER_STEP, hd, hd), F32),
                        pltpu.VMEM((SUBLANES, wd), F32),
                        pltpu.VMEM((SUBLANES, wd), F32),
                        pltpu.VMEM((SUBLANES, wd), F32)],
        compiler_params=_cparams(("parallel", "parallel", "arbitrary")),
        name="gdn",
    )(proj, proj, proj, proj, pc, pr, conv_w, conv_w, conv_w, norm_w)


def _outproj_kernel(ya_ref, yb_ref, x_ref, wa_ref, wb_ref, o_ref):
    acc = _dot(ya_ref[...].astype(BF16), wa_ref[...]) + _dot(yb_ref[...].astype(BF16), wb_ref[...])
    o_ref[...] = x_ref[...] + acc


def _outproj(ya, yb, x2, wa, wb):
    t, d = x2.shape
    ka, kb = ya.shape[1], yb.shape[1]
    tm = min(512, t)
    return pl.pallas_call(
        _outproj_kernel,
        grid=(t // tm,),
        in_specs=[pl.BlockSpec((tm, ka), lambda i: (i, 0)),
                  pl.BlockSpec((tm, kb), lambda i: (i, 0)),
                  pl.BlockSpec((tm, d), lambda i: (i, 0)),
                  pl.BlockSpec((ka, d), lambda i: (0, 0)),
                  pl.BlockSpec((kb, d), lambda i: (0, 0))],
        out_specs=pl.BlockSpec((tm, d), lambda i: (i, 0)),
        out_shape=jax.ShapeDtypeStruct((t, d), F32),
        compiler_params=_cparams(("parallel",)),
        name="outproj",
    )(ya, yb, x2, wa, wb)


def _router_kernel(x_ref, g_ref, whi_ref, wlo_ref, b_ref, h_ref, ids_ref, gate_ref):
    x = x_ref[...]
    ms = jnp.mean(x * x, axis=-1, keepdims=True)
    h = x * lax.rsqrt(ms + EPS) * g_ref[...]
    h_ref[...] = h
    h_hi = h.astype(BF16)
    h_lo = (h - h_hi.astype(F32)).astype(BF16)
    w_hi = whi_ref[...]
    logits = _dot(h_hi, w_hi) + _dot(h_lo, w_hi) + _dot(h_hi, wlo_ref[...]) + b_ref[...]
    lane = lax.broadcasted_iota(jnp.int32, logits.shape, 1)
    neg = -jnp.inf
    big = jnp.int32(1 << 20)
    is_g = (lane >= MOE_EXPERTS) & (lane < MOE_EXPERTS + MOE_GROUPS)
    gl = jnp.where(is_g, logits, neg)
    gmax = jnp.max(gl, axis=-1, keepdims=True)
    g_lane = jnp.min(jnp.where(gl == gmax, lane, big), axis=-1, keepdims=True)
    g_idx = g_lane - MOE_EXPERTS
    g_w = 1.0 / jnp.sum(jnp.exp(gl - gmax), axis=-1, keepdims=True)
    in_grp = (lane >= g_idx * MOE_PER_GROUP) & (lane < (g_idx + 1) * MOE_PER_GROUP)
    el = jnp.where(in_grp, logits, neg)
    e0 = jnp.max(el, axis=-1, keepdims=True)
    l0 = jnp.min(jnp.where(el == e0, lane, big), axis=-1, keepdims=True)
    el1 = jnp.where(lane == l0, neg, el)
    e1 = jnp.max(el1, axis=-1, keepdims=True)
    l1 = jnp.min(jnp.where(el1 == e1, lane, big), axis=-1, keepdims=True)
    r = jnp.exp(e1 - e0)
    p0 = 1.0 / (1.0 + r)
    p1 = r / (1.0 + r)
    ids_ref[...] = jnp.where(lane == 0, l0, jnp.where(lane == 1, l1, 0))
    gate_ref[...] = jnp.where(lane == 0, g_w * p0, jnp.where(lane == 1, g_w * p1, 0.0))


def _router(x2, g, w, bias):
    t, d = x2.shape
    tm = min(256, t)
    w_hi = w.astype(BF16)
    w_lo = (w - w_hi.astype(F32)).astype(BF16)
    return pl.pallas_call(
        _router_kernel,
        grid=(t // tm,),
        in_specs=[pl.BlockSpec((tm, d), lambda i: (i, 0)),
                  pl.BlockSpec((1, d), lambda i: (0, 0)),
                  pl.BlockSpec((d, LANES), lambda i: (0, 0)),
                  pl.BlockSpec((d, LANES), lambda i: (0, 0)),
                  pl.BlockSpec((1, LANES), lambda i: (0, 0))],
        out_specs=[pl.BlockSpec((tm, d), lambda i: (i, 0)),
                   pl.BlockSpec((tm, LANES), lambda i: (i, 0)),
                   pl.BlockSpec((tm, LANES), lambda i: (i, 0))],
        out_shape=[jax.ShapeDtypeStruct((t, d), F32),
                   jax.ShapeDtypeStruct((t, LANES), jnp.int32),
                   jax.ShapeDtypeStruct((t, LANES), F32)],
        compiler_params=_cparams(("parallel",)),
        name="router",
    )(x2, g, w_hi, w_lo, bias)


def _moe_plan(expert, n_blocks):
    t = expert.shape[0]
    n_assign = 2 * t
    flat_e = expert.reshape(n_assign)
    onehot = (flat_e[:, None] == jnp.arange(MOE_EXPERTS, dtype=jnp.int32)[None, :]).astype(jnp.int32)
    csum = jnp.cumsum(onehot, axis=0)
    rank = jnp.sum(onehot * csum, axis=1) - 1
    counts = csum[-1]
    nblk = (counts + MOE_ROWS - 1) // MOE_ROWS
    blk_end = jnp.cumsum(nblk)
    blk_start = blk_end - nblk
    dest = jnp.sum(onehot * blk_start[None, :], axis=1) * MOE_ROWS + rank
    spare = 2 * (t + jnp.arange(n_blocks * MOE_ROWS, dtype=jnp.int32) % MOE_DMA_GROUP)
    row_dst = spare.at[dest].set(jnp.arange(n_assign, dtype=jnp.int32))
    bid = jnp.arange(n_blocks, dtype=jnp.int32)
    total = blk_end[-1]
    be = jnp.minimum(jnp.sum((bid[:, None] >= blk_end[None, :]).astype(jnp.int32), axis=1), MOE_EXPERTS - 1)
    used = bid < total
    blk_n = jnp.where(used, jnp.clip(counts[be] - (bid - blk_start[be]) * MOE_ROWS, 0, MOE_ROWS), 0)
    blk_e = jnp.where(used, be, be[jnp.maximum(total - 1, 0)])
    return row_dst, blk_e.astype(jnp.int32), blk_n.astype(jnp.int32)


def _moe_kernel(blk_e_ref, blk_n_ref, dst_ref, h_hbm, wg_ref, wu_ref, wd_ref, yk_hbm,
                xf_scr, xb_scr, g_scr, u_scr, y_scr, z_scr, gsem, ssem):
    b = pl.program_id(0)
    k = pl.program_id(1)
    nb = pl.num_programs(0)
    n = blk_n_ref[b]
    R = MOE_ROWS
    G = MOE_DMA_GROUP
    t, d = h_hbm.shape


    def gather_copy(blk, g, j):
        tok = jnp.minimum(lax.shift_right_logical(dst_ref[blk * R + g * G + j], 1), t - 1)
        return pltpu.make_async_copy(h_hbm.at[pl.ds(tok, 1), :], xf_scr.at[g, pl.ds(j, 1), :], gsem)

    def scatter_copy(blk, g, j):
        dst = dst_ref[blk * R + g * G + j]
        tok = lax.shift_right_logical(dst, 1)
        return pltpu.make_async_copy(y_scr.at[g, pl.ds(j, 1), :], yk_hbm.at[dst & 1, pl.ds(tok, 1), :], ssem)

    def for_rows(blk, fn):
        def body(g, carry):
            for j in range(G):
                fn(blk, g, j)
            return carry
        lax.fori_loop(0, lax.shift_right_logical(blk_n_ref[blk] + (G - 1), MOE_DMA_GROUP_LOG2), body, 0)

    @pl.when((b == 0) & (k == 0))
    def _():
        xf_scr[...] = jnp.zeros_like(xf_scr)
        for_rows(0, lambda *a: gather_copy(*a).start())
        z_scr[...] = jnp.zeros_like(z_scr)
        spare = [pltpu.make_async_copy(z_scr, yk_hbm.at[s, pl.ds(t, MOE_DMA_GROUP), :], ssem) for s in range(2)]
        for cp in spare:
            cp.start()
        for cp in spare:
            cp.wait()

    @pl.when(k == 0)
    def _():
        for_rows(b, lambda *a: gather_copy(*a).wait())
        for kk in range(MOE_KT):
            xb_scr[kk] = xf_scr[:, :, kk * MOE_KW:(kk + 1) * MOE_KW].reshape(R, MOE_KW).astype(BF16)

        @pl.when(b + 1 < nb)
        def _():
            for_rows(b + 1, lambda *a: gather_copy(*a).start())

    def for_sub_blocks(fn):
        for sb in range(R // MOE_SUB):
            @pl.when(sb * MOE_SUB < n)
            def _():
                fn(slice(sb * MOE_SUB, (sb + 1) * MOE_SUB))

    @pl.when(n > 0)
    def _():
        wg = wg_ref[...].astype(BF16)
        wu = wu_ref[...].astype(BF16)

        def accumulate(rows):
            x = xb_scr[k, rows, :]
            pg = _dot(x, wg)
            pu = _dot(x, wu)

            @pl.when(k == 0)
            def _():
                g_scr[rows, :] = pg
                u_scr[rows, :] = pu

            @pl.when(k != 0)
            def _():
                g_scr[rows, :] += pg
                u_scr[rows, :] += pu

        for_sub_blocks(accumulate)

    @pl.when(k == MOE_KT - 1)
    def _():
        @pl.when(b > 0)
        def _():
            for_rows(b - 1, lambda *a: scatter_copy(*a).wait())

        @pl.when(n > 0)
        def _():
            wd = wd_ref[...].astype(BF16)

            def down(rows):
                hmid = _silu(g_scr[rows, :]) * u_scr[rows, :]
                y = _dot(hmid.astype(BF16), wd)
                y_scr[rows.start // G:rows.stop // G] = y.reshape(MOE_SUB // G, G, d)

            for_sub_blocks(down)
            for_rows(b, lambda *a: scatter_copy(*a).start())

        @pl.when(b == nb - 1)
        def _():
            for_rows(b, lambda *a: scatter_copy(*a).wait())


def _moe_experts(h2, row_dst, blk_e, blk_n, w_gate, w_up, w_down, layer, n_blocks):
    t, d = h2.shape
    R = MOE_ROWS
    last = MOE_KT - 1

    def w_in_map(b, k, be, bn, dst):
        return (layer, be[b], jnp.where(bn[b] > 0, k, last), 0)

    def w_out_map(b, k, be, bn, dst):
        return (layer, be[b], 0, 0)

    grid_spec = pltpu.PrefetchScalarGridSpec(
        num_scalar_prefetch=3,
        grid=(n_blocks, MOE_KT),
        in_specs=[pl.BlockSpec(memory_space=pl.ANY),
                  pl.BlockSpec((None, None, MOE_KW, MOE_FF), w_in_map),
                  pl.BlockSpec((None, None, MOE_KW, MOE_FF), w_in_map),
                  pl.BlockSpec((None, None, MOE_FF, d), w_out_map)],
        out_specs=pl.BlockSpec(memory_space=pl.ANY),
        scratch_shapes=[pltpu.VMEM((R // MOE_DMA_GROUP, MOE_DMA_GROUP, d), F32),
                        pltpu.VMEM((MOE_KT, R, MOE_KW), BF16),
                        pltpu.VMEM((R, MOE_FF), F32), pltpu.VMEM((R, MOE_FF), F32),
                        pltpu.VMEM((R // MOE_DMA_GROUP, MOE_DMA_GROUP, d), F32),
                        pltpu.VMEM((MOE_DMA_GROUP, d), F32),
                        pltpu.SemaphoreType.DMA(()), pltpu.SemaphoreType.DMA(())],
    )
    return pl.pallas_call(
        _moe_kernel,
        grid_spec=grid_spec,
        out_shape=jax.ShapeDtypeStruct((2, t + MOE_DMA_GROUP, d), F32),
        compiler_params=_cparams(("arbitrary", "arbitrary")),
        name="moe_experts",
    )(blk_e, blk_n, row_dst, h2, w_gate, w_up, w_down)


def _combine_kernel(y0_ref, y1_ref, x_ref, gate_ref, g_ref, *out_refs):
    gate = gate_ref[...]
    x = x_ref[...] + gate[:, 0:1] * y0_ref[...] + gate[:, 1:2] * y1_ref[...]
    if len(out_refs) == 2:
        out_refs[0][...] = x
    ms = jnp.mean(x * x, axis=-1, keepdims=True)
    out_refs[-1][...] = x * lax.rsqrt(ms + EPS) * g_ref[...]


def _moe_combine(yk, x2, gate, next_norm, want_stream):
    t, d = x2.shape
    tm = min(256, t)
    n_out = 2 if want_stream else 1
    return pl.pallas_call(
        _combine_kernel,
        grid=(t // tm,),
        in_specs=[pl.BlockSpec((None, tm, d), lambda i: (0, i, 0)),
                  pl.BlockSpec((None, tm, d), lambda i: (1, i, 0)),
                  pl.BlockSpec((tm, d), lambda i: (i, 0)),
                  pl.BlockSpec((tm, LANES), lambda i: (i, 0)),
                  pl.BlockSpec((1, d), lambda i: (0, 0))],
        out_specs=[pl.BlockSpec((tm, d), lambda i: (i, 0))] * n_out,
        out_shape=[jax.ShapeDtypeStruct((t, d), F32)] * n_out,
        compiler_params=_cparams(("parallel",)),
        name="moe_combine",
    )(yk, yk, x2, gate, next_norm)


def _hier_moe(x2, ffn_norm, w_rg, b_rg, w_re, b_re, w_gate, w_up, w_down, layer, next_norm, want_stream=True):
    t, d = x2.shape
    pad = LANES - MOE_EXPERTS - MOE_GROUPS
    w_r = jnp.concatenate([w_re, w_rg, jnp.zeros((d, pad), F32)], axis=1)
    b_r = jnp.concatenate([b_re, b_rg, jnp.zeros((pad,), F32)]).reshape(1, LANES)
    h2, ids, gate = _router(x2, ffn_norm.reshape(1, d), w_r, b_r)
    n_blocks = (2 * t) // MOE_ROWS + MOE_EXPERTS
    row_dst, blk_e, blk_n = _moe_plan(ids[:, :2], n_blocks)
    yk = _moe_experts(h2, row_dst, blk_e, blk_n, w_gate, w_up, w_down, layer, n_blocks)
    return _moe_combine(yk, x2, gate, next_norm.reshape(1, d), want_stream)


def _s5_disc_kernel(lre_ref, lim_ref, ls_ref, bre_ref, bim_ref, are_ref, aim_ref, bbre_ref, bbim_ref):
    lr = jnp.minimum(lre_ref[...], -1e-4)
    li = lim_ref[...]
    step = jnp.exp(ls_ref[...])
    mag = jnp.exp(lr * step)
    ang = li * step
    ab_re = mag * jnp.cos(ang)
    ab_im = mag * jnp.sin(ang)
    den = lr * lr + li * li
    zr = ab_re - 1.0
    f_re = (zr * lr + ab_im * li) / den
    f_im = (ab_im * lr - zr * li) / den
    br = bre_ref[...]
    bi = bim_ref[...]
    are_ref[...] = ab_re
    aim_ref[...] = ab_im
    bbre_ref[...] = f_re * br - f_im * bi
    bbim_ref[...] = f_re * bi + f_im * br


def _s5_discretise(lam_re, lam_im, log_step, b_re, b_im):
    g, p = lam_re.shape
    n = g * p
    col = lambda a: a.reshape(n, 1)
    ls = jnp.broadcast_to(log_step[:, None], (g, p))
    rows = 1024
    full = lambda w: pl.BlockSpec((rows, w), lambda i: (i, 0))
    return pl.pallas_call(
        _s5_disc_kernel,
        grid=(n // rows,),
        in_specs=[full(1), full(1), full(1), full(SSM_GROUP), full(SSM_GROUP)],
        out_specs=[full(1), full(1), full(SSM_GROUP), full(SSM_GROUP)],
        out_shape=[jax.ShapeDtypeStruct((n, 1), F32), jax.ShapeDtypeStruct((n, 1), F32),
                   jax.ShapeDtypeStruct((n, SSM_GROUP), F32), jax.ShapeDtypeStruct((n, SSM_GROUP), F32)],
        compiler_params=_cparams(("parallel",)),
        name="s5_discretise",
    )(col(lam_re), col(lam_im), col(ls), b_re.reshape(n, SSM_GROUP), b_im.reshape(n, SSM_GROUP))


def _cmul(ar, ai, br, bi):
    return ar * br - ai * bi, ar * bi + ai * br


def _s5_kernel(u_ref, bw_ref, cw_ref, a_ref, d_ref, y_ref, bu_scr, xb_scr, carry_scr, il_scr):
    tau = pl.program_id(2)
    Lb = u_ref.shape[0]
    seg = Lb // SUBLANES
    ns = S5_NSTATE
    ar = a_ref[0:1, :]
    ai = a_ref[1:2, :]

    @pl.when(tau == 0)
    def _():
        carry_scr[...] = jnp.zeros_like(carry_scr)

    pitch = seg + SUBLANES
    for j in range(SUBLANES):
        il_scr[j * pitch:j * pitch + seg, :] = u_ref[j * seg:(j + 1) * seg, :]
    u_perm = jnp.concatenate([il_scr[pl.ds(i, SUBLANES, stride=pitch), :] for i in range(seg)], axis=0)
    bu_scr[...] = _dot(u_perm.astype(BF16), bw_ref[...])

    ar8 = jnp.broadcast_to(ar, (SUBLANES, ns))
    ai8 = jnp.broadcast_to(ai, (SUBLANES, ns))

    def step(i, state):
        xr, xi = state
        r0 = pl.multiple_of(i * SUBLANES, SUBLANES)
        return (ar8 * xr - ai8 * xi + bu_scr[pl.ds(r0, SUBLANES), 0:ns],
                ar8 * xi + ai8 * xr + bu_scr[pl.ds(r0, SUBLANES), ns:2 * ns])

    zeros = jnp.zeros((SUBLANES, ns), F32)
    er, ei = lax.fori_loop(0, seg, step, (zeros, zeros), unroll=8)

    alr, ali = ar, ai
    for _ in range(seg.bit_length() - 1):
        alr, ali = _cmul(alr, ali, alr, ali)
    pr = carry_scr[0:1, 0:ns]
    pi = carry_scr[0:1, ns:2 * ns]
    prs, pis = [], []
    for j in range(SUBLANES):
        prs.append(pr)
        pis.append(pi)
        mr, mi = _cmul(alr, ali, pr, pi)
        pr = er[j:j + 1, :] + mr
        pi = ei[j:j + 1, :] + mi
    carry_scr[0:1, 0:ns] = pr
    carry_scr[0:1, ns:2 * ns] = pi
    p_re = jnp.concatenate(prs, axis=0)
    p_im = jnp.concatenate(pis, axis=0)
    pack = 2 * SUBLANES

    def store_body(i2, state):
        x1 = step(2 * i2, state)
        x2 = step(2 * i2 + 1, x1)
        r0 = pl.multiple_of(i2 * pack, pack)
        xb_scr[pl.ds(r0, pack), 0:ns] = jnp.concatenate([x1[0], x2[0]], axis=0).astype(BF16)
        xb_scr[pl.ds(r0, pack), ns:2 * ns] = jnp.concatenate([x1[1], x2[1]], axis=0).astype(BF16)
        return x2

    lax.fori_loop(0, seg // 2, store_body, (p_re, p_im), unroll=4)

    y = _dot(xb_scr[...], cw_ref[...]) + d_ref[...] * u_perm
    for i in range(seg):
        il_scr[pl.ds(i, SUBLANES, stride=pitch), :] = y[i * SUBLANES:(i + 1) * SUBLANES, :]
    for j in range(SUBLANES):
        y_ref[j * seg:(j + 1) * seg, :] = il_scr[j * pitch:j * pitch + seg, :]


def _s5_scan(h3, bw, cw, a_rows, d_skip):
    b, s, w = h3.shape
    Lb = min(S5_BLOCK, s)
    seg = Lb // SUBLANES
    assert seg & (seg - 1) == 0 and seg >= 2, "segment length must be a power of two"
    nb = w // LANES
    ns2 = 2 * S5_NSTATE
    return pl.pallas_call(
        _s5_kernel,
        grid=(b, nb, s // Lb),
        in_specs=[pl.BlockSpec((None, Lb, LANES), lambda i, k, c: (i, c, k)),
                  pl.BlockSpec((None, LANES, ns2), lambda i, k, c: (k, 0, 0)),
                  pl.BlockSpec((None, ns2, LANES), lambda i, k, c: (k, 0, 0)),
                  pl.BlockSpec((None, 2, S5_NSTATE), lambda i, k, c: (k, 0, 0)),
                  pl.BlockSpec((1, LANES), lambda i, k, c: (0, k))],
        out_specs=pl.BlockSpec((None, Lb, LANES), lambda i, k, c: (i, c, k)),
        out_shape=jax.ShapeDtypeStruct((b, s, w), F32),
        scratch_shapes=[pltpu.VMEM((Lb, ns2), F32), pltpu.VMEM((Lb, ns2), BF16),
                        pltpu.VMEM((SUBLANES, ns2), F32),
                        pltpu.VMEM((Lb + SUBLANES * SUBLANES, LANES), F32)],
        compiler_params=_cparams(("parallel", "parallel", "arbitrary")),
        name="s5_scan",
    )(h3, bw, cw, a_rows, d_skip)


def _glu_kernel(y_ref, x_ref, wv_ref, wg_ref, bv_ref, bg_ref, o_ref, a_scr):
    @pl.when(pl.program_id(1) == 0)
    def _():
        a_scr[...] = jax.nn.gelu(y_ref[...]).astype(BF16)

    a = a_scr[...]
    val = _dot(a, wv_ref[...]) + bv_ref[...]
    gate = _dot(a, wg_ref[...]) + bg_ref[...]
    o_ref[...] = x_ref[...] + val * _sigmoid(gate)


def _glu(y2, x2, w_bf, bias):
    t, d = x2.shape
    tm = min(512, t)
    tn = 512
    nj = d // tn
    return pl.pallas_call(
        _glu_kernel,
        grid=(t // tm, nj),
        in_specs=[pl.BlockSpec((tm, d), lambda i, j: (i, 0)),
                  pl.BlockSpec((tm, tn), lambda i, j: (i, j)),
                  pl.BlockSpec((d, tn), lambda i, j: (0, j)),
                  pl.BlockSpec((d, tn), lambda i, j: (0, nj + j)),
                  pl.BlockSpec((1, tn), lambda i, j: (0, j)),
                  pl.BlockSpec((1, tn), lambda i, j: (0, nj + j))],
        out_specs=pl.BlockSpec((tm, tn), lambda i, j: (i, j)),
        out_shape=jax.ShapeDtypeStruct((t, d), F32),
        scratch_shapes=[pltpu.VMEM((tm, d), BF16)],
        compiler_params=_cparams(("parallel", "arbitrary")),
        name="glu",
    )(y2, x2, w_bf, w_bf, bias, bias)


def _block_diag(w):
    nb, gb, r, c = w.shape
    eye = jnp.eye(gb, dtype=w.dtype)
    return (w[:, :, :, None, :] * eye[None, :, None, :, None]).reshape(nb, gb * r, gb * c)


def _mixer_s5(h3, x2, lam_re, lam_im, log_step, b_re, b_im, c_re, c_im, d_skip, w_glu, b_glu):
    b, s, w = h3.shape
    g, p = lam_re.shape
    gb = S5_GROUPS_PER_BLOCK
    nb = g // gb
    a_re, a_im, bb_re, bb_im = _s5_discretise(lam_re, lam_im, log_step, b_re, b_im)
    a_rows = jnp.stack([a_re.reshape(nb, gb * p), a_im.reshape(nb, gb * p)], axis=1)
    bt = lambda m: jnp.swapaxes(m.reshape(nb, gb, p, SSM_GROUP), 2, 3)
    bw = jnp.concatenate([_block_diag(bt(bb_re)), _block_diag(bt(bb_im))], axis=2).astype(BF16)
    ct = lambda m: jnp.swapaxes(m.reshape(nb, gb, SSM_GROUP, p), 2, 3)
    cw = jnp.concatenate([_block_diag(ct(c_re)), -_block_diag(ct(c_im))], axis=1).astype(BF16)
    y = _s5_scan(h3, bw, cw, a_rows, d_skip.reshape(1, w))
    return _glu(y.reshape(b * s, w), x2, w_glu.astype(BF16), b_glu.reshape(1, -1))


def _mixer_ab(x2, bsz, norm_w, w_in, a_i_bias, a_f_bias, a_norm, b_conv, b_a_log, b_dt_bias, b_norm, w_out):
    t, d = x2.shape
    s = t // bsz
    zeros = lambda n: jnp.zeros((n,), F32)
    n_a = 2 * A_HEADS * A_QK_DIM + 2 * A_HEADS * A_V_DIM
    n_b = 4 * B_HEADS * B_HEAD_DIM
    a_end = n_a + 2 * A_HEADS
    w_gates = jnp.concatenate([w_in[:, n_a:a_end], w_in[:, a_end + n_b:],
                               jnp.zeros((d, LANES - G_END), F32)], axis=1).astype(BF16)
    w_wide = jnp.concatenate([w_in[:, :n_a], w_in[:, a_end:a_end + n_b]], axis=1).astype(BF16)
    proj, gates = _inproj(x2, norm_w.reshape(1, d), w_wide, w_gates)
    proj = proj.reshape(bsz, s, N_PROJ)
    gates = gates.reshape(bsz, s, LANES)
    gates_t = jnp.swapaxes(gates[:, :, :GATE_ROWS], 1, 2)
    bias = jnp.concatenate([a_i_bias, a_f_bias, zeros(B_HEADS), b_dt_bias, zeros(LANES - G_END)])
    alog = jnp.concatenate([zeros(G_BA), b_a_log, zeros(LANES - G_END)])
    pc, pr = _gates(gates, gates_t, bias.reshape(1, LANES), bias[:GATE_ROWS].reshape(GATE_ROWS, 1),
                    alog.reshape(1, LANES), alog[:GATE_ROWS].reshape(GATE_ROWS, 1))
    ya = _mlstm(proj, pc, pr, a_norm.reshape(1, -1))
    yb = _gdn(proj, pc, pr, b_conv, b_norm.reshape(1, -1))
    n_ya = A_HEADS * A_V_DIM
    w_out_bf = w_out.astype(BF16)
    return _outproj(ya.reshape(t, n_ya), yb.reshape(t, -1), x2, w_out_bf[:n_ya], w_out_bf[n_ya:])


def kernel(x, mix_norm, ab_w_in, mlstm_i_bias, mlstm_f_bias, mlstm_norm, gdn_conv, gdn_a_log, gdn_dt_bias, gdn_norm, ab_w_out, ssm_lambda_re, ssm_lambda_im, ssm_log_step, ssm_b_re, ssm_b_im, ssm_c_re, ssm_c_im, ssm_d, glu_w, glu_b, ffn_norm, router_group_w, router_group_b, router_expert_w, router_expert_b, expert_w_gate, expert_w_up, expert_w_down, final_norm):
    bsz, s, d = x.shape
    t = bsz * s
    x2 = x.reshape(t, d)
    x2 = _mixer_ab(x2, bsz, mix_norm[0], ab_w_in[0], mlstm_i_bias[0], mlstm_f_bias[0], mlstm_norm[0],
                   gdn_conv[0], gdn_a_log[0], gdn_dt_bias[0], gdn_norm[0], ab_w_out[0])
    x2, h3 = _hier_moe(x2, ffn_norm[0], router_group_w[0], router_group_b[0], router_expert_w[0],
                       router_expert_b[0], expert_w_gate, expert_w_up, expert_w_down, 0, mix_norm[1])
    x2 = _mixer_s5(h3.reshape(bsz, s, d), x2, ssm_lambda_re[0], ssm_lambda_im[0], ssm_log_step[0],
                   ssm_b_re[0], ssm_b_im[0], ssm_c_re[0], ssm_c_im[0], ssm_d[0], glu_w[0], glu_b[0])
    out, = _hier_moe(x2, ffn_norm[1], router_group_w[1], router_group_b[1], router_expert_w[1],
                     router_expert_b[1], expert_w_gate, expert_w_up, expert_w_down, 1, final_norm, want_stream=False)
    return out.reshape(bsz, s, d)
```

```python
import functools

import jax
import jax.numpy as jnp
from jax import lax
from jax.experimental import pallas as pl
from jax.experimental.pallas import tpu as pltpu

F32 = jnp.float32
BF16 = jnp.bfloat16
HIGHEST = lax.Precision.HIGHEST

EPS = 1e-6
D_MODEL = 2048
A_HEADS = 4
A_QK_DIM = 128
A_V_DIM = 256
A_GATE_CAP = 15.0
B_HEADS = 8
B_HEAD_DIM = 128
B_CONV = 4
SSM_GROUP = 16
SSM_STATE = 64
MOE_GROUPS = 8
MOE_PER_GROUP = 8
MOE_EXPERTS = 64
MOE_FF = 768

LANES = 128
SUBLANES = 8
VMEM_LIMIT = 56 * 1024 * 1024

COL_AQ, COL_AK, COL_AV, COL_AO = 0, 512, 1024, 2048
COL_BQ, COL_BK, COL_BV, COL_BZ = 3072, 4096, 5120, 6144
N_PROJ = 7168
G_AI, G_AF, G_BB, G_BA, G_END = 0, 4, 8, 16, 24
GATE_ROWS = 32

MLSTM_CHUNK = 256
GDN_BLOCK = MLSTM_CHUNK
GDN_CHUNK = 64
GDN_HEADS_PER_STEP = 4
S5_BLOCK = 512
S5_GROUPS_PER_BLOCK = 8
S5_BLOCKS_PER_STEP = 2
S5_NSTATE = S5_GROUPS_PER_BLOCK * SSM_STATE
MOE_ROWS = 512
MOE_SUB = 256
MOE_KT = 2
MOE_KW = D_MODEL // MOE_KT
MOE_DMA_GROUP_LOG2 = 4
MOE_DMA_GROUP = 1 << MOE_DMA_GROUP_LOG2


def _cparams(sem):
    return pltpu.CompilerParams(dimension_semantics=sem, vmem_limit_bytes=VMEM_LIMIT)


def _softcap(t, cap):
    return cap * jnp.tanh(t / cap)


def _log_sigmoid(t):
    return jnp.minimum(t, 0.0) - jnp.log(1.0 + jnp.exp(-jnp.abs(t)))


def _softplus(t):
    return jnp.maximum(t, 0.0) + jnp.log(1.0 + jnp.exp(-jnp.abs(t)))


def _sigmoid(t):
    return 1.0 / (1.0 + jnp.exp(-t))


def _silu(t):
    return t * _sigmoid(t)


def _pick_col(x, idx):
    lane = lax.broadcasted_iota(jnp.int32, x.shape, 1)
    return jnp.sum(jnp.where(lane == idx, x, 0.0), axis=-1, keepdims=True)


def _dot(a, b):
    return jnp.dot(a, b, preferred_element_type=F32)


def _dot_hi(a, b):
    return jnp.dot(a, b, precision=HIGHEST, preferred_element_type=F32)


def _inproj_kernel(x_ref, g_ref, w_ref, wg_ref, o_ref, og_ref, h_scr):
    @pl.when(pl.program_id(1) == 0)
    def _():
        x = x_ref[...]
        ms = jnp.mean(x * x, axis=-1, keepdims=True)
        h_scr[...] = (x * lax.rsqrt(ms + EPS) * g_ref[...]).astype(BF16)
        og_ref[...] = _dot(h_scr[...], wg_ref[...])

    o_ref[...] = _dot(h_scr[...], w_ref[...])


def _inproj(x2, g, w_bf, wg_bf):
    t, d = x2.shape
    n = w_bf.shape[1]
    tm = min(1024, t)
    tn = 1024
    return pl.pallas_call(
        _inproj_kernel,
        grid=(t // tm, n // tn),
        in_specs=[pl.BlockSpec((tm, d), lambda i, j: (i, 0)),
                  pl.BlockSpec((1, d), lambda i, j: (0, 0)),
                  pl.BlockSpec((d, tn), lambda i, j: (0, j)),
                  pl.BlockSpec((d, LANES), lambda i, j: (0, 0))],
        out_specs=[pl.BlockSpec((tm, tn), lambda i, j: (i, j)),
                   pl.BlockSpec((tm, LANES), lambda i, j: (i, 0))],
        out_shape=[jax.ShapeDtypeStruct((t, n), F32), jax.ShapeDtypeStruct((t, LANES), F32)],
        scratch_shapes=[pltpu.VMEM((tm, d), BF16)],
        compiler_params=_cparams(("parallel", "arbitrary")),
        name="inproj",
    )(x2, g, w_bf, wg_bf)


def _gate_values(g, alog, idx, cum_a, cum_b):
    sc = _softcap(g, A_GATE_CAP)
    cum_logf = cum_a(_log_sigmoid(sc))
    cum_g = cum_b(-jnp.exp(alog) * _softplus(g))
    return jnp.where(idx < G_AF, sc,
                     jnp.where(idx < G_BB, cum_logf,
                               jnp.where(idx < G_BA, _sigmoid(g), jnp.where(idx < G_END, cum_g, 0.0))))


def _gates_kernel(gc_ref, gr_ref, bc_ref, br_ref, alc_ref, alr_ref, pc_ref, pr_ref):
    L = gc_ref.shape[0]
    row = lax.broadcasted_iota(jnp.int32, (L, L), 0)
    col = lax.broadcasted_iota(jnp.int32, (L, L), 1)
    same = (row // GDN_CHUNK) == (col // GDN_CHUNK)
    tril = (row >= col).astype(F32)
    triu = (row <= col).astype(F32)
    blk_tril = (same & (row >= col)).astype(F32)
    blk_triu = (same & (row <= col)).astype(F32)
    lane = lax.broadcasted_iota(jnp.int32, (L, LANES), 1)
    pc_ref[...] = _gate_values(gc_ref[...] + bc_ref[...], alc_ref[...], lane,
                               lambda v: _dot_hi(tril, v), lambda v: _dot_hi(blk_tril, v))
    sub = lax.broadcasted_iota(jnp.int32, (GATE_ROWS, L), 0)
    pr_ref[...] = _gate_values(gr_ref[...] + br_ref[...], alr_ref[...], sub,
                               lambda v: _dot_hi(v, triu), lambda v: _dot_hi(v, blk_triu))


def _gates(gates, gates_t, bias_col, bias_row, alog_col, alog_row):
    b, s, _ = gates.shape
    L = min(MLSTM_CHUNK, s)
    return pl.pallas_call(
        _gates_kernel,
        grid=(b, s // L),
        in_specs=[pl.BlockSpec((None, L, LANES), lambda i, c: (i, c, 0)),
                  pl.BlockSpec((None, GATE_ROWS, L), lambda i, c: (i, 0, c)),
                  pl.BlockSpec((1, LANES), lambda i, c: (0, 0)),
                  pl.BlockSpec((GATE_ROWS, 1), lambda i, c: (0, 0)),
                  pl.BlockSpec((1, LANES), lambda i, c: (0, 0)),
                  pl.BlockSpec((GATE_ROWS, 1), lambda i, c: (0, 0))],
        out_specs=[pl.BlockSpec((None, L, LANES), lambda i, c: (i, c, 0)),
                   pl.BlockSpec((None, GATE_ROWS, L), lambda i, c: (i, 0, c))],
        out_shape=[jax.ShapeDtypeStruct((b, s, LANES), F32),
                   jax.ShapeDtypeStruct((b, GATE_ROWS, s), F32)],
        compiler_params=_cparams(("parallel", "parallel")),
        name="gates",
    )(gates, gates_t, bias_col, bias_row, alog_col, alog_row)


def _mlstm_kernel(q_ref, k_ref, v_ref, o_ref, pc_ref, pr_ref, nw_ref, out_ref, c_scr, n_scr, m_scr):
    c = pl.program_id(1)
    L = q_ref.shape[0]
    nh, dk, dv = A_HEADS, A_QK_DIM, A_V_DIM

    @pl.when(c == 0)
    def _():
        c_scr[...] = jnp.zeros_like(c_scr)
        n_scr[...] = jnp.zeros_like(n_scr)
        m_scr[...] = jnp.zeros_like(m_scr)

    row = lax.broadcasted_iota(jnp.int32, (L, L), 0)
    col = lax.broadcasted_iota(jnp.int32, (L, L), 1)
    causal = row >= col
    per_head = lambda fn: jnp.stack([fn(h) for h in range(nh)], axis=0)
    bmm = lambda a, b: jnp.einsum("bij,bjk->bik", a.astype(BF16), b.astype(BF16), preferred_element_type=F32)
    bmm_nt = lambda a, b: jnp.einsum("bik,bjk->bij", a.astype(BF16), b.astype(BF16), preferred_element_type=F32)

    pc = pc_ref[...]
    i_col = per_head(lambda h: _pick_col(pc, G_AI + h))
    b_col = per_head(lambda h: _pick_col(pc, G_AF + h))
    i_row = per_head(lambda h: pr_ref[G_AI + h:G_AI + h + 1, :])
    b_row = per_head(lambda h: pr_ref[G_AF + h:G_AF + h + 1, :])
    b_last = b_col[:, L - 1:L, :]

    m_prev = m_scr[...]
    log_d = jnp.where(causal, b_col - b_row + i_row, -jnp.inf)
    log_inter = b_col + m_prev
    m_t = jnp.maximum(log_inter, jnp.max(log_d, axis=-1, keepdims=True))
    dmat = jnp.exp(log_d - m_t)
    inter = jnp.exp(log_inter - m_t)

    q = per_head(lambda h: q_ref[:, h * dk:(h + 1) * dk])
    k = per_head(lambda h: k_ref[:, h * dk:(h + 1) * dk]) * (dk ** -0.5)
    vb = per_head(lambda h: v_ref[:, h * dv:(h + 1) * dv]).astype(BF16)
    scores = bmm_nt(q, k) * dmat
    c_mat = c_scr[...]
    n_vec = n_scr[...]
    num = inter * bmm(q, c_mat) + bmm(scores, vb)
    den = inter * jnp.sum(q * n_vec, axis=-1, keepdims=True) + jnp.sum(scores, axis=-1, keepdims=True)
    hh = num / jnp.maximum(jnp.abs(den), jnp.exp(-m_t))
    hh = hh * lax.rsqrt(jnp.mean(hh * hh, axis=-1, keepdims=True) + EPS)
    for h in range(nh):
        cols = slice(h * dv, (h + 1) * dv)
        out_ref[:, cols] = hh[h] * nw_ref[:, cols] * _sigmoid(o_ref[:, cols])

    le_col = b_last - b_col + i_col
    m_new = jnp.maximum(b_last + m_prev, jnp.max(le_col, axis=1, keepdims=True))
    carry_scale = jnp.exp(b_last + m_prev - m_new)
    kw = k * jnp.exp(le_col - m_new)
    c_scr[...] = c_mat * carry_scale + bmm(jnp.swapaxes(kw, 1, 2), vb)
    n_scr[...] = n_vec * carry_scale + jnp.sum(kw, axis=1, keepdims=True)
    m_scr[...] = m_new


def _mlstm(proj, pc, pr, norm_w):
    b, s, _ = proj.shape
    L = min(MLSTM_CHUNK, s)
    wqk, wv = A_HEADS * A_QK_DIM, A_HEADS * A_V_DIM
    return pl.pallas_call(
        _mlstm_kernel,
        grid=(b, s // L),
        in_specs=[pl.BlockSpec((None, L, wqk), lambda i, c: (i, c, COL_AQ // wqk)),
                  pl.BlockSpec((None, L, wqk), lambda i, c: (i, c, COL_AK // wqk)),
                  pl.BlockSpec((None, L, wv), lambda i, c: (i, c, COL_AV // wv)),
                  pl.BlockSpec((None, L, wv), lambda i, c: (i, c, COL_AO // wv)),
                  pl.BlockSpec((None, L, LANES), lambda i, c: (i, c, 0)),
                  pl.BlockSpec((None, GATE_ROWS, L), lambda i, c: (i, 0, c)),
                  pl.BlockSpec((1, wv), lambda i, c: (0, 0))],
        out_specs=pl.BlockSpec((None, L, wv), lambda i, c: (i, c, 0)),
        out_shape=jax.ShapeDtypeStruct((b, s, wv), F32),
        scratch_shapes=[pltpu.VMEM((A_HEADS, A_QK_DIM, A_V_DIM), F32),
                        pltpu.VMEM((A_HEADS, 1, A_QK_DIM), F32),
                        pltpu.VMEM((A_HEADS, 1, 1), F32)],
        compiler_params=_cparams(("parallel", "arbitrary")),
        name="mlstm",
    )(proj, proj, proj, proj, pc, pr, norm_w)


def _causal_conv_silu(x, tail, w):
    row8 = lax.broadcasted_iota(jnp.int32, (SUBLANES, x.shape[1]), 0)
    acc = x * w[B_CONV - 1:B_CONV, :]
    for d in range(1, B_CONV):
        rolled = pltpu.roll(x, d, 0)
        head = jnp.where(row8 < d, pltpu.roll(tail, d, 0), rolled[0:SUBLANES, :])
        shifted = jnp.concatenate([head, rolled[SUBLANES:, :]], axis=0)
        acc = acc + shifted * w[B_CONV - 1 - d:B_CONV - d, :]
    return _silu(acc)


def _l2norm(t):
    return t * lax.rsqrt(jnp.sum(t * t, axis=-1, keepdims=True) + EPS)


def _gdn_kernel(q_ref, k_ref, v_ref, z_ref, pc_ref, pr_ref, wq_ref, wk_ref, wv_ref, nw_ref, out_ref,
                s_scr, tq_scr, tk_scr, tv_scr):
    hp = pl.program_id(1)
    c = pl.program_id(2)
    Lb = q_ref.shape[0]
    hd = B_HEAD_DIM

    @pl.when(c == 0)
    def _():
        s_scr[...] = jnp.zeros_like(s_scr)
        tq_scr[...] = jnp.zeros_like(tq_scr)
        tk_scr[...] = jnp.zeros_like(tk_scr)
        tv_scr[...] = jnp.zeros_like(tv_scr)

    xq, xk, xv = q_ref[...], k_ref[...], v_ref[...]
    cq = _causal_conv_silu(xq, tq_scr[...], wq_ref[...])
    ck = _causal_conv_silu(xk, tk_scr[...], wk_ref[...])
    cv = _causal_conv_silu(xv, tv_scr[...], wv_ref[...])
    tq_scr[...] = xq[Lb - SUBLANES:, :]
    tk_scr[...] = xk[Lb - SUBLANES:, :]
    tv_scr[...] = xv[Lb - SUBLANES:, :]
    pc = pc_ref[...]
    nw = nw_ref[...]
    heads = [hp * GDN_HEADS_PER_STEP + hh for hh in range(GDN_HEADS_PER_STEP)]
    per_head = lambda x: jnp.stack([x[:, hh * hd:(hh + 1) * hd] for hh in range(GDN_HEADS_PER_STEP)], axis=0)
    hb, s_scr[...] = _gdn_heads(
        _l2norm(per_head(cq)) * (hd ** -0.5), _l2norm(per_head(ck)), per_head(cv),
        jnp.stack([_pick_col(pc, G_BB + h) for h in heads], axis=0),
        jnp.stack([_pick_col(pc, G_BA + h) for h in heads], axis=0),
        jnp.stack([pr_ref[pl.ds(G_BA + h, 1), :] for h in heads], axis=0), s_scr[...])
    hb = hb * lax.rsqrt(jnp.mean(hb * hb, axis=-1, keepdims=True) + EPS) * nw
    for hh in range(GDN_HEADS_PER_STEP):
        cols = slice(hh * hd, (hh + 1) * hd)
        out_ref[:, cols] = hb[hh] * _silu(z_ref[:, cols])


def _gdn_heads(q, k, v, beta, dec_col, dec_row, state):
    nh, Lb, _ = q.shape
    C = GDN_CHUNK
    nsub = Lb // C
    bmm = lambda a, b: jnp.einsum("bij,bjk->bik", a.astype(BF16), b.astype(BF16), preferred_element_type=F32)
    bmm_nt = lambda a, b: jnp.einsum("bik,bjk->bij", a.astype(BF16), b.astype(BF16), preferred_element_type=F32)
    r64 = lax.broadcasted_iota(jnp.int32, (C, C), 0)
    c64 = lax.broadcasted_iota(jnp.int32, (C, C), 1)
    causal = r64 >= c64
    strict = r64 > c64
    eye = (r64 == c64).astype(F32)

    kbeta = k * beta
    edec = jnp.exp(dec_col)
    q_dec = (q * edec).astype(BF16)
    vbeta = (v * beta).astype(BF16)
    kbdec = (kbeta * edec).astype(BF16)
    kb16 = k.astype(BF16)
    qb16 = q.astype(BF16)
    kbeta16 = kbeta.astype(BF16)

    segs, nmats = [], []
    for j in range(nsub):
        sl = slice(j * C, (j + 1) * C)
        seg = jnp.exp(jnp.where(causal, dec_col[:, sl, :] - dec_row[:, :, sl], -jnp.inf))
        a_low = jnp.where(strict, bmm_nt(kbeta16[:, sl, :], kb16[:, sl, :]) * seg, 0.0)
        segs.append(seg)
        nmats.append(-a_low)
    nmat = jnp.concatenate(nmats, axis=0)
    tmat = eye[None] + nmat
    npow = bmm(nmat, nmat)
    for _ in range(4):
        tmat, npow = tmat + bmm(tmat, npow), bmm(npow, npow)
    tmat = tmat + bmm(tmat, npow)

    outs = []
    for j in range(nsub):
        sl = slice(j * C, (j + 1) * C)
        t16 = tmat[j * nh:(j + 1) * nh].astype(BF16)
        u = bmm(t16, vbeta[:, sl, :])
        w = bmm(t16, kbdec[:, sl, :])
        attn = bmm_nt(qb16[:, sl, :], kb16[:, sl, :]) * segs[j]
        s16 = state.astype(BF16)
        v_new = u - bmm(w, s16)
        o = bmm(q_dec[:, sl, :], s16) + bmm(attn, v_new)
        d_last = dec_col[:, (j + 1) * C - 1:(j + 1) * C, :]
        k_end = k[:, sl, :] * jnp.exp(d_last - dec_col[:, sl, :])
        state = state * jnp.exp(d_last) + bmm(jnp.swapaxes(k_end, 1, 2), v_new)
        outs.append(o)
    return jnp.concatenate(outs, axis=1), state


def _gdn(proj, pc, pr, conv_w, norm_w):
    b, s, _ = proj.shape
    Lb = min(GDN_BLOCK, s)
    hd = B_HEAD_DIM
    wd = GDN_HEADS_PER_STEP * hd
    npair = B_HEADS // GDN_HEADS_PER_STEP
    qb, kb, vb, zb = COL_BQ // wd, COL_BK // wd, COL_BV // wd, COL_BZ // wd
    blk = lambda off: pl.BlockSpec((None, Lb, wd), lambda i, h, c: (i, c, off + h))
    return pl.pallas_call(
        _gdn_kernel,
        grid=(b, npair, s // Lb),
        in_specs=[blk(qb), blk(kb), blk(vb), blk(zb),
                  pl.BlockSpec((None, Lb, LANES), lambda i, h, c: (i, c, 0)),
                  pl.BlockSpec((None, GATE_ROWS, Lb), lambda i, h, c: (i, 0, c)),
                  pl.BlockSpec((B_CONV, wd), lambda i, h, c: (0, h)),
                  pl.BlockSpec((B_CONV, wd), lambda i, h, c: (0, npair + h)),
                  pl.BlockSpec((B_CONV, wd), lambda i, h, c: (0, 2 * npair + h)),
                  pl.BlockSpec((1, hd), lambda i, h, c: (0, 0))],
        out_specs=pl.BlockSpec((None, Lb, wd), lambda i, h, c: (i, c, h)),
        out_shape=jax.ShapeDtypeStruct((b, s, B_HEADS * hd), F32),
        scratch_shapes=[pltpu.VMEM((GDN_HEADS_PER_STEP, hd, hd), F32),
                        pltpu.VMEM((SUBLANES, wd), F32),
                        pltpu.VMEM((SUBLANES, wd), F32),
                        pltpu.VMEM((SUBLANES, wd), F32)],
        compiler_params=_cparams(("parallel", "parallel", "arbitrary")),
        name="gdn",
    )(proj, proj, proj, proj, pc, pr, conv_w, conv_w, conv_w, norm_w)


def _outproj_kernel(ya_ref, yb_ref, x_ref, wa_ref, wb_ref, o_ref):
    acc = _dot(ya_ref[...].astype(BF16), wa_ref[...]) + _dot(yb_ref[...].astype(BF16), wb_ref[...])
    o_ref[...] = x_ref[...] + acc


def _outproj(ya, yb, x2, w):
    t, d = x2.shape
    ka, kb = ya.shape[1], yb.shape[1]
    assert ka == kb and w.shape == (ka + kb, d)
    tm = min(512, t)
    return pl.pallas_call(
        _outproj_kernel,
        grid=(t // tm,),
        in_specs=[pl.BlockSpec((tm, ka), lambda i: (i, 0)),
                  pl.BlockSpec((tm, kb), lambda i: (i, 0)),
                  pl.BlockSpec((tm, d), lambda i: (i, 0)),
                  pl.BlockSpec((ka, d), lambda i: (0, 0)),
                  pl.BlockSpec((kb, d), lambda i: (1, 0))],
        out_specs=pl.BlockSpec((tm, d), lambda i: (i, 0)),
        out_shape=jax.ShapeDtypeStruct((t, d), F32),
        compiler_params=_cparams(("parallel",)),
        name="outproj",
    )(ya, yb, x2, w, w)


def _router_kernel(x_ref, g_ref, whi_ref, wlo_ref, b_ref, h_ref, ids_ref, gate_ref):
    x = x_ref[...]
    ms = jnp.mean(x * x, axis=-1, keepdims=True)
    h = x * lax.rsqrt(ms + EPS) * g_ref[...]
    h_ref[...] = h
    h_hi = h.astype(BF16)
    h_lo = (h - h_hi.astype(F32)).astype(BF16)
    w_hi = whi_ref[...]
    logits = _dot(h_hi, w_hi) + _dot(h_lo, w_hi) + _dot(h_hi, wlo_ref[...]) + b_ref[...]
    lane = lax.broadcasted_iota(jnp.int32, logits.shape, 1)
    neg = -jnp.inf
    big = jnp.int32(1 << 20)
    is_g = (lane >= MOE_EXPERTS) & (lane < MOE_EXPERTS + MOE_GROUPS)
    gl = jnp.where(is_g, logits, neg)
    gmax = jnp.max(gl, axis=-1, keepdims=True)
    g_lane = jnp.min(jnp.where(gl == gmax, lane, big), axis=-1, keepdims=True)
    g_idx = g_lane - MOE_EXPERTS
    g_w = 1.0 / jnp.sum(jnp.exp(gl - gmax), axis=-1, keepdims=True)
    in_grp = (lane >= g_idx * MOE_PER_GROUP) & (lane < (g_idx + 1) * MOE_PER_GROUP)
    el = jnp.where(in_grp, logits, neg)
    e0 = jnp.max(el, axis=-1, keepdims=True)
    l0 = jnp.min(jnp.where(el == e0, lane, big), axis=-1, keepdims=True)
    el1 = jnp.where(lane == l0, neg, el)
    e1 = jnp.max(el1, axis=-1, keepdims=True)
    l1 = jnp.min(jnp.where(el1 == e1, lane, big), axis=-1, keepdims=True)
    r = jnp.exp(e1 - e0)
    p0 = 1.0 / (1.0 + r)
    p1 = r / (1.0 + r)
    ids_ref[...] = jnp.where(lane == 0, l0, jnp.where(lane == 1, l1, 0))
    gate_ref[...] = jnp.where(lane == 0, g_w * p0, jnp.where(lane == 1, g_w * p1, 0.0))


def _router(x2, g, w, bias):
    t, d = x2.shape
    tm = min(256, t)
    w_hi = w.astype(BF16)
    w_lo = (w - w_hi.astype(F32)).astype(BF16)
    return pl.pallas_call(
        _router_kernel,
        grid=(t // tm,),
        in_specs=[pl.BlockSpec((tm, d), lambda i: (i, 0)),
                  pl.BlockSpec((1, d), lambda i: (0, 0)),
                  pl.BlockSpec((d, LANES), lambda i: (0, 0)),
                  pl.BlockSpec((d, LANES), lambda i: (0, 0)),
                  pl.BlockSpec((1, LANES), lambda i: (0, 0))],
        out_specs=[pl.BlockSpec((tm, d), lambda i: (i, 0)),
                   pl.BlockSpec((tm, LANES), lambda i: (i, 0)),
                   pl.BlockSpec((tm, LANES), lambda i: (i, 0))],
        out_shape=[jax.ShapeDtypeStruct((t, d), F32),
                   jax.ShapeDtypeStruct((t, LANES), jnp.int32),
                   jax.ShapeDtypeStruct((t, LANES), F32)],
        compiler_params=_cparams(("parallel",)),
        name="router",
    )(x2, g, w_hi, w_lo, bias)


def _moe_plan(expert, n_blocks):
    t = expert.shape[0]
    n_assign = 2 * t
    flat_e = expert.reshape(n_assign)
    onehot = (flat_e[:, None] == jnp.arange(MOE_EXPERTS, dtype=jnp.int32)[None, :]).astype(jnp.int32)
    csum = jnp.cumsum(onehot, axis=0)
    rank = jnp.sum(onehot * csum, axis=1) - 1
    counts = csum[-1]
    nblk = (counts + MOE_ROWS - 1) // MOE_ROWS
    blk_end = jnp.cumsum(nblk)
    blk_start = blk_end - nblk
    dest = jnp.sum(onehot * blk_start[None, :], axis=1) * MOE_ROWS + rank
    spare = 2 * (t + jnp.arange(n_blocks * MOE_ROWS, dtype=jnp.int32) % MOE_DMA_GROUP)
    row_dst = spare.at[dest].set(jnp.arange(n_assign, dtype=jnp.int32))
    bid = jnp.arange(n_blocks, dtype=jnp.int32)
    total = blk_end[-1]
    be = jnp.minimum(jnp.sum((bid[:, None] >= blk_end[None, :]).astype(jnp.int32), axis=1), MOE_EXPERTS - 1)
    used = bid < total
    blk_n = jnp.where(used, jnp.clip(counts[be] - (bid - blk_start[be]) * MOE_ROWS, 0, MOE_ROWS), 0)
    blk_e = jnp.where(used, be, be[jnp.maximum(total - 1, 0)])
    return row_dst, blk_e.astype(jnp.int32), blk_n.astype(jnp.int32)


def _moe_kernel(blk_e_ref, blk_n_ref, dst_ref, h_hbm, wg_ref, wu_ref, wd_ref, yk_hbm,
                xf_scr, xb_scr, g_scr, u_scr, y_scr, z_scr, gsem, ssem):
    b = pl.program_id(0)
    k = pl.program_id(1)
    nb = pl.num_programs(0)
    n = blk_n_ref[b]
    R = MOE_ROWS
    G = MOE_DMA_GROUP
    t, d = h_hbm.shape


    def gather_copy(blk, g, j):
        tok = jnp.minimum(lax.shift_right_logical(dst_ref[blk * R + g * G + j], 1), t - 1)
        return pltpu.make_async_copy(h_hbm.at[pl.ds(tok, 1), :], xf_scr.at[g, pl.ds(j, 1), :], gsem)

    def scatter_copy(blk, g, j):
        dst = dst_ref[blk * R + g * G + j]
        tok = lax.shift_right_logical(dst, 1)
        return pltpu.make_async_copy(y_scr.at[g, pl.ds(j, 1), :], yk_hbm.at[dst & 1, pl.ds(tok, 1), :], ssem)

    def for_rows(blk, fn):
        def body(g, carry):
            for j in range(G):
                fn(blk, g, j)
            return carry
        lax.fori_loop(0, lax.shift_right_logical(blk_n_ref[blk] + (G - 1), MOE_DMA_GROUP_LOG2), body, 0)

    @pl.when((b == 0) & (k == 0))
    def _():
        xf_scr[...] = jnp.zeros_like(xf_scr)
        for_rows(0, lambda *a: gather_copy(*a).start())
        z_scr[...] = jnp.zeros_like(z_scr)
        spare = [pltpu.make_async_copy(z_scr, yk_hbm.at[s, pl.ds(t, MOE_DMA_GROUP), :], ssem) for s in range(2)]
        for cp in spare:
            cp.start()
        for cp in spare:
            cp.wait()

    @pl.when(k == 0)
    def _():
        for_rows(b, lambda *a: gather_copy(*a).wait())
        for kk in range(MOE_KT):
            xb_scr[kk] = xf_scr[:, :, kk * MOE_KW:(kk + 1) * MOE_KW].reshape(R, MOE_KW).astype(BF16)

        @pl.when(b + 1 < nb)
        def _():
            for_rows(b + 1, lambda *a: gather_copy(*a).start())

    def for_sub_blocks(fn):
        for sb in range(R // MOE_SUB):
            @pl.when(sb * MOE_SUB < n)
            def _():
                fn(slice(sb * MOE_SUB, (sb + 1) * MOE_SUB))

    @pl.when(n > 0)
    def _():
        wg = wg_ref[...].astype(BF16)
        wu = wu_ref[...].astype(BF16)

        def accumulate(rows):
            x = xb_scr[k, rows, :]
            pg = _dot(x, wg)
            pu = _dot(x, wu)

            @pl.when(k == 0)
            def _():
                g_scr[rows, :] = pg
                u_scr[rows, :] = pu

            @pl.when(k != 0)
            def _():
                g_scr[rows, :] += pg
                u_scr[rows, :] += pu

        for_sub_blocks(accumulate)

    @pl.when(k == MOE_KT - 1)
    def _():
        @pl.when(b > 0)
        def _():
            for_rows(b - 1, lambda *a: scatter_copy(*a).wait())

        @pl.when(n > 0)
        def _():
            wd = wd_ref[...].astype(BF16)

            def down(rows):
                hmid = _silu(g_scr[rows, :]) * u_scr[rows, :]
                y = _dot(hmid.astype(BF16), wd)
                y_scr[rows.start // G:rows.stop // G] = y.reshape(MOE_SUB // G, G, d)

            for_sub_blocks(down)
            for_rows(b, lambda *a: scatter_copy(*a).start())

        @pl.when(b == nb - 1)
        def _():
            for_rows(b, lambda *a: scatter_copy(*a).wait())


def _moe_experts(h2, row_dst, blk_e, blk_n, w_gate, w_up, w_down, layer, n_blocks):
    t, d = h2.shape
    R = MOE_ROWS
    last = MOE_KT - 1

    def w_in_map(b, k, be, bn, dst):
        return (layer, be[b], jnp.where(bn[b] > 0, k, last), 0)

    def w_out_map(b, k, be, bn, dst):
        return (layer, be[b], 0, 0)

    grid_spec = pltpu.PrefetchScalarGridSpec(
        num_scalar_prefetch=3,
        grid=(n_blocks, MOE_KT),
        in_specs=[pl.BlockSpec(memory_space=pl.ANY),
                  pl.BlockSpec((None, None, MOE_KW, MOE_FF), w_in_map),
                  pl.BlockSpec((None, None, MOE_KW, MOE_FF), w_in_map),
                  pl.BlockSpec((None, None, MOE_FF, d), w_out_map)],
        out_specs=pl.BlockSpec(memory_space=pl.ANY),
        scratch_shapes=[pltpu.VMEM((R // MOE_DMA_GROUP, MOE_DMA_GROUP, d), F32),
                        pltpu.VMEM((MOE_KT, R, MOE_KW), BF16),
                        pltpu.VMEM((R, MOE_FF), F32), pltpu.VMEM((R, MOE_FF), F32),
                        pltpu.VMEM((R // MOE_DMA_GROUP, MOE_DMA_GROUP, d), F32),
                        pltpu.VMEM((MOE_DMA_GROUP, d), F32),
                        pltpu.SemaphoreType.DMA(()), pltpu.SemaphoreType.DMA(())],
    )
    return pl.pallas_call(
        _moe_kernel,
        grid_spec=grid_spec,
        out_shape=jax.ShapeDtypeStruct((2, t + MOE_DMA_GROUP, d), F32),
        compiler_params=_cparams(("arbitrary", "arbitrary")),
        name="moe_experts",
    )(blk_e, blk_n, row_dst, h2, w_gate, w_up, w_down)


def _combine_kernel(y0_ref, y1_ref, x_ref, gate_ref, g_ref, *out_refs):
    gate = gate_ref[...]
    x = x_ref[...] + gate[:, 0:1] * y0_ref[...] + gate[:, 1:2] * y1_ref[...]
    if len(out_refs) == 2:
        out_refs[0][...] = x
    ms = jnp.mean(x * x, axis=-1, keepdims=True)
    out_refs[-1][...] = x * lax.rsqrt(ms + EPS) * g_ref[...]


def _moe_combine(yk, x2, gate, next_norm, want_stream):
    t, d = x2.shape
    tm = min(256, t)
    n_out = 2 if want_stream else 1
    return pl.pallas_call(
        _combine_kernel,
        grid=(t // tm,),
        in_specs=[pl.BlockSpec((None, tm, d), lambda i: (0, i, 0)),
                  pl.BlockSpec((None, tm, d), lambda i: (1, i, 0)),
                  pl.BlockSpec((tm, d), lambda i: (i, 0)),
                  pl.BlockSpec((tm, LANES), lambda i: (i, 0)),
                  pl.BlockSpec((1, d), lambda i: (0, 0))],
        out_specs=[pl.BlockSpec((tm, d), lambda i: (i, 0))] * n_out,
        out_shape=[jax.ShapeDtypeStruct((t, d), F32)] * n_out,
        compiler_params=_cparams(("parallel",)),
        name="moe_combine",
    )(yk, yk, x2, gate, next_norm)


def _hier_moe(x2, ffn_norm, w_rg, b_rg, w_re, b_re, w_gate, w_up, w_down, layer, next_norm, want_stream=True):
    t, d = x2.shape
    pad = LANES - MOE_EXPERTS - MOE_GROUPS
    w_r = jnp.concatenate([w_re, w_rg, jnp.zeros((d, pad), F32)], axis=1)
    b_r = jnp.concatenate([b_re, b_rg, jnp.zeros((pad,), F32)]).reshape(1, LANES)
    h2, ids, gate = _router(x2, ffn_norm.reshape(1, d), w_r, b_r)
    n_blocks = (2 * t) // MOE_ROWS + MOE_EXPERTS
    row_dst, blk_e, blk_n = _moe_plan(ids[:, :2], n_blocks)
    yk = _moe_experts(h2, row_dst, blk_e, blk_n, w_gate, w_up, w_down, layer, n_blocks)
    return _moe_combine(yk, x2, gate, next_norm.reshape(1, d), want_stream)


def _s5_disc_kernel(lre_ref, lim_ref, ls_ref, bre_ref, bim_ref, are_ref, aim_ref, bbre_ref, bbim_ref):
    lr = jnp.minimum(lre_ref[...], -1e-4)
    li = lim_ref[...]
    step = jnp.exp(ls_ref[...])
    mag = jnp.exp(lr * step)
    ang = li * step
    ab_re = mag * jnp.cos(ang)
    ab_im = mag * jnp.sin(ang)
    den = lr * lr + li * li
    zr = ab_re - 1.0
    f_re = (zr * lr + ab_im * li) / den
    f_im = (ab_im * lr - zr * li) / den
    br = bre_ref[...]
    bi = bim_ref[...]
    are_ref[...] = ab_re
    aim_ref[...] = ab_im
    bbre_ref[...] = f_re * br - f_im * bi
    bbim_ref[...] = f_re * bi + f_im * br


def _s5_discretise(lam_re, lam_im, log_step, b_re, b_im):
    g, p = lam_re.shape
    n = g * p
    col = lambda a: a.reshape(n, 1)
    ls = jnp.broadcast_to(log_step[:, None], (g, p))
    rows = 1024
    full = lambda w: pl.BlockSpec((rows, w), lambda i: (i, 0))
    return pl.pallas_call(
        _s5_disc_kernel,
        grid=(n // rows,),
        in_specs=[full(1), full(1), full(1), full(SSM_GROUP), full(SSM_GROUP)],
        out_specs=[full(1), full(1), full(SSM_GROUP), full(SSM_GROUP)],
        out_shape=[jax.ShapeDtypeStruct((n, 1), F32), jax.ShapeDtypeStruct((n, 1), F32),
                   jax.ShapeDtypeStruct((n, SSM_GROUP), F32), jax.ShapeDtypeStruct((n, SSM_GROUP), F32)],
        compiler_params=_cparams(("parallel",)),
        name="s5_discretise",
    )(col(lam_re), col(lam_im), col(ls), b_re.reshape(n, SSM_GROUP), b_im.reshape(n, SSM_GROUP))


def _cmul(ar, ai, br, bi):
    return ar * br - ai * bi, ar * bi + ai * br


def _s5_kernel(u_ref, bw_ref, cw_ref, a_ref, d_ref, y_ref, bu_scr, xb_scr, carry_scr, il_scr):
    for q in range(S5_BLOCKS_PER_STEP):
        cols = pl.ds(q * LANES, LANES)
        _s5_block(u_ref.at[:, cols], bw_ref.at[q], cw_ref.at[q], a_ref.at[q], d_ref.at[:, cols], y_ref.at[:, cols],
                  bu_scr, xb_scr, carry_scr.at[q], il_scr)


def _s5_block(u_ref, bw_ref, cw_ref, a_ref, d_ref, y_ref, bu_scr, xb_scr, carry_scr, il_scr):
    tau = pl.program_id(2)
    Lb = u_ref.shape[0]
    seg = Lb // SUBLANES
    ns = S5_NSTATE
    ar = a_ref[0:1, :]
    ai = a_ref[1:2, :]

    @pl.when(tau == 0)
    def _():
        carry_scr[...] = jnp.zeros_like(carry_scr)

    pitch = seg + SUBLANES
    for j in range(SUBLANES):
        il_scr[j * pitch:j * pitch + seg, :] = u_ref[j * seg:(j + 1) * seg, :]
    u_perm = jnp.concatenate([il_scr[pl.ds(i, SUBLANES, stride=pitch), :] for i in range(seg)], axis=0)
    bu_scr[...] = _dot(u_perm.astype(BF16), bw_ref[...])

    ar8 = jnp.broadcast_to(ar, (SUBLANES, ns))
    ai8 = jnp.broadcast_to(ai, (SUBLANES, ns))

    def step(i, state):
        xr, xi = state
        r0 = pl.multiple_of(i * SUBLANES, SUBLANES)
        return (ar8 * xr - ai8 * xi + bu_scr[pl.ds(r0, SUBLANES), 0:ns],
                ar8 * xi + ai8 * xr + bu_scr[pl.ds(r0, SUBLANES), ns:2 * ns])

    zeros = jnp.zeros((SUBLANES, ns), F32)
    er, ei = lax.fori_loop(0, seg, step, (zeros, zeros), unroll=8)

    alr, ali = ar, ai
    for _ in range(seg.bit_length() - 1):
        alr, ali = _cmul(alr, ali, alr, ali)
    pr = carry_scr[0:1, 0:ns]
    pi = carry_scr[0:1, ns:2 * ns]
    prs, pis = [], []
    for j in range(SUBLANES):
        prs.append(pr)
        pis.append(pi)
        mr, mi = _cmul(alr, ali, pr, pi)
        pr = er[j:j + 1, :] + mr
        pi = ei[j:j + 1, :] + mi
    carry_scr[0:1, 0:ns] = pr
    carry_scr[0:1, ns:2 * ns] = pi
    p_re = jnp.concatenate(prs, axis=0)
    p_im = jnp.concatenate(pis, axis=0)
    pack = 2 * SUBLANES

    def store_body(i2, state):
        x1 = step(2 * i2, state)
        x2 = step(2 * i2 + 1, x1)
        r0 = pl.multiple_of(i2 * pack, pack)
        xb_scr[pl.ds(r0, pack), 0:ns] = jnp.concatenate([x1[0], x2[0]], axis=0).astype(BF16)
        xb_scr[pl.ds(r0, pack), ns:2 * ns] = jnp.concatenate([x1[1], x2[1]], axis=0).astype(BF16)
        return x2

    lax.fori_loop(0, seg // 2, store_body, (p_re, p_im), unroll=4)

    y = _dot(xb_scr[...], cw_ref[...]) + d_ref[...] * u_perm
    for i in range(seg):
        il_scr[pl.ds(i, SUBLANES, stride=pitch), :] = y[i * SUBLANES:(i + 1) * SUBLANES, :]
    for j in range(SUBLANES):
        y_ref[j * seg:(j + 1) * seg, :] = il_scr[j * pitch:j * pitch + seg, :]


def _s5_scan(h3, bw, cw, a_rows, d_skip):
    b, s, w = h3.shape
    Lb = min(S5_BLOCK, s)
    seg = Lb // SUBLANES
    assert seg & (seg - 1) == 0 and seg >= 2, "segment length must be a power of two"
    nq = S5_BLOCKS_PER_STEP
    wq = nq * LANES
    ns2 = 2 * S5_NSTATE
    return pl.pallas_call(
        _s5_kernel,
        grid=(b, w // wq, s // Lb),
        in_specs=[pl.BlockSpec((None, Lb, wq), lambda i, k, c: (i, c, k)),
                  pl.BlockSpec((nq, LANES, ns2), lambda i, k, c: (k, 0, 0)),
                  pl.BlockSpec((nq, ns2, LANES), lambda i, k, c: (k, 0, 0)),
                  pl.BlockSpec((nq, 2, S5_NSTATE), lambda i, k, c: (k, 0, 0)),
                  pl.BlockSpec((1, wq), lambda i, k, c: (0, k))],
        out_specs=pl.BlockSpec((None, Lb, wq), lambda i, k, c: (i, c, k)),
        out_shape=jax.ShapeDtypeStruct((b, s, w), F32),
        scratch_shapes=[pltpu.VMEM((Lb, ns2), F32), pltpu.VMEM((Lb, ns2), BF16),
                        pltpu.VMEM((nq, SUBLANES, ns2), F32),
                        pltpu.VMEM((Lb + SUBLANES * SUBLANES, LANES), F32)],
        compiler_params=_cparams(("parallel", "parallel", "arbitrary")),
        name="s5_scan",
    )(h3, bw, cw, a_rows, d_skip)


def _glu_kernel(y_ref, x_ref, wv_ref, wg_ref, bv_ref, bg_ref, o_ref, a_scr):
    @pl.when(pl.program_id(1) == 0)
    def _():
        a_scr[...] = jax.nn.gelu(y_ref[...]).astype(BF16)

    a = a_scr[...]
    val = _dot(a, wv_ref[...]) + bv_ref[...]
    gate = _dot(a, wg_ref[...]) + bg_ref[...]
    o_ref[...] = x_ref[...] + val * _sigmoid(gate)


def _glu(y2, x2, w_bf, bias):
    t, d = x2.shape
    tm = min(1024, t)
    tn = 512
    nj = d // tn
    return pl.pallas_call(
        _glu_kernel,
        grid=(t // tm, nj),
        in_specs=[pl.BlockSpec((tm, d), lambda i, j: (i, 0)),
                  pl.BlockSpec((tm, tn), lambda i, j: (i, j)),
                  pl.BlockSpec((d, tn), lambda i, j: (0, j)),
                  pl.BlockSpec((d, tn), lambda i, j: (0, nj + j)),
                  pl.BlockSpec((1, tn), lambda i, j: (0, j)),
                  pl.BlockSpec((1, tn), lambda i, j: (0, nj + j))],
        out_specs=pl.BlockSpec((tm, tn), lambda i, j: (i, j)),
        out_shape=jax.ShapeDtypeStruct((t, d), F32),
        scratch_shapes=[pltpu.VMEM((tm, d), BF16)],
        compiler_params=_cparams(("parallel", "arbitrary")),
        name="glu",
    )(y2, x2, w_bf, w_bf, bias, bias)


def _block_diag(w):
    nb, gb, r, c = w.shape
    eye = jnp.eye(gb, dtype=w.dtype)
    return (w[:, :, :, None, :] * eye[None, :, None, :, None]).reshape(nb, gb * r, gb * c)


def _mixer_s5(h3, x2, lam_re, lam_im, log_step, b_re, b_im, c_re, c_im, d_skip, w_glu, b_glu):
    b, s, w = h3.shape
    g, p = lam_re.shape
    gb = S5_GROUPS_PER_BLOCK
    nb = g // gb
    a_re, a_im, bb_re, bb_im = _s5_discretise(lam_re, lam_im, log_step, b_re, b_im)
    a_rows = jnp.stack([a_re.reshape(nb, gb * p), a_im.reshape(nb, gb * p)], axis=1)
    bt = lambda m: jnp.swapaxes(m.reshape(nb, gb, p, SSM_GROUP), 2, 3)
    bw = jnp.concatenate([_block_diag(bt(bb_re)), _block_diag(bt(bb_im))], axis=2).astype(BF16)
    ct = lambda m: jnp.swapaxes(m.reshape(nb, gb, SSM_GROUP, p), 2, 3)
    cw = jnp.concatenate([_block_diag(ct(c_re)), -_block_diag(ct(c_im))], axis=1).astype(BF16)
    y = _s5_scan(h3, bw, cw, a_rows, d_skip.reshape(1, w))
    return _glu(y.reshape(b * s, w), x2, w_glu.astype(BF16), b_glu.reshape(1, -1))


def _mixer_ab(x2, bsz, norm_w, w_in, a_i_bias, a_f_bias, a_norm, b_conv, b_a_log, b_dt_bias, b_norm, w_out):
    t, d = x2.shape
    s = t // bsz
    zeros = lambda n: jnp.zeros((n,), F32)
    n_a = 2 * A_HEADS * A_QK_DIM + 2 * A_HEADS * A_V_DIM
    n_b = 4 * B_HEADS * B_HEAD_DIM
    a_end = n_a + 2 * A_HEADS
    w_gates = jnp.concatenate([w_in[:, n_a:a_end], w_in[:, a_end + n_b:],
                               jnp.zeros((d, LANES - G_END), F32)], axis=1).astype(BF16)
    w_wide = jnp.concatenate([w_in[:, :n_a], w_in[:, a_end:a_end + n_b]], axis=1).astype(BF16)
    proj, gates = _inproj(x2, norm_w.reshape(1, d), w_wide, w_gates)
    proj = proj.reshape(bsz, s, N_PROJ)
    gates = gates.reshape(bsz, s, LANES)
    gates_t = jnp.swapaxes(gates[:, :, :GATE_ROWS], 1, 2)
    bias = jnp.concatenate([a_i_bias, a_f_bias, zeros(B_HEADS), b_dt_bias, zeros(LANES - G_END)])
    alog = jnp.concatenate([zeros(G_BA), b_a_log, zeros(LANES - G_END)])
    pc, pr = _gates(gates, gates_t, bias.reshape(1, LANES), bias[:GATE_ROWS].reshape(GATE_ROWS, 1),
                    alog.reshape(1, LANES), alog[:GATE_ROWS].reshape(GATE_ROWS, 1))
    ya = _mlstm(proj, pc, pr, a_norm.reshape(1, -1))
    yb = _gdn(proj, pc, pr, b_conv, b_norm.reshape(1, -1))
    n_ya = A_HEADS * A_V_DIM
    return _outproj(ya.reshape(t, n_ya), yb.reshape(t, -1), x2, w_out.astype(BF16))


def kernel(x, mix_norm, ab_w_in, mlstm_i_bias, mlstm_f_bias, mlstm_norm, gdn_conv, gdn_a_log, gdn_dt_bias, gdn_norm, ab_w_out, ssm_lambda_re, ssm_lambda_im, ssm_log_step, ssm_b_re, ssm_b_im, ssm_c_re, ssm_c_im, ssm_d, glu_w, glu_b, ffn_norm, router_group_w, router_group_b, router_expert_w, router_expert_b, expert_w_gate, expert_w_up, expert_w_down, final_norm):
    bsz, s, d = x.shape
    t = bsz * s
    x2 = x.reshape(t, d)
    x2 = _mixer_ab(x2, bsz, mix_norm[0], ab_w_in[0], mlstm_i_bias[0], mlstm_f_bias[0], mlstm_norm[0],
                   gdn_conv[0], gdn_a_log[0], gdn_dt_bias[0], gdn_norm[0], ab_w_out[0])
    x2, h3 = _hier_moe(x2, ffn_norm[0], router_group_w[0], router_group_b[0], router_expert_w[0],
                       router_expert_b[0], expert_w_gate, expert_w_up, expert_w_down, 0, mix_norm[1])
    x2 = _mixer_s5(h3.reshape(bsz, s, d), x2, ssm_lambda_re[0], ssm_lambda_im[0], ssm_log_step[0],
                   ssm_b_re[0], ssm_b_im[0], ssm_c_re[0], ssm_c_im[0], ssm_d[0], glu_w[0], glu_b[0])
    out, = _hier_moe(x2, ffn_norm[1], router_group_w[1], router_group_b[1], router_expert_w[1],
                     router_expert_b[1], expert_w_gate, expert_w_up, expert_w_down, 1, final_norm, want_stream=False)
    return out.reshape(bsz, s, d)
```

```python
import functools

import jax
import jax.numpy as jnp
from jax import lax
from jax.experimental import pallas as pl
from jax.experimental.pallas import tpu as pltpu

F32 = jnp.float32
BF16 = jnp.bfloat16
HIGHEST = lax.Precision.HIGHEST

EPS = 1e-6
D_MODEL = 2048
A_HEADS = 4
A_QK_DIM = 128
A_V_DIM = 256
A_GATE_CAP = 15.0
B_HEADS = 8
B_HEAD_DIM = 128
B_CONV = 4
SSM_GROUP = 16
SSM_STATE = 64
MOE_GROUPS = 8
MOE_PER_GROUP = 8
MOE_EXPERTS = 64
MOE_FF = 768

LANES = 128
SUBLANES = 8
VMEM_LIMIT = 56 * 1024 * 1024

COL_AQ, COL_AK, COL_AV, COL_AO = 0, 512, 1024, 2048
COL_BQ, COL_BK, COL_BV, COL_BZ = 3072, 4096, 5120, 6144
N_PROJ = 7168
G_AI, G_AF, G_BB, G_BA, G_END = 0, 4, 8, 16, 24
GATE_ROWS = 32

MLSTM_CHUNK = 256
GDN_BLOCK = MLSTM_CHUNK
GDN_CHUNK = 64
GDN_HEADS_PER_STEP = 8
S5_BLOCK = 512
S5_GROUPS_PER_BLOCK = 8
S5_BLOCKS_PER_STEP = 1
S5_NSTATE = S5_GROUPS_PER_BLOCK * SSM_STATE
MOE_ROWS = 512
MOE_SUB = 256
MOE_KT = 2
MOE_KW = D_MODEL // MOE_KT
MOE_DMA_GROUP_LOG2 = 4
MOE_DMA_GROUP = 1 << MOE_DMA_GROUP_LOG2


def _cparams(sem):
    return pltpu.CompilerParams(dimension_semantics=sem, vmem_limit_bytes=VMEM_LIMIT)


def _softcap(t, cap):
    return cap * jnp.tanh(t / cap)


def _log_sigmoid(t):
    return jnp.minimum(t, 0.0) - jnp.log(1.0 + jnp.exp(-jnp.abs(t)))


def _softplus(t):
    return jnp.maximum(t, 0.0) + jnp.log(1.0 + jnp.exp(-jnp.abs(t)))


def _sigmoid(t):
    return 1.0 / (1.0 + jnp.exp(-t))


def _silu(t):
    return t * _sigmoid(t)


def _pick_col(x, idx):
    lane = lax.broadcasted_iota(jnp.int32, x.shape, 1)
    return jnp.sum(jnp.where(lane == idx, x, 0.0), axis=-1, keepdims=True)


def _dot(a, b):
    return jnp.dot(a, b, preferred_element_type=F32)


def _dot_hi(a, b):
    return jnp.dot(a, b, precision=HIGHEST, preferred_element_type=F32)


def _inproj_kernel(x_ref, g_ref, w_ref, wg_ref, o_ref, og_ref, h_scr):
    @pl.when(pl.program_id(1) == 0)
    def _():
        x = x_ref[...]
        ms = jnp.mean(x * x, axis=-1, keepdims=True)
        h_scr[...] = (x * lax.rsqrt(ms + EPS) * g_ref[...]).astype(BF16)
        og_ref[...] = _dot(h_scr[...], wg_ref[...])

    o_ref[...] = _dot(h_scr[...], w_ref[...])


def _inproj(x2, g, w_bf, wg_bf):
    t, d = x2.shape
    n = w_bf.shape[1]
    tm = min(1024, t)
    tn = 1024
    return pl.pallas_call(
        _inproj_kernel,
        grid=(t // tm, n // tn),
        in_specs=[pl.BlockSpec((tm, d), lambda i, j: (i, 0)),
                  pl.BlockSpec((1, d), lambda i, j: (0, 0)),
                  pl.BlockSpec((d, tn), lambda i, j: (0, j)),
                  pl.BlockSpec((d, LANES), lambda i, j: (0, 0))],
        out_specs=[pl.BlockSpec((tm, tn), lambda i, j: (i, j)),
                   pl.BlockSpec((tm, LANES), lambda i, j: (i, 0))],
        out_shape=[jax.ShapeDtypeStruct((t, n), F32), jax.ShapeDtypeStruct((t, LANES), F32)],
        scratch_shapes=[pltpu.VMEM((tm, d), BF16)],
        compiler_params=_cparams(("parallel", "arbitrary")),
        name="inproj",
    )(x2, g, w_bf, wg_bf)


def _gate_values(g, alog, idx, cum_a, cum_b):
    sc = _softcap(g, A_GATE_CAP)
    cum_logf = cum_a(_log_sigmoid(sc))
    cum_g = cum_b(-jnp.exp(alog) * _softplus(g))
    return jnp.where(idx < G_AF, sc,
                     jnp.where(idx < G_BB, cum_logf,
                               jnp.where(idx < G_BA, _sigmoid(g), jnp.where(idx < G_END, cum_g, 0.0))))


def _gates_kernel(gc_ref, gr_ref, bc_ref, br_ref, alc_ref, alr_ref, pc_ref, pr_ref):
    L = gc_ref.shape[0]
    row = lax.broadcasted_iota(jnp.int32, (L, L), 0)
    col = lax.broadcasted_iota(jnp.int32, (L, L), 1)
    same = (row // GDN_CHUNK) == (col // GDN_CHUNK)
    tril = (row >= col).astype(F32)
    triu = (row <= col).astype(F32)
    blk_tril = (same & (row >= col)).astype(F32)
    blk_triu = (same & (row <= col)).astype(F32)
    lane = lax.broadcasted_iota(jnp.int32, (L, LANES), 1)
    pc_ref[...] = _gate_values(gc_ref[...] + bc_ref[...], alc_ref[...], lane,
                               lambda v: _dot_hi(tril, v), lambda v: _dot_hi(blk_tril, v))
    sub = lax.broadcasted_iota(jnp.int32, (GATE_ROWS, L), 0)
    pr_ref[...] = _gate_values(gr_ref[...] + br_ref[...], alr_ref[...], sub,
                               lambda v: _dot_hi(v, triu), lambda v: _dot_hi(v, blk_triu))


def _gates(gates, gates_t, bias_col, bias_row, alog_col, alog_row):
    b, s, _ = gates.shape
    L = min(MLSTM_CHUNK, s)
    return pl.pallas_call(
        _gates_kernel,
        grid=(b, s // L),
        in_specs=[pl.BlockSpec((None, L, LANES), lambda i, c: (i, c, 0)),
                  pl.BlockSpec((None, GATE_ROWS, L), lambda i, c: (i, 0, c)),
                  pl.BlockSpec((1, LANES), lambda i, c: (0, 0)),
                  pl.BlockSpec((GATE_ROWS, 1), lambda i, c: (0, 0)),
                  pl.BlockSpec((1, LANES), lambda i, c: (0, 0)),
                  pl.BlockSpec((GATE_ROWS, 1), lambda i, c: (0, 0))],
        out_specs=[pl.BlockSpec((None, L, LANES), lambda i, c: (i, c, 0)),
                   pl.BlockSpec((None, GATE_ROWS, L), lambda i, c: (i, 0, c))],
        out_shape=[jax.ShapeDtypeStruct((b, s, LANES), F32),
                   jax.ShapeDtypeStruct((b, GATE_ROWS, s), F32)],
        compiler_params=_cparams(("parallel", "parallel")),
        name="gates",
    )(gates, gates_t, bias_col, bias_row, alog_col, alog_row)


def _mlstm_kernel(q_ref, k_ref, v_ref, o_ref, pc_ref, pr_ref, nw_ref, out_ref, c_scr, n_scr, m_scr):
    c = pl.program_id(1)
    L = q_ref.shape[0]
    nh, dk, dv = A_HEADS, A_QK_DIM, A_V_DIM

    @pl.when(c == 0)
    def _():
        c_scr[...] = jnp.zeros_like(c_scr)
        n_scr[...] = jnp.zeros_like(n_scr)
        m_scr[...] = jnp.zeros_like(m_scr)

    row = lax.broadcasted_iota(jnp.int32, (L, L), 0)
    col = lax.broadcasted_iota(jnp.int32, (L, L), 1)
    causal = row >= col
    per_head = lambda fn: jnp.stack([fn(h) for h in range(nh)], axis=0)
    bmm = lambda a, b: jnp.einsum("bij,bjk->bik", a.astype(BF16), b.astype(BF16), preferred_element_type=F32)
    bmm_nt = lambda a, b: jnp.einsum("bik,bjk->bij", a.astype(BF16), b.astype(BF16), preferred_element_type=F32)

    pc = pc_ref[...]
    i_col = per_head(lambda h: _pick_col(pc, G_AI + h))
    b_col = per_head(lambda h: _pick_col(pc, G_AF + h))
    i_row = per_head(lambda h: pr_ref[G_AI + h:G_AI + h + 1, :])
    b_row = per_head(lambda h: pr_ref[G_AF + h:G_AF + h + 1, :])
    b_last = b_col[:, L - 1:L, :]

    m_prev = m_scr[...]
    log_d = jnp.where(causal, b_col - b_row + i_row, -jnp.inf)
    log_inter = b_col + m_prev
    m_t = jnp.maximum(log_inter, jnp.max(log_d, axis=-1, keepdims=True))
    dmat = jnp.exp(log_d - m_t)
    inter = jnp.exp(log_inter - m_t)

    q = per_head(lambda h: q_ref[:, h * dk:(h + 1) * dk])
    k = per_head(lambda h: k_ref[:, h * dk:(h + 1) * dk]) * (dk ** -0.5)
    vb = per_head(lambda h: v_ref[:, h * dv:(h + 1) * dv]).astype(BF16)
    scores = bmm_nt(q, k) * dmat
    c_mat = c_scr[...]
    n_vec = n_scr[...]
    num = inter * bmm(q, c_mat) + bmm(scores, vb)
    den = inter * jnp.sum(q * n_vec, axis=-1, keepdims=True) + jnp.sum(scores, axis=-1, keepdims=True)
    hh = num / jnp.maximum(jnp.abs(den), jnp.exp(-m_t))
    hh = hh * lax.rsqrt(jnp.mean(hh * hh, axis=-1, keepdims=True) + EPS)
    for h in range(nh):
        cols = slice(h * dv, (h + 1) * dv)
        out_ref[:, cols] = hh[h] * nw_ref[:, cols] * _sigmoid(o_ref[:, cols])

    le_col = b_last - b_col + i_col
    m_new = jnp.maximum(b_last + m_prev, jnp.max(le_col, axis=1, keepdims=True))
    carry_scale = jnp.exp(b_last + m_prev - m_new)
    kw = k * jnp.exp(le_col - m_new)
    c_scr[...] = c_mat * carry_scale + bmm(jnp.swapaxes(kw, 1, 2), vb)
    n_scr[...] = n_vec * carry_scale + jnp.sum(kw, axis=1, keepdims=True)
    m_scr[...] = m_new


def _mlstm(proj, pc, pr, norm_w):
    b, s, _ = proj.shape
    L = min(MLSTM_CHUNK, s)
    wqk, wv = A_HEADS * A_QK_DIM, A_HEADS * A_V_DIM
    return pl.pallas_call(
        _mlstm_kernel,
        grid=(b, s // L),
        in_specs=[pl.BlockSpec((None, L, wqk), lambda i, c: (i, c, COL_AQ // wqk)),
                  pl.BlockSpec((None, L, wqk), lambda i, c: (i, c, COL_AK // wqk)),
                  pl.BlockSpec((None, L, wv), lambda i, c: (i, c, COL_AV // wv)),
                  pl.BlockSpec((None, L, wv), lambda i, c: (i, c, COL_AO // wv)),
                  pl.BlockSpec((None, L, LANES), lambda i, c: (i, c, 0)),
                  pl.BlockSpec((None, GATE_ROWS, L), lambda i, c: (i, 0, c)),
                  pl.BlockSpec((1, wv), lambda i, c: (0, 0))],
        out_specs=pl.BlockSpec((None, L, wv), lambda i, c: (i, c, 0)),
        out_shape=jax.ShapeDtypeStruct((b, s, wv), F32),
        scratch_shapes=[pltpu.VMEM((A_HEADS, A_QK_DIM, A_V_DIM), F32),
                        pltpu.VMEM((A_HEADS, 1, A_QK_DIM), F32),
                        pltpu.VMEM((A_HEADS, 1, 1), F32)],
        compiler_params=_cparams(("parallel", "arbitrary")),
        name="mlstm",
    )(proj, proj, proj, proj, pc, pr, norm_w)


def _causal_conv_silu(x, tail, w):
    row8 = lax.broadcasted_iota(jnp.int32, (SUBLANES, x.shape[1]), 0)
    acc = x * w[B_CONV - 1:B_CONV, :]
    for d in range(1, B_CONV):
        rolled = pltpu.roll(x, d, 0)
        head = jnp.where(row8 < d, pltpu.roll(tail, d, 0), rolled[0:SUBLANES, :])
        shifted = jnp.concatenate([head, rolled[SUBLANES:, :]], axis=0)
        acc = acc + shifted * w[B_CONV - 1 - d:B_CONV - d, :]
    return _silu(acc)


def _l2norm(t):
    return t * lax.rsqrt(jnp.sum(t * t, axis=-1, keepdims=True) + EPS)


def _gdn_kernel(q_ref, k_ref, v_ref, z_ref, pc_ref, pr_ref, wq_ref, wk_ref, wv_ref, nw_ref, out_ref,
                s_scr, tq_scr, tk_scr, tv_scr):
    hp = pl.program_id(1)
    c = pl.program_id(2)
    Lb = q_ref.shape[0]
    hd = B_HEAD_DIM

    @pl.when(c == 0)
    def _():
        s_scr[...] = jnp.zeros_like(s_scr)
        tq_scr[...] = jnp.zeros_like(tq_scr)
        tk_scr[...] = jnp.zeros_like(tk_scr)
        tv_scr[...] = jnp.zeros_like(tv_scr)

    xq, xk, xv = q_ref[...], k_ref[...], v_ref[...]
    cq = _causal_conv_silu(xq, tq_scr[...], wq_ref[...])
    ck = _causal_conv_silu(xk, tk_scr[...], wk_ref[...])
    cv = _causal_conv_silu(xv, tv_scr[...], wv_ref[...])
    tq_scr[...] = xq[Lb - SUBLANES:, :]
    tk_scr[...] = xk[Lb - SUBLANES:, :]
    tv_scr[...] = xv[Lb - SUBLANES:, :]
    pc = pc_ref[...]
    nw = nw_ref[...]
    heads = [hp * GDN_HEADS_PER_STEP + hh for hh in range(GDN_HEADS_PER_STEP)]
    per_head = lambda x: jnp.stack([x[:, hh * hd:(hh + 1) * hd] for hh in range(GDN_HEADS_PER_STEP)], axis=0)
    hb, s_scr[...] = _gdn_heads(
        _l2norm(per_head(cq)) * (hd ** -0.5), _l2norm(per_head(ck)), per_head(cv),
        jnp.stack([_pick_col(pc, G_BB + h) for h in heads], axis=0),
        jnp.stack([_pick_col(pc, G_BA + h) for h in heads], axis=0),
        jnp.stack([pr_ref[pl.ds(G_BA + h, 1), :] for h in heads], axis=0), s_scr[...])
    hb = hb * lax.rsqrt(jnp.mean(hb * hb, axis=-1, keepdims=True) + EPS) * nw
    for hh in range(GDN_HEADS_PER_STEP):
        cols = slice(hh * hd, (hh + 1) * hd)
        out_ref[:, cols] = hb[hh] * _silu(z_ref[:, cols])


def _gdn_heads(q, k, v, beta, dec_col, dec_row, state):
    nh, Lb, _ = q.shape
    C = GDN_CHUNK
    nsub = Lb // C
    bmm = lambda a, b: jnp.einsum("bij,bjk->bik", a.astype(BF16), b.astype(BF16), preferred_element_type=F32)
    bmm_nt = lambda a, b: jnp.einsum("bik,bjk->bij", a.astype(BF16), b.astype(BF16), preferred_element_type=F32)
    r64 = lax.broadcasted_iota(jnp.int32, (C, C), 0)
    c64 = lax.broadcasted_iota(jnp.int32, (C, C), 1)
    causal = r64 >= c64
    strict = r64 > c64
    eye = (r64 == c64).astype(F32)

    kbeta = k * beta
    edec = jnp.exp(dec_col)
    q_dec = (q * edec).astype(BF16)
    vbeta = (v * beta).astype(BF16)
    kbdec = (kbeta * edec).astype(BF16)
    kb16 = k.astype(BF16)
    qb16 = q.astype(BF16)
    kbeta16 = kbeta.astype(BF16)

    segs, nmats = [], []
    for j in range(nsub):
        sl = slice(j * C, (j + 1) * C)
        seg = jnp.exp(jnp.where(causal, dec_col[:, sl, :] - dec_row[:, :, sl], -jnp.inf))
        a_low = jnp.where(strict, bmm_nt(kbeta16[:, sl, :], kb16[:, sl, :]) * seg, 0.0)
        segs.append(seg)
        nmats.append(-a_low)
    nmat = jnp.concatenate(nmats, axis=0)
    tmat = eye[None] + nmat
    npow = bmm(nmat, nmat)
    for _ in range(4):
        tmat, npow = tmat + bmm(tmat, npow), bmm(npow, npow)
    tmat = tmat + bmm(tmat, npow)

    outs = []
    for j in range(nsub):
        sl = slice(j * C, (j + 1) * C)
        t16 = tmat[j * nh:(j + 1) * nh].astype(BF16)
        u = bmm(t16, vbeta[:, sl, :])
        w = bmm(t16, kbdec[:, sl, :])
        attn = bmm_nt(qb16[:, sl, :], kb16[:, sl, :]) * segs[j]
        s16 = state.astype(BF16)
        v_new = u - bmm(w, s16)
        o = bmm(q_dec[:, sl, :], s16) + bmm(attn, v_new)
        d_last = dec_col[:, (j + 1) * C - 1:(j + 1) * C, :]
        k_end = k[:, sl, :] * jnp.exp(d_last - dec_col[:, sl, :])
        state = state * jnp.exp(d_last) + bmm(jnp.swapaxes(k_end, 1, 2), v_new)
        outs.append(o)
    return jnp.concatenate(outs, axis=1), state


def _gdn(proj, pc, pr, conv_w, norm_w):
    b, s, _ = proj.shape
    Lb = min(GDN_BLOCK, s)
    hd = B_HEAD_DIM
    wd = GDN_HEADS_PER_STEP * hd
    npair = B_HEADS // GDN_HEADS_PER_STEP
    qb, kb, vb, zb = COL_BQ // wd, COL_BK // wd, COL_BV // wd, COL_BZ // wd
    blk = lambda off: pl.BlockSpec((None, Lb, wd), lambda i, h, c: (i, c, off + h))
    return pl.pallas_call(
        _gdn_kernel,
        grid=(b, npair, s // Lb),
        in_specs=[blk(qb), blk(kb), blk(vb), blk(zb),
                  pl.BlockSpec((None, Lb, LANES), lambda i, h, c: (i, c, 0)),
                  pl.BlockSpec((None, GATE_ROWS, Lb), lambda i, h, c: (i, 0, c)),
                  pl.BlockSpec((B_CONV, wd), lambda i, h, c: (0, h)),
                  pl.BlockSpec((B_CONV, wd), lambda i, h, c: (0, npair + h)),
                  pl.BlockSpec((B_CONV, wd), lambda i, h, c: (0, 2 * npair + h)),
                  pl.BlockSpec((1, hd), lambda i, h, c: (0, 0))],
        out_specs=pl.BlockSpec((None, Lb, wd), lambda i, h, c: (i, c, h)),
        out_shape=jax.ShapeDtypeStruct((b, s, B_HEADS * hd), F32),
        scratch_shapes=[pltpu.VMEM((GDN_HEADS_PER_STEP, hd, hd), F32),
                        pltpu.VMEM((SUBLANES, wd), F32),
                        pltpu.VMEM((SUBLANES, wd), F32),
                        pltpu.VMEM((SUBLANES, wd), F32)],
        compiler_params=_cparams(("parallel", "parallel", "arbitrary")),
        name="gdn",
    )(proj, proj, proj, proj, pc, pr, conv_w, conv_w, conv_w, norm_w)


def _outproj_kernel(ya_ref, yb_ref, x_ref, wa_ref, wb_ref, o_ref):
    acc = _dot(ya_ref[...].astype(BF16), wa_ref[...]) + _dot(yb_ref[...].astype(BF16), wb_ref[...])
    o_ref[...] = x_ref[...] + acc


def _outproj(ya, yb, x2, w):
    t, d = x2.shape
    ka, kb = ya.shape[1], yb.shape[1]
    assert ka == kb and w.shape == (ka + kb, d)
    tm = min(512, t)
    return pl.pallas_call(
        _outproj_kernel,
        grid=(t // tm,),
        in_specs=[pl.BlockSpec((tm, ka), lambda i: (i, 0)),
                  pl.BlockSpec((tm, kb), lambda i: (i, 0)),
                  pl.BlockSpec((tm, d), lambda i: (i, 0)),
                  pl.BlockSpec((ka, d), lambda i: (0, 0)),
                  pl.BlockSpec((kb, d), lambda i: (1, 0))],
        out_specs=pl.BlockSpec((tm, d), lambda i: (i, 0)),
        out_shape=jax.ShapeDtypeStruct((t, d), F32),
        compiler_params=_cparams(("parallel",)),
        name="outproj",
    )(ya, yb, x2, w, w)


def _router_kernel(x_ref, g_ref, whi_ref, wlo_ref, b_ref, h_ref, ids_ref, gate_ref):
    x = x_ref[...]
    ms = jnp.mean(x * x, axis=-1, keepdims=True)
    h = x * lax.rsqrt(ms + EPS) * g_ref[...]
    h_ref[...] = h
    h_hi = h.astype(BF16)
    h_lo = (h - h_hi.astype(F32)).astype(BF16)
    w_hi = whi_ref[...]
    logits = _dot(h_hi, w_hi) + _dot(h_lo, w_hi) + _dot(h_hi, wlo_ref[...]) + b_ref[...]
    lane = lax.broadcasted_iota(jnp.int32, logits.shape, 1)
    neg = -jnp.inf
    big = jnp.int32(1 << 20)
    is_g = (lane >= MOE_EXPERTS) & (lane < MOE_EXPERTS + MOE_GROUPS)
    gl = jnp.where(is_g, logits, neg)
    gmax = jnp.max(gl, axis=-1, keepdims=True)
    g_lane = jnp.min(jnp.where(gl == gmax, lane, big), axis=-1, keepdims=True)
    g_idx = g_lane - MOE_EXPERTS
    g_w = 1.0 / jnp.sum(jnp.exp(gl - gmax), axis=-1, keepdims=True)
    in_grp = (lane >= g_idx * MOE_PER_GROUP) & (lane < (g_idx + 1) * MOE_PER_GROUP)
    el = jnp.where(in_grp, logits, neg)
    e0 = jnp.max(el, axis=-1, keepdims=True)
    l0 = jnp.min(jnp.where(el == e0, lane, big), axis=-1, keepdims=True)
    el1 = jnp.where(lane == l0, neg, el)
    e1 = jnp.max(el1, axis=-1, keepdims=True)
    l1 = jnp.min(jnp.where(el1 == e1, lane, big), axis=-1, keepdims=True)
    r = jnp.exp(e1 - e0)
    p0 = 1.0 / (1.0 + r)
    p1 = r / (1.0 + r)
    ids_ref[...] = jnp.where(lane == 0, l0, jnp.where(lane == 1, l1, 0))
    gate_ref[...] = jnp.where(lane == 0, g_w * p0, jnp.where(lane == 1, g_w * p1, 0.0))


def _router(x2, g, w, bias):
    t, d = x2.shape
    tm = min(512, t)
    w_hi = w.astype(BF16)
    w_lo = (w - w_hi.astype(F32)).astype(BF16)
    return pl.pallas_call(
        _router_kernel,
        grid=(t // tm,),
        in_specs=[pl.BlockSpec((tm, d), lambda i: (i, 0)),
                  pl.BlockSpec((1, d), lambda i: (0, 0)),
                  pl.BlockSpec((d, LANES), lambda i: (0, 0)),
                  pl.BlockSpec((d, LANES), lambda i: (0, 0)),
                  pl.BlockSpec((1, LANES), lambda i: (0, 0))],
        out_specs=[pl.BlockSpec((tm, d), lambda i: (i, 0)),
                   pl.BlockSpec((tm, LANES), lambda i: (i, 0)),
                   pl.BlockSpec((tm, LANES), lambda i: (i, 0))],
        out_shape=[jax.ShapeDtypeStruct((t, d), F32),
                   jax.ShapeDtypeStruct((t, LANES), jnp.int32),
                   jax.ShapeDtypeStruct((t, LANES), F32)],
        compiler_params=_cparams(("parallel",)),
        name="router",
    )(x2, g, w_hi, w_lo, bias)


def _moe_plan(expert, n_blocks):
    t = expert.shape[0]
    n_assign = 2 * t
    flat_e = expert.reshape(n_assign)
    onehot = (flat_e[:, None] == jnp.arange(MOE_EXPERTS, dtype=jnp.int32)[None, :]).astype(jnp.int32)
    csum = jnp.cumsum(onehot, axis=0)
    rank = jnp.sum(onehot * csum, axis=1) - 1
    counts = csum[-1]
    nblk = (counts + MOE_ROWS - 1) // MOE_ROWS
    blk_end = jnp.cumsum(nblk)
    blk_start = blk_end - nblk
    dest = jnp.sum(onehot * blk_start[None, :], axis=1) * MOE_ROWS + rank
    spare = 2 * (t + jnp.arange(n_blocks * MOE_ROWS, dtype=jnp.int32) % MOE_DMA_GROUP)
    row_dst = spare.at[dest].set(jnp.arange(n_assign, dtype=jnp.int32))
    bid = jnp.arange(n_blocks, dtype=jnp.int32)
    total = blk_end[-1]
    be = jnp.minimum(jnp.sum((bid[:, None] >= blk_end[None, :]).astype(jnp.int32), axis=1), MOE_EXPERTS - 1)
    used = bid < total
    blk_n = jnp.where(used, jnp.clip(counts[be] - (bid - blk_start[be]) * MOE_ROWS, 0, MOE_ROWS), 0)
    blk_e = jnp.where(used, be, be[jnp.maximum(total - 1, 0)])
    return row_dst, blk_e.astype(jnp.int32), blk_n.astype(jnp.int32)


def _moe_kernel(blk_e_ref, blk_n_ref, dst_ref, h_hbm, wg_ref, wu_ref, wd_ref, yk_hbm,
                xf_scr, xb_scr, g_scr, u_scr, y_scr, z_scr, gsem, ssem):
    b = pl.program_id(0)
    k = pl.program_id(1)
    nb = pl.num_programs(0)
    n = blk_n_ref[b]
    R = MOE_ROWS
    G = MOE_DMA_GROUP
    t, d = h_hbm.shape


    def gather_copy(blk, g, j):
        tok = jnp.minimum(lax.shift_right_logical(dst_ref[blk * R + g * G + j], 1), t - 1)
        return pltpu.make_async_copy(h_hbm.at[pl.ds(tok, 1), :], xf_scr.at[g, pl.ds(j, 1), :], gsem)

    def scatter_copy(blk, g, j):
        dst = dst_ref[blk * R + g * G + j]
        tok = lax.shift_right_logical(dst, 1)
        return pltpu.make_async_copy(y_scr.at[g, pl.ds(j, 1), :], yk_hbm.at[dst & 1, pl.ds(tok, 1), :], ssem)

    def for_rows(blk, fn):
        def body(g, carry):
            for j in range(G):
                fn(blk, g, j)
            return carry
        lax.fori_loop(0, lax.shift_right_logical(blk_n_ref[blk] + (G - 1), MOE_DMA_GROUP_LOG2), body, 0)

    @pl.when((b == 0) & (k == 0))
    def _():
        xf_scr[...] = jnp.zeros_like(xf_scr)
        for_rows(0, lambda *a: gather_copy(*a).start())
        z_scr[...] = jnp.zeros_like(z_scr)
        spare = [pltpu.make_async_copy(z_scr, yk_hbm.at[s, pl.ds(t, MOE_DMA_GROUP), :], ssem) for s in range(2)]
        for cp in spare:
            cp.start()
        for cp in spare:
            cp.wait()

    @pl.when(k == 0)
    def _():
        for_rows(b, lambda *a: gather_copy(*a).wait())
        for kk in range(MOE_KT):
            xb_scr[kk] = xf_scr[:, :, kk * MOE_KW:(kk + 1) * MOE_KW].reshape(R, MOE_KW).astype(BF16)

        @pl.when(b + 1 < nb)
        def _():
            for_rows(b + 1, lambda *a: gather_copy(*a).start())

    def for_sub_blocks(fn):
        for sb in range(R // MOE_SUB):
            @pl.when(sb * MOE_SUB < n)
            def _():
                fn(slice(sb * MOE_SUB, (sb + 1) * MOE_SUB))

    @pl.when(n > 0)
    def _():
        wg = wg_ref[...].astype(BF16)
        wu = wu_ref[...].astype(BF16)

        def accumulate(rows):
            x = xb_scr[k, rows, :]
            pg = _dot(x, wg)
            pu = _dot(x, wu)

            @pl.when(k == 0)
            def _():
                g_scr[rows, :] = pg
                u_scr[rows, :] = pu

            @pl.when(k != 0)
            def _():
                g_scr[rows, :] += pg
                u_scr[rows, :] += pu

        for_sub_blocks(accumulate)

    @pl.when(k == MOE_KT - 1)
    def _():
        @pl.when(b > 0)
        def _():
            for_rows(b - 1, lambda *a: scatter_copy(*a).wait())

        @pl.when(n > 0)
        def _():
            wd = wd_ref[...].astype(BF16)

            def down(rows):
                hmid = _silu(g_scr[rows, :]) * u_scr[rows, :]
                y = _dot(hmid.astype(BF16), wd)
                y_scr[rows.start // G:rows.stop // G] = y.reshape(MOE_SUB // G, G, d)

            for_sub_blocks(down)
            for_rows(b, lambda *a: scatter_copy(*a).start())

        @pl.when(b == nb - 1)
        def _():
            for_rows(b, lambda *a: scatter_copy(*a).wait())


def _moe_experts(h2, row_dst, blk_e, blk_n, w_gate, w_up, w_down, layer, n_blocks):
    t, d = h2.shape
    R = MOE_ROWS
    last = MOE_KT - 1

    def w_in_map(b, k, be, bn, dst):
        return (layer, be[b], jnp.where(bn[b] > 0, k, last), 0)

    def w_out_map(b, k, be, bn, dst):
        return (layer, be[b], 0, 0)

    grid_spec = pltpu.PrefetchScalarGridSpec(
        num_scalar_prefetch=3,
        grid=(n_blocks, MOE_KT),
        in_specs=[pl.BlockSpec(memory_space=pl.ANY),
                  pl.BlockSpec((None, None, MOE_KW, MOE_FF), w_in_map),
                  pl.BlockSpec((None, None, MOE_KW, MOE_FF), w_in_map),
                  pl.BlockSpec((None, None, MOE_FF, d), w_out_map)],
        out_specs=pl.BlockSpec(memory_space=pl.ANY),
        scratch_shapes=[pltpu.VMEM((R // MOE_DMA_GROUP, MOE_DMA_GROUP, d), F32),
                        pltpu.VMEM((MOE_KT, R, MOE_KW), BF16),
                        pltpu.VMEM((R, MOE_FF), F32), pltpu.VMEM((R, MOE_FF), F32),
                        pltpu.VMEM((R // MOE_DMA_GROUP, MOE_DMA_GROUP, d), F32),
                        pltpu.VMEM((MOE_DMA_GROUP, d), F32),
                        pltpu.SemaphoreType.DMA(()), pltpu.SemaphoreType.DMA(())],
    )
    return pl.pallas_call(
        _moe_kernel,
        grid_spec=grid_spec,
        out_shape=jax.ShapeDtypeStruct((2, t + MOE_DMA_GROUP, d), F32),
        compiler_params=_cparams(("arbitrary", "arbitrary")),
        name="moe_experts",
    )(blk_e, blk_n, row_dst, h2, w_gate, w_up, w_down)


def _combine_kernel(y0_ref, y1_ref, x_ref, gate_ref, g_ref, *out_refs):
    gate = gate_ref[...]
    x = x_ref[...] + gate[:, 0:1] * y0_ref[...] + gate[:, 1:2] * y1_ref[...]
    if len(out_refs) == 2:
        out_refs[0][...] = x
    ms = jnp.mean(x * x, axis=-1, keepdims=True)
    out_refs[-1][...] = x * lax.rsqrt(ms + EPS) * g_ref[...]


def _moe_combine(yk, x2, gate, next_norm, want_stream):
    t, d = x2.shape
    tm = min(512, t)
    n_out = 2 if want_stream else 1
    return pl.pallas_call(
        _combine_kernel,
        grid=(t // tm,),
        in_specs=[pl.BlockSpec((None, tm, d), lambda i: (0, i, 0)),
                  pl.BlockSpec((None, tm, d), lambda i: (1, i, 0)),
                  pl.BlockSpec((tm, d), lambda i: (i, 0)),
                  pl.BlockSpec((tm, LANES), lambda i: (i, 0)),
                  pl.BlockSpec((1, d), lambda i: (0, 0))],
        out_specs=[pl.BlockSpec((tm, d), lambda i: (i, 0))] * n_out,
        out_shape=[jax.ShapeDtypeStruct((t, d), F32)] * n_out,
        compiler_params=_cparams(("parallel",)),
        name="moe_combine",
    )(yk, yk, x2, gate, next_norm)


def _hier_moe(x2, ffn_norm, w_rg, b_rg, w_re, b_re, w_gate, w_up, w_down, layer, next_norm, want_stream=True):
    t, d = x2.shape
    pad = LANES - MOE_EXPERTS - MOE_GROUPS
    w_r = jnp.concatenate([w_re, w_rg, jnp.zeros((d, pad), F32)], axis=1)
    b_r = jnp.concatenate([b_re, b_rg, jnp.zeros((pad,), F32)]).reshape(1, LANES)
    h2, ids, gate = _router(x2, ffn_norm.reshape(1, d), w_r, b_r)
    n_blocks = (2 * t) // MOE_ROWS + MOE_EXPERTS
    row_dst, blk_e, blk_n = _moe_plan(ids[:, :2], n_blocks)
    yk = _moe_experts(h2, row_dst, blk_e, blk_n, w_gate, w_up, w_down, layer, n_blocks)
    return _moe_combine(yk, x2, gate, next_norm.reshape(1, d), want_stream)


def _s5_disc_kernel(lre_ref, lim_ref, ls_ref, bre_ref, bim_ref, are_ref, aim_ref, bbre_ref, bbim_ref):
    lr = jnp.minimum(lre_ref[...], -1e-4)
    li = lim_ref[...]
    step = jnp.exp(ls_ref[...])
    mag = jnp.exp(lr * step)
    ang = li * step
    ab_re = mag * jnp.cos(ang)
    ab_im = mag * jnp.sin(ang)
    den = lr * lr + li * li
    zr = ab_re - 1.0
    f_re = (zr * lr + ab_im * li) / den
    f_im = (ab_im * lr - zr * li) / den
    br = bre_ref[...]
    bi = bim_ref[...]
    are_ref[...] = ab_re
    aim_ref[...] = ab_im
    bbre_ref[...] = f_re * br - f_im * bi
    bbim_ref[...] = f_re * bi + f_im * br


def _s5_discretise(lam_re, lam_im, log_step, b_re, b_im):
    g, p = lam_re.shape
    n = g * p
    col = lambda a: a.reshape(n, 1)
    ls = jnp.broadcast_to(log_step[:, None], (g, p))
    rows = 1024
    full = lambda w: pl.BlockSpec((rows, w), lambda i: (i, 0))
    return pl.pallas_call(
        _s5_disc_kernel,
        grid=(n // rows,),
        in_specs=[full(1), full(1), full(1), full(SSM_GROUP), full(SSM_GROUP)],
        out_specs=[full(1), full(1), full(SSM_GROUP), full(SSM_GROUP)],
        out_shape=[jax.ShapeDtypeStruct((n, 1), F32), jax.ShapeDtypeStruct((n, 1), F32),
                   jax.ShapeDtypeStruct((n, SSM_GROUP), F32), jax.ShapeDtypeStruct((n, SSM_GROUP), F32)],
        compiler_params=_cparams(("parallel",)),
        name="s5_discretise",
    )(col(lam_re), col(lam_im), col(ls), b_re.reshape(n, SSM_GROUP), b_im.reshape(n, SSM_GROUP))


def _cmul(ar, ai, br, bi):
    return ar * br - ai * bi, ar * bi + ai * br


def _s5_kernel(u_ref, bw_ref, cw_ref, a_ref, d_ref, y_ref, bu_scr, xb_scr, carry_scr, il_scr):
    for q in range(S5_BLOCKS_PER_STEP):
        cols = pl.ds(q * LANES, LANES)
        _s5_block(u_ref.at[:, cols], bw_ref.at[q], cw_ref.at[q], a_ref.at[q], d_ref.at[:, cols], y_ref.at[:, cols],
                  bu_scr, xb_scr, carry_scr.at[q], il_scr)


def _s5_block(u_ref, bw_ref, cw_ref, a_ref, d_ref, y_ref, bu_scr, xb_scr, carry_scr, il_scr):
    tau = pl.program_id(2)
    Lb = u_ref.shape[0]
    seg = Lb // SUBLANES
    ns = S5_NSTATE
    ar = a_ref[0:1, :]
    ai = a_ref[1:2, :]

    @pl.when(tau == 0)
    def _():
        carry_scr[...] = jnp.zeros_like(carry_scr)

    pitch = seg + SUBLANES
    for j in range(SUBLANES):
        il_scr[j * pitch:j * pitch + seg, :] = u_ref[j * seg:(j + 1) * seg, :]
    u_perm = jnp.concatenate([il_scr[pl.ds(i, SUBLANES, stride=pitch), :] for i in range(seg)], axis=0)
    bu_scr[...] = _dot(u_perm.astype(BF16), bw_ref[...])

    ar8 = jnp.broadcast_to(ar, (SUBLANES, ns))
    ai8 = jnp.broadcast_to(ai, (SUBLANES, ns))

    def step(i, state):
        xr, xi = state
        r0 = pl.multiple_of(i * SUBLANES, SUBLANES)
        return (ar8 * xr - ai8 * xi + bu_scr[pl.ds(r0, SUBLANES), 0:ns],
                ar8 * xi + ai8 * xr + bu_scr[pl.ds(r0, SUBLANES), ns:2 * ns])

    zeros = jnp.zeros((SUBLANES, ns), F32)
    er, ei = lax.fori_loop(0, seg, step, (zeros, zeros), unroll=8)

    alr, ali = ar, ai
    for _ in range(seg.bit_length() - 1):
        alr, ali = _cmul(alr, ali, alr, ali)
    pr = carry_scr[0:1, 0:ns]
    pi = carry_scr[0:1, ns:2 * ns]
    prs, pis = [], []
    for j in range(SUBLANES):
        prs.append(pr)
        pis.append(pi)
        mr, mi = _cmul(alr, ali, pr, pi)
        pr = er[j:j + 1, :] + mr
        pi = ei[j:j + 1, :] + mi
    carry_scr[0:1, 0:ns] = pr
    carry_scr[0:1, ns:2 * ns] = pi
    p_re = jnp.concatenate(prs, axis=0)
    p_im = jnp.concatenate(pis, axis=0)
    pack = 2 * SUBLANES

    def store_body(i2, state):
        x1 = step(2 * i2, state)
        x2 = step(2 * i2 + 1, x1)
        r0 = pl.multiple_of(i2 * pack, pack)
        xb_scr[pl.ds(r0, pack), 0:ns] = jnp.concatenate([x1[0], x2[0]], axis=0).astype(BF16)
        xb_scr[pl.ds(r0, pack), ns:2 * ns] = jnp.concatenate([x1[1], x2[1]], axis=0).astype(BF16)
        return x2

    lax.fori_loop(0, seg // 2, store_body, (p_re, p_im), unroll=4)

    y = _dot(xb_scr[...], cw_ref[...]) + d_ref[...] * u_perm
    for i in range(seg):
        il_scr[pl.ds(i, SUBLANES, stride=pitch), :] = y[i * SUBLANES:(i + 1) * SUBLANES, :]
    for j in range(SUBLANES):
        y_ref[j * seg:(j + 1) * seg, :] = il_scr[j * pitch:j * pitch + seg, :]


def _s5_scan(h3, bw, cw, a_rows, d_skip):
    b, s, w = h3.shape
    Lb = min(S5_BLOCK, s)
    seg = Lb // SUBLANES
    assert seg & (seg - 1) == 0 and seg >= 2, "segment length must be a power of two"
    nq = S5_BLOCKS_PER_STEP
    wq = nq * LANES
    ns2 = 2 * S5_NSTATE
    return pl.pallas_call(
        _s5_kernel,
        grid=(b, w // wq, s // Lb),
        in_specs=[pl.BlockSpec((None, Lb, wq), lambda i, k, c: (i, c, k)),
                  pl.BlockSpec((nq, LANES, ns2), lambda i, k, c: (k, 0, 0)),
                  pl.BlockSpec((nq, ns2, LANES), lambda i, k, c: (k, 0, 0)),
                  pl.BlockSpec((nq, 2, S5_NSTATE), lambda i, k, c: (k, 0, 0)),
                  pl.BlockSpec((1, wq), lambda i, k, c: (0, k))],
        out_specs=pl.BlockSpec((None, Lb, wq), lambda i, k, c: (i, c, k)),
        out_shape=jax.ShapeDtypeStruct((b, s, w), F32),
        scratch_shapes=[pltpu.VMEM((Lb, ns2), F32), pltpu.VMEM((Lb, ns2), BF16),
                        pltpu.VMEM((nq, SUBLANES, ns2), F32),
                        pltpu.VMEM((Lb + SUBLANES * SUBLANES, LANES), F32)],
        compiler_params=_cparams(("parallel", "parallel", "arbitrary")),
        name="s5_scan",
    )(h3, bw, cw, a_rows, d_skip)


def _glu_kernel(y_ref, x_ref, wv_ref, wg_ref, bv_ref, bg_ref, o_ref, a_scr):
    @pl.when(pl.program_id(1) == 0)
    def _():
        a_scr[...] = jax.nn.gelu(y_ref[...]).astype(BF16)

    a = a_scr[...]
    val = _dot(a, wv_ref[...]) + bv_ref[...]
    gate = _dot(a, wg_ref[...]) + bg_ref[...]
    o_ref[...] = x_ref[...] + val * _sigmoid(gate)


def _glu(y2, x2, w_bf, bias):
    t, d = x2.shape
    tm = min(1024, t)
    tn = 512
    nj = d // tn
    return pl.pallas_call(
        _glu_kernel,
        grid=(t // tm, nj),
        in_specs=[pl.BlockSpec((tm, d), lambda i, j: (i, 0)),
                  pl.BlockSpec((tm, tn), lambda i, j: (i, j)),
                  pl.BlockSpec((d, tn), lambda i, j: (0, j)),
                  pl.BlockSpec((d, tn), lambda i, j: (0, nj + j)),
                  pl.BlockSpec((1, tn), lambda i, j: (0, j)),
                  pl.BlockSpec((1, tn), lambda i, j: (0, nj + j))],
        out_specs=pl.BlockSpec((tm, tn), lambda i, j: (i, j)),
        out_shape=jax.ShapeDtypeStruct((t, d), F32),
        scratch_shapes=[pltpu.VMEM((tm, d), BF16)],
        compiler_params=_cparams(("parallel", "arbitrary")),
        name="glu",
    )(y2, x2, w_bf, w_bf, bias, bias)


def _block_diag(w):
    nb, gb, r, c = w.shape
    eye = jnp.eye(gb, dtype=w.dtype)
    return (w[:, :, :, None, :] * eye[None, :, None, :, None]).reshape(nb, gb * r, gb * c)


def _mixer_s5(h3, x2, lam_re, lam_im, log_step, b_re, b_im, c_re, c_im, d_skip, w_glu, b_glu):
    b, s, w = h3.shape
    g, p = lam_re.shape
    gb = S5_GROUPS_PER_BLOCK
    nb = g // gb
    a_re, a_im, bb_re, bb_im = _s5_discretise(lam_re, lam_im, log_step, b_re, b_im)
    a_rows = jnp.stack([a_re.reshape(nb, gb * p), a_im.reshape(nb, gb * p)], axis=1)
    bt = lambda m: jnp.swapaxes(m.reshape(nb, gb, p, SSM_GROUP), 2, 3)
    bw = jnp.concatenate([_block_diag(bt(bb_re)), _block_diag(bt(bb_im))], axis=2).astype(BF16)
    ct = lambda m: jnp.swapaxes(m.reshape(nb, gb, SSM_GROUP, p), 2, 3)
    cw = jnp.concatenate([_block_diag(ct(c_re)), -_block_diag(ct(c_im))], axis=1).astype(BF16)
    y = _s5_scan(h3, bw, cw, a_rows, d_skip.reshape(1, w))
    return _glu(y.reshape(b * s, w), x2, w_glu.astype(BF16), b_glu.reshape(1, -1))


def _mixer_ab(x2, bsz, norm_w, w_in, a_i_bias, a_f_bias, a_norm, b_conv, b_a_log, b_dt_bias, b_norm, w_out):
    t, d = x2.shape
    s = t // bsz
    zeros = lambda n: jnp.zeros((n,), F32)
    n_a = 2 * A_HEADS * A_QK_DIM + 2 * A_HEADS * A_V_DIM
    n_b = 4 * B_HEADS * B_HEAD_DIM
    a_end = n_a + 2 * A_HEADS
    w_gates = jnp.concatenate([w_in[:, n_a:a_end], w_in[:, a_end + n_b:],
                               jnp.zeros((d, LANES - G_END), F32)], axis=1).astype(BF16)
    w_wide = jnp.concatenate([w_in[:, :n_a], w_in[:, a_end:a_end + n_b]], axis=1).astype(BF16)
    proj, gates = _inproj(x2, norm_w.reshape(1, d), w_wide, w_gates)
    proj = proj.reshape(bsz, s, N_PROJ)
    gates = gates.reshape(bsz, s, LANES)
    gates_t = jnp.swapaxes(gates[:, :, :GATE_ROWS], 1, 2)
    bias = jnp.concatenate([a_i_bias, a_f_bias, zeros(B_HEADS), b_dt_bias, zeros(LANES - G_END)])
    alog = jnp.concatenate([zeros(G_BA), b_a_log, zeros(LANES - G_END)])
    pc, pr = _gates(gates, gates_t, bias.reshape(1, LANES), bias[:GATE_ROWS].reshape(GATE_ROWS, 1),
                    alog.reshape(1, LANES), alog[:GATE_ROWS].reshape(GATE_ROWS, 1))
    ya = _mlstm(proj, pc, pr, a_norm.reshape(1, -1))
    yb = _gdn(proj, pc, pr, b_conv, b_norm.reshape(1, -1))
    n_ya = A_HEADS * A_V_DIM
    return _outproj(ya.reshape(t, n_ya), yb.reshape(t, -1), x2, w_out.astype(BF16))


def kernel(x, mix_norm, ab_w_in, mlstm_i_bias, mlstm_f_bias, mlstm_norm, gdn_conv, gdn_a_log, gdn_dt_bias, gdn_norm, ab_w_out, ssm_lambda_re, ssm_lambda_im, ssm_log_step, ssm_b_re, ssm_b_im, ssm_c_re, ssm_c_im, ssm_d, glu_w, glu_b, ffn_norm, router_group_w, router_group_b, router_expert_w, router_expert_b, expert_w_gate, expert_w_up, expert_w_down, final_norm):
    bsz, s, d = x.shape
    t = bsz * s
    x2 = x.reshape(t, d)
    x2 = _mixer_ab(x2, bsz, mix_norm[0], ab_w_in[0], mlstm_i_bias[0], mlstm_f_bias[0], mlstm_norm[0],
                   gdn_conv[0], gdn_a_log[0], gdn_dt_bias[0], gdn_norm[0], ab_w_out[0])
    x2, h3 = _hier_moe(x2, ffn_norm[0], router_group_w[0], router_group_b[0], router_expert_w[0],
                       router_expert_b[0], expert_w_gate, expert_w_up, expert_w_down, 0, mix_norm[1])
    x2 = _mixer_s5(h3.reshape(bsz, s, d), x2, ssm_lambda_re[0], ssm_lambda_im[0], ssm_log_step[0],
                   ssm_b_re[0], ssm_b_im[0], ssm_c_re[0], ssm_c_im[0], ssm_d[0], glu_w[0], glu_b[0])
    out, = _hier_moe(x2, ffn_norm[1], router_group_w[1], router_group_b[1], router_expert_w[1],
                     router_expert_b[1], expert_w_gate, expert_w_up, expert_w_down, 1, final_norm, want_stream=False)
    return out.reshape(bsz, s, d)
```
